```python
import math
import jax, jax.numpy as jnp
from jax import lax
import numpy as np

D_MODEL = 4096
BATCH = 4
SEQ = 2048
DEPTH = 2
DEC_BATCH = 128
DEC_SEQ = 4
PAST_LEN = 16384
PAGE_SIZE = 128

RWKV_WIDTH = D_MODEL // 2
HEAD_SIZE = 64
RWKV_HEADS = RWKV_WIDTH // HEAD_SIZE
S5_WIDTH = D_MODEL - RWKV_WIDTH
S5_GROUP = 16
S5_GROUPS = S5_WIDTH // S5_GROUP
S5_STATE = 64
W_LORA = 96
A_LORA = 96
G_LORA = 256
RWKV_IN = 3 * RWKV_WIDTH + W_LORA + A_LORA + G_LORA
D_IN = RWKV_IN + S5_WIDTH
D_FF = 11008
EPS_RMS = 1e-6
EPS_GN = 64e-5
S5_CHUNK = 128

kernel_name = "rwkv7_s5_parallel_heads_macaron_step"


def rmsnorm(x, g):
    xf = x.astype(jnp.float32)
    y = xf * lax.rsqrt(jnp.mean(xf * xf, axis=-1, keepdims=True) + EPS_RMS)
    return (y * g.astype(jnp.float32)).astype(x.dtype)


def swiglu(x, w_gate, w_up, w_down):
    return (jax.nn.silu(x @ w_gate) * (x @ w_up)) @ w_down


def project_with_shift(h, h_prev, w_in):
    p = jnp.concatenate([h_prev.astype(h.dtype)[:, None], h], axis=1) @ w_in
    return p[:, 1:], p[:, :-1]


def wkv7_scan(r, decay, k, v, kk, a, s0):
    def step(s, xs):
        r_t, w_t, k_t, v_t, kk_t, a_t = xs
        sa = jnp.einsum('bhij,bhj->bhi', s, -kk_t)
        s = (s * w_t[:, :, None, :]
             + sa[..., None] * (kk_t * a_t)[:, :, None, :]
             + v_t[..., None] * k_t[:, :, None, :])
        y = jnp.einsum('bhij,bhj->bhi', s, r_t)
        return s, y
    xs = tuple(jnp.moveaxis(t, 1, 0) for t in (r, decay, k, v, kk, a))
    s_fin, ys = lax.scan(step, s0, xs)
    return jnp.moveaxis(ys, 0, 1), s_fin


def rwkv7_mix(q_cur, q_prev, s0, mu, w0, w_up, a0, a_up, g_up, k_k, k_a, r_k, lnx_w, lnx_b):
    f32 = jnp.float32
    B, L, _ = q_cur.shape
    mu, w0, w_up, a0, a_up, g_up, k_k, k_a, r_k, lnx_w, lnx_b = (
        t.astype(f32) for t in (mu, w0, w_up, a0, a_up, g_up, k_k, k_a, r_k, lnx_w, lnx_b))
    q = q_cur.astype(f32)
    q = q + (q_prev.astype(f32) - q) * mu
    RW = RWKV_WIDTH
    r, k, v, wd, ad, gd = jnp.split(
        q, [RW, 2 * RW, 3 * RW, 3 * RW + W_LORA, 3 * RW + W_LORA + A_LORA], axis=-1)
    w_log = -jax.nn.softplus(-(w0 + jnp.tanh(wd) @ w_up)) - 0.5
    decay = jnp.exp(-jnp.exp(w_log))
    a = jax.nn.sigmoid(a0 + ad @ a_up)
    g = jax.nn.sigmoid(gd) @ g_up
    kk = k * k_k
    k = k * (1.0 + (a - 1.0) * k_a)
    heads = lambda t: t.reshape(B, L, RWKV_HEADS, HEAD_SIZE)
    r, k, v, kk, a, decay = map(heads, (r, k, v, kk, a, decay))
    kk = kk / jnp.maximum(jnp.sqrt(jnp.sum(kk * kk, axis=-1, keepdims=True)), 1e-12)
    y, s_fin = wkv7_scan(r, decay, k, v, kk, a, s0.astype(f32))
    mean = jnp.mean(y, axis=-1, keepdims=True)
    var = jnp.mean(jnp.square(y - mean), axis=-1, keepdims=True)
    yn = ((y - mean) * lax.rsqrt(var + EPS_GN)).reshape(B, L, RW) * lnx_w + lnx_b
    bonus = jnp.sum(r * k * r_k, axis=-1, keepdims=True) * v
    out = (yn + bonus.reshape(B, L, RW)) * g
    return out.astype(q_cur.dtype), s_fin


def s5_combine(e1, e2):
    a1r, a1i, b1r, b1i = e1
    a2r, a2i, b2r, b2i = e2
    return (a2r * a1r - a2i * a1i, a2r * a1i + a2i * a1r,
            a2r * b1r - a2i * b1i + b2r, a2r * b1i + a2i * b1r + b2i)


def s5_scan(uf, abar_re, abar_im, bbar_re, bbar_im, c_re, c_im, h0_re, h0_im):
    B, L, G, C = uf.shape
    P = abar_re.shape[-1]
    chunk = math.gcd(L, S5_CHUNK)
    n = L // chunk
    a_r = jnp.broadcast_to(abar_re[None, None], (chunk, 1, G, P))
    a_i = jnp.broadcast_to(abar_im[None, None], (chunk, 1, G, P))
    u_blocks = jnp.transpose(uf, (1, 0, 2, 3)).reshape(n, chunk, B, G, C)

    def step(carry, u_blk):
        hr, hi = carry
        bu_r = jnp.einsum('tbgc,gpc->tbgp', u_blk, bbar_re)
        bu_i = jnp.einsum('tbgc,gpc->tbgp', u_blk, bbar_im)
        pr, pi, lr, li = lax.associative_scan(s5_combine, (a_r, a_i, bu_r, bu_i), axis=0)
        out_r = lr + pr * hr - pi * hi
        out_i = li + pr * hi + pi * hr
        y = (jnp.einsum('gcp,tbgp->tbgc', c_re, out_r)
             - jnp.einsum('gcp,tbgp->tbgc', c_im, out_i))
        return (out_r[-1], out_i[-1]), y

    (hr, hi), ys = lax.scan(step, (h0_re, h0_im), u_blocks)
    ys = jnp.transpose(ys.reshape(L, B, G, C), (1, 0, 2, 3))
    return ys, hr, hi


def s5_mix(u, h0_re, h0_im, lam_re, lam_im, b_re, b_im, c_re, c_im, d, log_step, w_glu, b_glu):
    f32 = jnp.float32
    B, L, _ = u.shape
    lam_re, lam_im, b_re, b_im, c_re, c_im, d, log_step, w_glu, b_glu = (
        t.astype(f32) for t in (lam_re, lam_im, b_re, b_im, c_re, c_im, d, log_step, w_glu, b_glu))
    uf = u.astype(f32).reshape(B, L, S5_GROUPS, S5_GROUP)
    step = jnp.exp(log_step)[:, None]
    mag = jnp.exp(lam_re * step)
    abar_re = mag * jnp.cos(lam_im * step)
    abar_im = mag * jnp.sin(lam_im * step)
    den = lam_re * lam_re + lam_im * lam_im
    nr = abar_re - 1.0
    f_re = (nr * lam_re + abar_im * lam_im) / den
    f_im = (abar_im * lam_re - nr * lam_im) / den
    bbar_re = f_re[..., None] * b_re - f_im[..., None] * b_im
    bbar_im = f_re[..., None] * b_im + f_im[..., None] * b_re
    y, hr, hi = s5_scan(uf, abar_re, abar_im, bbar_re, bbar_im, c_re, c_im,
                        h0_re.astype(f32), h0_im.astype(f32))
    y = (y + d.reshape(S5_GROUPS, S5_GROUP) * uf).reshape(B, L, S5_WIDTH)
    z = jax.nn.gelu(y, approximate=False)
    out = z * jax.nn.sigmoid(z @ w_glu + b_glu)
    return out.astype(u.dtype), hr, hi


def run_trunk(x, shift_state, wkv_state, ssm_re_state, ssm_im_state, p):
    new_shift, new_wkv, new_re, new_im = [], [], [], []
    for l in range(DEPTH):
        h = rmsnorm(x, p['ffn1_norm'][l])
        x = x + 0.5 * swiglu(h, p['ffn1_w_gate'][l], p['ffn1_w_up'][l], p['ffn1_w_down'][l])
        h_mix = rmsnorm(x, p['mix_norm'][l])
        q_cur, q_prev = project_with_shift(h_mix, shift_state[l], p['w_in'][l])
        y_rw, s_fin = rwkv7_mix(q_cur[..., :RWKV_IN], q_prev[..., :RWKV_IN], wkv_state[l],
                                p['shift_mu'][l], p['rw_w0'][l], p['rw_w_up'][l], p['rw_a0'][l],
                                p['rw_a_up'][l], p['rw_g_up'][l], p['rw_k_k'][l], p['rw_k_a'][l],
                                p['rw_r_k'][l], p['rw_lnx_w'][l], p['rw_lnx_b'][l])
        y_s5, h_re, h_im = s5_mix(q_cur[..., RWKV_IN:], ssm_re_state[l], ssm_im_state[l],
                                  p['s5_lam_re'][l], p['s5_lam_im'][l], p['s5_b_re'][l],
                                  p['s5_b_im'][l], p['s5_c_re'][l], p['s5_c_im'][l], p['s5_d'][l],
                                  p['s5_log_step'][l], p['s5_w_glu'][l], p['s5_b_glu'][l])
        y_s5 = rmsnorm(y_s5, p['s5_out_norm'][l])
        merged = jnp.concatenate([y_rw.astype(x.dtype), y_s5.astype(x.dtype)], axis=-1)
        x = x + merged @ p['w_out'][l]
        h = rmsnorm(x, p['ffn2_norm'][l])
        x = x + 0.5 * swiglu(h, p['ffn2_w_gate'][l], p['ffn2_w_up'][l], p['ffn2_w_down'][l])
        new_shift.append(h_mix[:, -1])
        new_wkv.append(s_fin)
        new_re.append(h_re)
        new_im.append(h_im)
    y = rmsnorm(x, p['final_norm'])
    return y, jnp.stack(new_shift), jnp.stack(new_wkv), jnp.stack(new_re), jnp.stack(new_im)


def setup_inputs(seed: int = 0) -> dict:
    key = jax.random.key(seed)
    ks = iter(jax.random.split(key, 64))
    f32 = jnp.float32

    def nrm(shape, scale):
        return jax.random.normal(next(ks), shape, f32) * scale

    def gain(shape):
        return 1.0 + nrm(shape, 0.02)

    RW = RWKV_WIDTH
    G, P, C = S5_GROUPS, S5_STATE, S5_GROUP
    n_lin = jnp.arange(RW, dtype=f32) / (RW - 1)
    ratio = jnp.arange(DEPTH, dtype=f32) / max(DEPTH - 1, 1)
    w0 = -6.5 + 5.0 * n_lin[None] ** (0.85 + jnp.sqrt(ratio)[:, None]) + nrm((DEPTH, RW), 0.1)
    lam_im = jnp.broadcast_to(jnp.pi * jnp.arange(P, dtype=f32), (DEPTH, G, P)) + nrm((DEPTH, G, P), 0.01)
    return {
        'x_prompt': nrm((BATCH, SEQ, D_MODEL), 1.0),
        'x_sample': nrm((DEC_BATCH, DEC_SEQ, D_MODEL), 1.0),
        'state_shift': nrm((DEPTH, DEC_BATCH, D_MODEL), 1.0),
        'state_wkv': nrm((DEPTH, DEC_BATCH, RWKV_HEADS, HEAD_SIZE, HEAD_SIZE), 0.3),
        'state_ssm_re': nrm((DEPTH, DEC_BATCH, G, P), 0.3),
        'state_ssm_im': nrm((DEPTH, DEC_BATCH, G, P), 0.3),
        'ffn1_norm': gain((DEPTH, D_MODEL)),
        'ffn1_w_gate': nrm((DEPTH, D_MODEL, D_FF), D_MODEL ** -0.5),
        'ffn1_w_up': nrm((DEPTH, D_MODEL, D_FF), D_MODEL ** -0.5),
        'ffn1_w_down': nrm((DEPTH, D_FF, D_MODEL), D_FF ** -0.5),
        'mix_norm': gain((DEPTH, D_MODEL)),
        'w_in': nrm((DEPTH, D_MODEL, D_IN), D_MODEL ** -0.5),
        'shift_mu': jax.random.uniform(next(ks), (DEPTH, RWKV_IN), f32),
        'rw_w0': w0,
        'rw_w_up': nrm((DEPTH, W_LORA, RW), 0.1 * W_LORA ** -0.5),
        'rw_a0': nrm((DEPTH, RW), 0.1),
        'rw_a_up': nrm((DEPTH, A_LORA, RW), A_LORA ** -0.5),
        'rw_g_up': nrm((DEPTH, G_LORA, RW), G_LORA ** -0.5),
        'rw_k_k': 0.85 + nrm((DEPTH, RW), 0.02),
        'rw_k_a': 1.0 + nrm((DEPTH, RW), 0.02),
        'rw_r_k': nrm((DEPTH, RWKV_HEADS, HEAD_SIZE), 0.1),
        'rw_lnx_w': gain((DEPTH, RW)),
        'rw_lnx_b': nrm((DEPTH, RW), 0.01),
        's5_lam_re': -0.5 + nrm((DEPTH, G, P), 0.01),
        's5_lam_im': lam_im,
        's5_b_re': nrm((DEPTH, G, P, C), (2 * C) ** -0.5),
        's5_b_im': nrm((DEPTH, G, P, C), (2 * C) ** -0.5),
        's5_c_re': nrm((DEPTH, G, C, P), (2 * P) ** -0.5),
        's5_c_im': nrm((DEPTH, G, C, P), (2 * P) ** -0.5),
        's5_d': nrm((DEPTH, S5_WIDTH), 1.0),
        's5_log_step': jax.random.uniform(next(ks), (DEPTH, G), f32,
                                          minval=math.log(0.001), maxval=math.log(0.1)),
        's5_w_glu': nrm((DEPTH, S5_WIDTH, S5_WIDTH), S5_WIDTH ** -0.5),
        's5_b_glu': nrm((DEPTH, S5_WIDTH), 0.01),
        's5_out_norm': gain((DEPTH, S5_WIDTH)),
        'w_out': nrm((DEPTH, D_MODEL, D_MODEL), D_MODEL ** -0.5),
        'ffn2_norm': gain((DEPTH, D_MODEL)),
        'ffn2_w_gate': nrm((DEPTH, D_MODEL, D_FF), D_MODEL ** -0.5),
        'ffn2_w_up': nrm((DEPTH, D_MODEL, D_FF), D_MODEL ** -0.5),
        'ffn2_w_down': nrm((DEPTH, D_FF, D_MODEL), D_FF ** -0.5),
        'final_norm': gain((D_MODEL,)),
    }


def reference(x_prompt, x_sample, state_shift, state_wkv, state_ssm_re, state_ssm_im,
              ffn1_norm, ffn1_w_gate, ffn1_w_up, ffn1_w_down,
              mix_norm, w_in, shift_mu,
              rw_w0, rw_w_up, rw_a0, rw_a_up, rw_g_up, rw_k_k, rw_k_a, rw_r_k, rw_lnx_w, rw_lnx_b,
              s5_lam_re, s5_lam_im, s5_b_re, s5_b_im, s5_c_re, s5_c_im, s5_d, s5_log_step,
              s5_w_glu, s5_b_glu, s5_out_norm,
              w_out,
              ffn2_norm, ffn2_w_gate, ffn2_w_up, ffn2_w_down,
              final_norm):
    p = dict(ffn1_norm=ffn1_norm, ffn1_w_gate=ffn1_w_gate, ffn1_w_up=ffn1_w_up, ffn1_w_down=ffn1_w_down,
             mix_norm=mix_norm, w_in=w_in, shift_mu=shift_mu,
             rw_w0=rw_w0, rw_w_up=rw_w_up, rw_a0=rw_a0, rw_a_up=rw_a_up, rw_g_up=rw_g_up,
             rw_k_k=rw_k_k, rw_k_a=rw_k_a, rw_r_k=rw_r_k, rw_lnx_w=rw_lnx_w, rw_lnx_b=rw_lnx_b,
             s5_lam_re=s5_lam_re, s5_lam_im=s5_lam_im, s5_b_re=s5_b_re, s5_b_im=s5_b_im,
             s5_c_re=s5_c_re, s5_c_im=s5_c_im, s5_d=s5_d, s5_log_step=s5_log_step,
             s5_w_glu=s5_w_glu, s5_b_glu=s5_b_glu, s5_out_norm=s5_out_norm,
             w_out=w_out,
             ffn2_norm=ffn2_norm, ffn2_w_gate=ffn2_w_gate, ffn2_w_up=ffn2_w_up, ffn2_w_down=ffn2_w_down,
             final_norm=final_norm)
    B = x_prompt.shape[0]
    f32 = jnp.float32
    shift0 = jnp.zeros((DEPTH, B, D_MODEL), x_prompt.dtype)
    wkv0 = jnp.zeros((DEPTH, B, RWKV_HEADS, HEAD_SIZE, HEAD_SIZE), f32)
    re0 = jnp.zeros((DEPTH, B, S5_GROUPS, S5_STATE), f32)
    im0 = jnp.zeros((DEPTH, B, S5_GROUPS, S5_STATE), f32)
    y_prompt, shift_p, wkv_p, ssm_re_p, ssm_im_p = run_trunk(x_prompt, shift0, wkv0, re0, im0, p)
    y_sample, shift_s, wkv_s, ssm_re_s, ssm_im_s = run_trunk(
        x_sample, state_shift, state_wkv, state_ssm_re, state_ssm_im, p)
    return (y_prompt, y_sample, shift_p, wkv_p, ssm_re_p, ssm_im_p, shift_s, wkv_s, ssm_re_s, ssm_im_s)
```

```python
import functools
import math

import jax
import jax.numpy as jnp
from jax import lax
from jax.experimental import pallas as pl
from jax.experimental.pallas import tpu as pltpu

F32 = jnp.float32
BF16 = jnp.bfloat16

D_MODEL = 4096
BATCH = 4
SEQ = 2048
DEPTH = 2
DEC_BATCH = 128
DEC_SEQ = 4
M_PROMPT = BATCH * SEQ
M_SAMPLE = DEC_BATCH * DEC_SEQ
M_ALL = M_PROMPT + M_SAMPLE

RW = D_MODEL // 2
HEAD = 64
HEADS = RW // HEAD
S5W = D_MODEL - RW
S5_C = 16
S5_G = S5W // S5_C
S5_P = 64
W_LORA = 96
A_LORA = 96
G_LORA = 256
LORA_PAD = 128
D_FF = 11008
EPS_RMS = 1e-6
EPS_GN = 64e-5

COL_LORA = 3 * RW
COL_S5 = COL_LORA + 2 * LORA_PAD + G_LORA
D_INP = COL_S5 + S5W

V7X_VMEM_BYTES = 64 * 1024 * 1024
LANES = 128

S5_GT = 8
S5_TILES = S5_G // S5_GT
S5_UW = S5_GT * S5_C
S5_HW = S5_GT * S5_P


def _cparams(semantics, vmem_mib):
    assert vmem_mib * 1024 * 1024 < V7X_VMEM_BYTES
    return pltpu.CompilerParams(dimension_semantics=semantics,
                                vmem_limit_bytes=vmem_mib * 1024 * 1024)


def _rmsnorm_kernel(x_ref, g_ref, o_ref):
    x = x_ref[...]
    ms = jnp.mean(x * x, axis=-1, keepdims=True)
    o_ref[...] = (x * lax.rsqrt(ms + EPS_RMS) * g_ref[...]).astype(o_ref.dtype)


def _rmsnorm(x, g, out_dtype, tm):
    m, d = x.shape
    return pl.pallas_call(
        _rmsnorm_kernel,
        grid=(m // tm,),
        in_specs=[pl.BlockSpec((tm, d), lambda i: (i, 0)),
                  pl.BlockSpec((1, d), lambda i: (0, 0))],
        out_specs=pl.BlockSpec((tm, d), lambda i: (i, 0)),
        out_shape=jax.ShapeDtypeStruct((m, d), out_dtype),
        compiler_params=_cparams(("parallel",), 40),
        name="rmsnorm",
    )(x, g.reshape(1, d))


def _gate_up_kernel(h_ref, wg_ref, wu_ref, o_ref):
    h = h_ref[...]
    a = jnp.dot(h, wg_ref[...], preferred_element_type=F32)
    b = jnp.dot(h, wu_ref[...], preferred_element_type=F32)
    o_ref[...] = (a * jax.nn.sigmoid(a) * b).astype(o_ref.dtype)


def _gate_up(h, wg, wu, tm=1088, tn=256):
    m, d = h.shape
    f = wg.shape[1]
    return pl.pallas_call(
        _gate_up_kernel,
        grid=(m // tm, f // tn),
        in_specs=[pl.BlockSpec((tm, d), lambda i, j: (i, 0)),
                  pl.BlockSpec((d, tn), lambda i, j: (0, j)),
                  pl.BlockSpec((d, tn), lambda i, j: (0, j))],
        out_specs=pl.BlockSpec((tm, tn), lambda i, j: (i, j)),
        out_shape=jax.ShapeDtypeStruct((m, f), BF16),
        compiler_params=_cparams(("parallel", "arbitrary"), 48),
        name="ffn_gate_up",
    )(h, wg, wu)


def _res_matmul_kernel(*refs, n_pairs, scale):
    a_refs = refs[:n_pairs]
    w_refs = refs[n_pairs:2 * n_pairs]
    x_ref, o_ref = refs[2 * n_pairs], refs[2 * n_pairs + 1]
    acc = jnp.dot(a_refs[0][...], w_refs[0][...], preferred_element_type=F32)
    for a_ref, w_ref in zip(a_refs[1:], w_refs[1:]):
        acc = acc + jnp.dot(a_ref[...], w_ref[...], preferred_element_type=F32)
    if scale != 1.0:
        acc = scale * acc
    o_ref[...] = x_ref[...] + acc


def _res_matmul(a_list, w, x, scale, tm, tn, vmem_mib):
    m, n = x.shape
    n_pairs = len(a_list)
    in_specs = [pl.BlockSpec((tm, a.shape[1]), lambda i, j: (i, 0)) for a in a_list]
    for p, a in enumerate(a_list):
        in_specs.append(pl.BlockSpec((a.shape[1], tn), lambda i, j, p=p: (p, j)))
    in_specs.append(pl.BlockSpec((tm, tn), lambda i, j: (i, j)))
    assert sum(a.shape[1] for a in a_list) == w.shape[0]
    return pl.pallas_call(
        functools.partial(_res_matmul_kernel, n_pairs=n_pairs, scale=scale),
        grid=(m // tm, n // tn),
        in_specs=in_specs,
        out_specs=pl.BlockSpec((tm, tn), lambda i, j: (i, j)),
        out_shape=jax.ShapeDtypeStruct((m, n), F32),
        compiler_params=_cparams(("parallel", "arbitrary"), vmem_mib),
        name="res_matmul",
    )(*a_list, *([w] * n_pairs), x)


def _ffn(x, norm_g, wg, wu, wd):
    h = _rmsnorm(x, norm_g, BF16, 256)
    act = _gate_up(h, wg.astype(BF16), wu.astype(BF16))
    return _res_matmul([act], wd.astype(BF16), x, 0.5, tm=544, tn=512, vmem_mib=58)


WIN_TM = 512
WIN_TN = 512
WIN_PROMPT_TILES = M_PROMPT // WIN_TM
WIN_TILES_PER_SEQ = SEQ // WIN_TM


def _win_kernel(h_ref, h8_ref, hp_ref, w_ref, mu_ref, o_ref):
    i = pl.program_id(1)
    w = w_ref[...]
    mu = mu_ref[...]
    p = jnp.dot(h_ref[...], w, preferred_element_type=F32)

    @pl.when(i < WIN_PROMPT_TILES)
    def _():
        p8 = jnp.dot(h8_ref[...], w, preferred_element_type=F32)
        first = jnp.where(i % WIN_TILES_PER_SEQ == 0, 0.0, p8[7:8, :])
        rows = lax.broadcasted_iota(jnp.int32, p.shape, 0)
        prev = jnp.where(rows == 0, first, pltpu.roll(p, 1, 0))
        o_ref[...] = p + (prev - p) * mu

    @pl.when(i == WIN_PROMPT_TILES)
    def _():
        p0 = jnp.dot(hp_ref[...], w, preferred_element_type=F32)
        prev = jnp.concatenate([p0, p[:M_SAMPLE - DEC_BATCH]], axis=0)
        o_ref[...] = p + (prev - p) * mu


def _win(h, h_prev_sample, w_pad, mu_pad):
    tm, tn = WIN_TM, WIN_TN
    last8 = tm // 8
    return pl.pallas_call(
        _win_kernel,
        grid=(D_INP // tn, M_ALL // tm),
        in_specs=[pl.BlockSpec((tm, D_MODEL), lambda j, i: (i, 0)),
                  pl.BlockSpec((8, D_MODEL), lambda j, i: (jnp.maximum(i * last8 - 1, 0), 0)),
                  pl.BlockSpec((DEC_BATCH, D_MODEL), lambda j, i: (0, 0)),
                  pl.BlockSpec((D_MODEL, tn), lambda j, i: (0, j)),
                  pl.BlockSpec((1, tn), lambda j, i: (0, j))],
        out_specs=pl.BlockSpec((tm, tn), lambda j, i: (i, j)),
        out_shape=jax.ShapeDtypeStruct((M_ALL, D_INP), F32),
        compiler_params=_cparams(("parallel", "arbitrary"), 48),
        name="in_proj_shift",
    )(h, h, h_prev_sample, w_pad, mu_pad)


def _pad_in_proj(w_in, mu):
    o_w, o_a, o_g, o_s = 3 * RW, 3 * RW + W_LORA, 3 * RW + W_LORA + A_LORA, 3 * RW + W_LORA + A_LORA + G_LORA
    zw = jnp.zeros((D_MODEL, LORA_PAD - W_LORA), w_in.dtype)
    w_pad = jnp.concatenate([w_in[:, :o_w], w_in[:, o_w:o_a], zw, w_in[:, o_a:o_g], zw,
                             w_in[:, o_g:o_s], w_in[:, o_s:]], axis=1).astype(BF16)
    zm = jnp.zeros((LORA_PAD - W_LORA,), mu.dtype)
    mu_pad = jnp.concatenate([mu[:o_w], mu[o_w:o_a], zm, mu[o_a:o_g], zm, mu[o_g:o_s],
                              jnp.zeros((S5W,), mu.dtype)]).reshape(1, D_INP)
    return w_pad, mu_pad


def _head_sum(x):
    r = lax.broadcasted_iota(jnp.int32, (LANES, LANES), 0) // HEAD
    c = lax.broadcasted_iota(jnp.int32, (LANES, LANES), 1) // HEAD
    ones = (r == c).astype(BF16)
    hi = x.astype(BF16)
    r1 = x - hi.astype(F32)
    mid = r1.astype(BF16)
    lo = (r1 - mid.astype(F32)).astype(BF16)
    outs = []
    for s in range(x.shape[1] // LANES):
        sl = slice(s * LANES, (s + 1) * LANES)
        acc = jnp.dot(hi[:, sl], ones, preferred_element_type=F32)
        acc = acc + jnp.dot(mid[:, sl], ones, preferred_element_type=F32)
        acc = acc + jnp.dot(lo[:, sl], ones, preferred_element_type=F32)
        outs.append(acc)
    return jnp.concatenate(outs, axis=1)


def _rwkv_pre_kernel(r_ref, k_ref, v_ref, lora_ref, w0_ref, a0_ref, kk_ref, ka_ref, rk_ref,
                     wup_ref, aup_ref, gup_ref,
                     w_out, k_out, kk_out, b_out, g_out, bonus_out):
    r = r_ref[...]
    k = k_ref[...]
    lora = lora_ref[...]
    wd = lora[:, :LORA_PAD]
    ad = lora[:, LORA_PAD:2 * LORA_PAD]
    gd = lora[:, 2 * LORA_PAD:]
    z = -(w0_ref[...] + jnp.dot(jnp.tanh(wd).astype(BF16), wup_ref[...], preferred_element_type=F32))
    softplus = jnp.maximum(z, 0.0) + jnp.log1p(jnp.exp(-jnp.abs(z)))
    w_out[...] = jnp.exp(-jnp.exp(-softplus - 0.5))
    a = jax.nn.sigmoid(a0_ref[...] + jnp.dot(ad.astype(BF16), aup_ref[...], preferred_element_type=F32))
    g_out[...] = jnp.dot(jax.nn.sigmoid(gd).astype(BF16), gup_ref[...], preferred_element_type=F32)
    kk = k * kk_ref[...]
    k2 = k * (1.0 + (a - 1.0) * ka_ref[...])
    k_out[...] = k2
    kkn = kk / jnp.maximum(jnp.sqrt(_head_sum(kk * kk)), 1e-12)
    kk_out[...] = kkn
    b_out[...] = kkn * a
    bonus_out[...] = _head_sum(r * k2 * rk_ref[...]) * v_ref[...]


def _rwkv_pre(q, w0, a0, k_k, k_a, r_k, w_up, a_up, g_up, tm=256):
    m = q.shape[0]
    row = lambda v: v.reshape(1, RW)
    pad_rows = lambda u: jnp.concatenate(
        [u, jnp.zeros((LORA_PAD - u.shape[0], RW), u.dtype)], axis=0).astype(BF16)
    vec = pl.BlockSpec((1, RW), lambda i: (0, 0))
    full = lambda rows: pl.BlockSpec((rows, RW), lambda i: (0, 0))
    out = jax.ShapeDtypeStruct((m, RW), F32)
    return pl.pallas_call(
        _rwkv_pre_kernel,
        grid=(m // tm,),
        in_specs=[pl.BlockSpec((tm, RW), lambda i: (i, 0)),
                  pl.BlockSpec((tm, RW), lambda i: (i, 1)),
                  pl.BlockSpec((tm, RW), lambda i: (i, 2)),
                  pl.BlockSpec((tm, 2 * LORA_PAD + G_LORA), lambda i: (i, COL_LORA // (2 * LORA_PAD + G_LORA))),
                  vec, vec, vec, vec, vec, full(LORA_PAD), full(LORA_PAD), full(G_LORA)],
        out_specs=[pl.BlockSpec((tm, RW), lambda i: (i, 0))] * 6,
        out_shape=[out] * 6,
        compiler_params=_cparams(("parallel",), 48),
        name="rwkv_pre",
    )(q, q, q, q, row(w0), row(a0), row(k_k), row(k_a), row(r_k.reshape(RW)),
      pad_rows(w_up), pad_rows(a_up), g_up.astype(BF16))


def _rwkv_post_kernel(y_ref, bonus_ref, g_ref, lw_ref, lb_ref, o_ref):
    y = y_ref[...]
    mean = _head_sum(y) * (1.0 / HEAD)
    c = y - mean
    var = _head_sum(c * c) * (1.0 / HEAD)
    yn = c * lax.rsqrt(var + EPS_GN) * lw_ref[...] + lb_ref[...]
    o_ref[...] = ((yn + bonus_ref[...]) * g_ref[...]).astype(o_ref.dtype)


def _rwkv_post(y, bonus, g, lnx_w, lnx_b, tm=256):
    m = y.shape[0]
    blk = pl.BlockSpec((tm, RW), lambda i: (i, 0))
    vec = pl.BlockSpec((1, RW), lambda i: (0, 0))
    return pl.pallas_call(
        _rwkv_post_kernel,
        grid=(m // tm,),
        in_specs=[blk, blk, blk, vec, vec],
        out_specs=blk,
        out_shape=jax.ShapeDtypeStruct((m, RW), BF16),
        compiler_params=_cparams(("parallel",), 40),
        name="rwkv_post",
    )(y, bonus, g, lnx_w.reshape(1, RW), lnx_b.reshape(1, RW))


def _wkv_kernel(r_ref, w_ref, k_ref, v_ref, kk_ref, b_ref, s0_ref, y_ref, s_ref, wr_ref, *, steps):
    @pl.when(pl.program_id(1) == 0)
    def _():
        s_ref[...] = s0_ref[...]

    def step(t, carry):
        r = r_ref[t]
        wr_ref[...] = w_ref[t] * r
        beta = jnp.sum(b_ref[t] * r, axis=0, keepdims=True)
        kappa = jnp.sum(k_ref[t] * r, axis=0, keepdims=True)
        v = v_ref[t]
        sa = jnp.zeros((HEAD, LANES), F32)
        u = jnp.zeros((HEAD, LANES), F32)
        for j in range(HEAD):
            sj = s_ref[j]
            sa = sa - sj * kk_ref[t, pl.ds(j, 1), :]
            u = u + sj * wr_ref[pl.ds(j, 1), :]
        for j in range(HEAD):
            s_ref[j] = (s_ref[j] * w_ref[t, pl.ds(j, 1), :] + sa * b_ref[t, pl.ds(j, 1), :]
                        + v * k_ref[t, pl.ds(j, 1), :])
        y_ref[t] = u + sa * beta + v * kappa
        return carry

    lax.fori_loop(0, steps, step, 0)


def _wkv_scan(r, w, k, v, kk, b, s0, tc):
    length, _, n = r.shape
    vec = pl.BlockSpec((tc, HEAD, LANES), lambda c, t: (t, 0, c))
    st = pl.BlockSpec((HEAD, HEAD, LANES), lambda c, t: (0, 0, c))
    return pl.pallas_call(
        functools.partial(_wkv_kernel, steps=tc),
        grid=(n // LANES, length // tc),
        in_specs=[vec] * 6 + [st],
        out_specs=[vec, st],
        out_shape=[jax.ShapeDtypeStruct((length, HEAD, n), F32),
                   jax.ShapeDtypeStruct((HEAD, HEAD, n), F32)],
        scratch_shapes=[pltpu.VMEM((HEAD, LANES), F32)],
        compiler_params=_cparams(("parallel", "arbitrary"), 40),
        name="wkv_scan",
    )(r, w, k, v, kk, b, s0)


def _to_scan_prompt(a):
    return a[:M_PROMPT].reshape(BATCH, SEQ, HEADS, HEAD).transpose(1, 3, 0, 2).reshape(SEQ, HEAD, BATCH * HEADS)


def _to_scan_sample(a):
    return a[M_PROMPT:].reshape(DEC_SEQ, DEC_BATCH, HEADS, HEAD).transpose(0, 3, 1, 2).reshape(
        DEC_SEQ, HEAD, DEC_BATCH * HEADS)


def _from_scan_prompt(y):
    return y.reshape(SEQ, HEAD, BATCH, HEADS).transpose(2, 0, 3, 1).reshape(M_PROMPT, RW)


def _from_scan_sample(y):
    return y.reshape(DEC_SEQ, HEAD, DEC_BATCH, HEADS).transpose(0, 2, 3, 1).reshape(M_SAMPLE, RW)


def _s5_discretize(lre_ref, lim_ref, ls_ref, bre_ref, bim_ref):
    lre, lim = lre_ref[...], lim_ref[...]
    step = jnp.exp(ls_ref[...])
    mag = jnp.exp(lre * step)
    ar = mag * jnp.cos(lim * step)
    ai = mag * jnp.sin(lim * step)
    den = lre * lre + lim * lim
    nr = ar - 1.0
    f_re = (nr * lre + ai * lim) / den
    f_im = (ai * lre - nr * lim) / den
    bre, bim = bre_ref[...], bim_ref[...]
    return ar, ai, f_re * bre - f_im * bim, f_re * bim + f_im * bre


def _dot_f32(a, b):
    return jnp.dot(a, b, precision=lax.Precision.HIGHEST, preferred_element_type=F32)


def _s5_prompt_kernel(u_ref, lre_ref, lim_ref, ls_ref, bre_ref, bim_ref, cre_ref, cim_ref, d_ref,
                      y_ref, hre_ref, him_ref, hr_scr, hi_scr):
    ar, ai, bbr, bbi = _s5_discretize(lre_ref, lim_ref, ls_ref, bre_ref, bim_ref)
    u = u_ref[...]
    hr_scr[...] = _dot_f32(u, bbr)
    hi_scr[...] = _dot_f32(u, bbi)

    def body(t, carry):
        hr, hi = carry
        nr = ar * hr - ai * hi + hr_scr[pl.ds(t, 1), :]
        ni = ar * hi + ai * hr + hi_scr[pl.ds(t, 1), :]
        hr_scr[pl.ds(t, 1), :] = nr
        hi_scr[pl.ds(t, 1), :] = ni
        return nr, ni

    zero = jnp.zeros((1, S5_HW), F32)
    hr, hi = lax.fori_loop(0, SEQ, body, (zero, zero), unroll=8)
    hre_ref[...] = hr
    him_ref[...] = hi
    y_ref[...] = _dot_f32(hr_scr[...], cre_ref[...]) - _dot_f32(hi_scr[...], cim_ref[...]) + d_ref[...] * u


def _s5_sample_kernel(u_ref, h0r_ref, h0i_ref, lre_ref, lim_ref, ls_ref, bre_ref, bim_ref, cre_ref, cim_ref,
                      d_ref, y_ref, hre_ref, him_ref, hr_scr, hi_scr):
    ar, ai, bbr, bbi = _s5_discretize(lre_ref, lim_ref, ls_ref, bre_ref, bim_ref)
    u = u_ref[...]
    bu_r = _dot_f32(u, bbr)
    bu_i = _dot_f32(u, bbi)
    hr, hi = h0r_ref[...], h0i_ref[...]
    for t in range(DEC_SEQ):
        rows = slice(t * DEC_BATCH, (t + 1) * DEC_BATCH)
        hr, hi = ar * hr - ai * hi + bu_r[rows], ar * hi + ai * hr + bu_i[rows]
        hr_scr[rows, :] = hr
        hi_scr[rows, :] = hi
    hre_ref[...] = hr
    him_ref[...] = hi
    y_ref[...] = _dot_f32(hr_scr[...], cre_ref[...]) - _dot_f32(hi_scr[...], cim_ref[...]) + d_ref[...] * u


def _s5_params(lam_re, lam_im, log_step, b_re, b_im, c_re, c_im, d):
    eye = jnp.eye(S5_GT, dtype=F32)
    chan = lambda a: a.reshape(S5_TILES, 1, S5_HW)

    def b_blk(b):
        x = b.transpose(0, 2, 1).reshape(S5_TILES, S5_GT, S5_C, S5_P)
        return jnp.einsum('tgcp,gh->tgchp', x, eye).reshape(S5_TILES, S5_UW, S5_HW)

    def c_blk(c):
        x = c.transpose(0, 2, 1).reshape(S5_TILES, S5_GT, S5_P, S5_C)
        return jnp.einsum('tgpc,gh->tgphc', x, eye).reshape(S5_TILES, S5_HW, S5_UW)

    ls = jnp.broadcast_to(log_step[:, None], (S5_G, S5_P))
    return (chan(lam_re), chan(lam_im), chan(ls), b_blk(b_re), b_blk(b_im), c_blk(c_re), c_blk(c_im),
            d.reshape(S5_TILES, 1, S5_UW))


def _s5_param_specs(tile_of):
    chan = pl.BlockSpec((None, 1, S5_HW), lambda *g: (tile_of(*g), 0, 0))
    bspec = pl.BlockSpec((None, S5_UW, S5_HW), lambda *g: (tile_of(*g), 0, 0))
    cspec = pl.BlockSpec((None, S5_HW, S5_UW), lambda *g: (tile_of(*g), 0, 0))
    dspec = pl.BlockSpec((None, 1, S5_UW), lambda *g: (tile_of(*g), 0, 0))
    return [chan, chan, chan, bspec, bspec, cspec, cspec, dspec]


def _s5_prompt(q, params):
    col0 = COL_S5 // S5_UW
    state = jax.ShapeDtypeStruct((BATCH, 1, S5_G * S5_P), F32)
    st_spec = pl.BlockSpec((None, 1, S5_HW), lambda b, g: (b, 0, g))
    return pl.pallas_call(
        _s5_prompt_kernel,
        grid=(BATCH, S5_TILES),
        in_specs=[pl.BlockSpec((SEQ, S5_UW), lambda b, g: (b, col0 + g))] + _s5_param_specs(lambda b, g: g),
        out_specs=[pl.BlockSpec((SEQ, S5_UW), lambda b, g: (b, g)), st_spec, st_spec],
        out_shape=[jax.ShapeDtypeStruct((M_PROMPT, S5W), F32), state, state],
        scratch_shapes=[pltpu.VMEM((SEQ, S5_HW), F32), pltpu.VMEM((SEQ, S5_HW), F32)],
        compiler_params=_cparams(("parallel", "parallel"), 48),
        name="s5_prompt",
    )(q, *params)


def _s5_sample(q, h0_re, h0_im, params):
    col0 = COL_S5 // S5_UW
    row0 = M_PROMPT // M_SAMPLE
    state = jax.ShapeDtypeStruct((DEC_BATCH, S5_G * S5_P), F32)
    st_spec = pl.BlockSpec((DEC_BATCH, S5_HW), lambda g: (0, g))
    return pl.pallas_call(
        _s5_sample_kernel,
        grid=(S5_TILES,),
        in_specs=[pl.BlockSpec((M_SAMPLE, S5_UW), lambda g: (row0, col0 + g)), st_spec, st_spec]
                 + _s5_param_specs(lambda g: g),
        out_specs=[pl.BlockSpec((M_SAMPLE, S5_UW), lambda g: (0, g)), st_spec, st_spec],
        out_shape=[jax.ShapeDtypeStruct((M_SAMPLE, S5W), F32), state, state],
        scratch_shapes=[pltpu.VMEM((M_SAMPLE, S5_HW), F32), pltpu.VMEM((M_SAMPLE, S5_HW), F32)],
        compiler_params=_cparams(("parallel",), 32),
        name="s5_sample",
    )(q, h0_re, h0_im, *params)


def _glu_norm_kernel(y_ref, w_ref, b_ref, g_ref, o_ref):
    y = y_ref[...]
    z = 0.5 * y * (1.0 + lax.erf(y * math.sqrt(0.5)))
    gate = jax.nn.sigmoid(jnp.dot(z.astype(BF16), w_ref[...], preferred_element_type=F32) + b_ref[...])
    out = z * gate
    ms = jnp.mean(out * out, axis=-1, keepdims=True)
    o_ref[...] = (out * lax.rsqrt(ms + EPS_RMS) * g_ref[...]).astype(o_ref.dtype)


def _glu_norm(y, w_glu, b_glu, gain, tm=256):
    m = y.shape[0]
    vec = pl.BlockSpec((1, S5W), lambda i: (0, 0))
    return pl.pallas_call(
        _glu_norm_kernel,
        grid=(m // tm,),
        in_specs=[pl.BlockSpec((tm, S5W), lambda i: (i, 0)),
                  pl.BlockSpec((S5W, S5W), lambda i: (0, 0)), vec, vec],
        out_specs=pl.BlockSpec((tm, S5W), lambda i: (i, 0)),
        out_shape=jax.ShapeDtypeStruct((m, S5W), BF16),
        compiler_params=_cparams(("parallel",), 48),
        name="s5_glu_norm",
    )(y, w_glu.astype(BF16), b_glu.reshape(1, S5W), gain.reshape(1, S5W))


def kernel(x_prompt, x_sample, state_shift, state_wkv, state_ssm_re, state_ssm_im, ffn1_norm, ffn1_w_gate, ffn1_w_up, ffn1_w_down, mix_norm, w_in, shift_mu, rw_w0, rw_w_up, rw_a0, rw_a_up, rw_g_up, rw_k_k, rw_k_a, rw_r_k, rw_lnx_w, rw_lnx_b, s5_lam_re, s5_lam_im, s5_b_re, s5_b_im, s5_c_re, s5_c_im, s5_d, s5_log_step, s5_w_glu, s5_b_glu, s5_out_norm, w_out, ffn2_norm, ffn2_w_gate, ffn2_w_up, ffn2_w_down, final_norm):
    x = jnp.concatenate([x_prompt.reshape(M_PROMPT, D_MODEL),
                         x_sample.transpose(1, 0, 2).reshape(M_SAMPLE, D_MODEL)], axis=0)
    zero_state = jnp.zeros((HEAD, HEAD, BATCH * HEADS), F32)
    shift_p, wkv_p, re_p, im_p, shift_s, wkv_s, re_s, im_s = ([] for _ in range(8))

    for l in range(DEPTH):
        x = _ffn(x, ffn1_norm[l], ffn1_w_gate[l], ffn1_w_up[l], ffn1_w_down[l])

        h_mix = _rmsnorm(x, mix_norm[l], BF16, 256)
        n_last = BATCH + DEC_BATCH
        x_last = jnp.concatenate([x[:M_PROMPT].reshape(BATCH, SEQ, D_MODEL)[:, -1],
                                  x[M_ALL - DEC_BATCH:],
                                  jnp.zeros((-n_last % 8, D_MODEL), F32)], axis=0)
        h_last = _rmsnorm(x_last, mix_norm[l], F32, x_last.shape[0])
        shift_p.append(h_last[:BATCH])
        shift_s.append(h_last[BATCH:n_last])

        w_pad, mu_pad = _pad_in_proj(w_in[l], shift_mu[l])
        q = _win(h_mix, state_shift[l].astype(BF16), w_pad, mu_pad)

        dec, k2, kkn, bvec, gate, bonus = _rwkv_pre(
            q, rw_w0[l], rw_a0[l], rw_k_k[l], rw_k_a[l], rw_r_k[l], rw_w_up[l], rw_a_up[l], rw_g_up[l])
        r_cols, v_cols = q[:, :RW], q[:, 2 * RW:3 * RW]
        scan_in = (r_cols, dec, k2, v_cols, kkn, bvec)
        y_p, s_p = _wkv_scan(*[_to_scan_prompt(a) for a in scan_in], zero_state, tc=32)
        s0 = state_wkv[l].transpose(3, 2, 0, 1).reshape(HEAD, HEAD, DEC_BATCH * HEADS)
        y_s, s_s = _wkv_scan(*[_to_scan_sample(a) for a in scan_in], s0, tc=DEC_SEQ)
        wkv_p.append(s_p.reshape(HEAD, HEAD, BATCH, HEADS).transpose(2, 3, 1, 0))
        wkv_s.append(s_s.reshape(HEAD, HEAD, DEC_BATCH, HEADS).transpose(2, 3, 1, 0))
        y_wkv = jnp.concatenate([_from_scan_prompt(y_p), _from_scan_sample(y_s)], axis=0)
        y_rw = _rwkv_post(y_wkv, bonus, gate, rw_lnx_w[l], rw_lnx_b[l])

        s5p = _s5_params(s5_lam_re[l], s5_lam_im[l], s5_log_step[l], s5_b_re[l], s5_b_im[l],
                         s5_c_re[l], s5_c_im[l], s5_d[l])
        y5_p, hre_p, him_p = _s5_prompt(q, s5p)
        y5_s, hre_s, him_s = _s5_sample(q, state_ssm_re[l].reshape(DEC_BATCH, S5_G * S5_P),
                                        state_ssm_im[l].reshape(DEC_BATCH, S5_G * S5_P), s5p)
        re_p.append(hre_p.reshape(BATCH, S5_G, S5_P))
        im_p.append(him_p.reshape(BATCH, S5_G, S5_P))
        re_s.append(hre_s.reshape(DEC_BATCH, S5_G, S5_P))
        im_s.append(him_s.reshape(DEC_BATCH, S5_G, S5_P))
        y_s5 = _glu_norm(jnp.concatenate([y5_p, y5_s], axis=0), s5_w_glu[l], s5_b_glu[l], s5_out_norm[l])

        x = _res_matmul([y_rw, y_s5], w_out[l].astype(BF16), x, 1.0, tm=1088, tn=512, vmem_mib=48)
        x = _ffn(x, ffn2_norm[l], ffn2_w_gate[l], ffn2_w_up[l], ffn2_w_down[l])

    y = _rmsnorm(x, final_norm, F32, 256)
    y_prompt = y[:M_PROMPT].reshape(BATCH, SEQ, D_MODEL)
    y_sample = y[M_PROMPT:].reshape(DEC_SEQ, DEC_BATCH, D_MODEL).transpose(1, 0, 2)
    st = jnp.stack
    return (y_prompt, y_sample, st(shift_p), st(wkv_p), st(re_p), st(im_p),
            st(shift_s), st(wkv_s), st(re_s), st(im_s))
```

```python
import functools
import math

import jax
import jax.numpy as jnp
from jax import lax
from jax.experimental import pallas as pl
from jax.experimental.pallas import tpu as pltpu

F32 = jnp.float32
BF16 = jnp.bfloat16

D_MODEL = 4096
BATCH = 4
SEQ = 2048
DEPTH = 2
DEC_BATCH = 128
DEC_SEQ = 4
M_PROMPT = BATCH * SEQ
M_SAMPLE = DEC_BATCH * DEC_SEQ
M_ALL = M_PROMPT + M_SAMPLE

RW = D_MODEL // 2
HEAD = 64
HEADS = RW // HEAD
S5W = D_MODEL - RW
S5_C = 16
S5_G = S5W // S5_C
S5_P = 64
W_LORA = 96
A_LORA = 96
G_LORA = 256
LORA_PAD = 128
LORA_W = 2 * LORA_PAD + G_LORA
D_FF = 11008
EPS_RMS = 1e-6
EPS_GN = 64e-5

COL_LORA = 3 * RW
COL_S5 = COL_LORA + LORA_W
D_INP = COL_S5 + S5W

V7X_VMEM_BYTES = 64 * 1024 * 1024
LANES = 128
SUBLANES = 8

S5_GT = 8
S5_TILES = S5_G // S5_GT
S5_UW = S5_GT * S5_C
S5_HW = S5_GT * S5_P

ROW_TM = 256


def _cparams(semantics, vmem_mib):
    assert vmem_mib * 1024 * 1024 < V7X_VMEM_BYTES
    return pltpu.CompilerParams(dimension_semantics=semantics,
                                vmem_limit_bytes=vmem_mib * 1024 * 1024)


def _joined_out(in_specs, args, joined):
    if joined is None:
        return in_specs, args, {}
    return (in_specs + [pl.BlockSpec(memory_space=pl.ANY)], args + [joined], {len(args): 0})


def _rmsnorm_kernel(x_ref, g_ref, o_ref):
    x = x_ref[...]
    ms = jnp.mean(x * x, axis=-1, keepdims=True)
    o_ref[...] = (x * lax.rsqrt(ms + EPS_RMS) * g_ref[...]).astype(o_ref.dtype)


def _rmsnorm(x, g, out_dtype, tm):
    m, d = x.shape
    return pl.pallas_call(
        _rmsnorm_kernel,
        grid=(m // tm,),
        in_specs=[pl.BlockSpec((tm, d), lambda i: (i, 0)),
                  pl.BlockSpec((1, d), lambda i: (0, 0))],
        out_specs=pl.BlockSpec((tm, d), lambda i: (i, 0)),
        out_shape=jax.ShapeDtypeStruct((m, d), out_dtype),
        compiler_params=_cparams(("parallel",), 40),
        name="rmsnorm",
    )(x, g.reshape(1, d))


def _gate_up_kernel(h_ref, wg_ref, wu_ref, o_ref):
    h = h_ref[...]
    a = jnp.dot(h, wg_ref[...].astype(BF16), preferred_element_type=F32)
    b = jnp.dot(h, wu_ref[...].astype(BF16), preferred_element_type=F32)
    o_ref[...] = (a * jax.nn.sigmoid(a) * b).astype(o_ref.dtype)


def _gate_up(h, wg, wu, tm=1088, tn=256):
    m, d = h.shape
    f = wg.shape[1]
    return pl.pallas_call(
        _gate_up_kernel,
        grid=(m // tm, f // tn),
        in_specs=[pl.BlockSpec((tm, d), lambda i, j: (i, 0)),
                  pl.BlockSpec((d, tn), lambda i, j: (0, j)),
                  pl.BlockSpec((d, tn), lambda i, j: (0, j))],
        out_specs=pl.BlockSpec((tm, tn), lambda i, j: (i, j)),
        out_shape=jax.ShapeDtypeStruct((m, f), BF16),
        compiler_params=_cparams(("parallel", "arbitrary"), 56),
        name="ffn_gate_up",
    )(h, wg, wu)


def _res_matmul_kernel(*refs, n_pairs, scale):
    a_refs = refs[:n_pairs]
    w_refs = refs[n_pairs:2 * n_pairs]
    x_ref, o_ref = refs[2 * n_pairs], refs[2 * n_pairs + 1]
    acc = jnp.dot(a_refs[0][...], w_refs[0][...], preferred_element_type=F32)
    for a_ref, w_ref in zip(a_refs[1:], w_refs[1:]):
        acc = acc + jnp.dot(a_ref[...], w_ref[...], preferred_element_type=F32)
    if scale != 1.0:
        acc = scale * acc
    o_ref[...] = x_ref[...] + acc


def _res_matmul(a_list, w, x, scale, tm, tn, vmem_mib):
    m, n = x.shape
    n_pairs = len(a_list)
    in_specs = [pl.BlockSpec((tm, a.shape[1]), lambda i, j: (i, 0)) for a in a_list]
    for p, a in enumerate(a_list):
        in_specs.append(pl.BlockSpec((a.shape[1], tn), lambda i, j, p=p: (p, j)))
    in_specs.append(pl.BlockSpec((tm, tn), lambda i, j: (i, j)))
    assert sum(a.shape[1] for a in a_list) == w.shape[0]
    return pl.pallas_call(
        functools.partial(_res_matmul_kernel, n_pairs=n_pairs, scale=scale),
        grid=(m // tm, n // tn),
        in_specs=in_specs,
        out_specs=pl.BlockSpec((tm, tn), lambda i, j: (i, j)),
        out_shape=jax.ShapeDtypeStruct((m, n), F32),
        compiler_params=_cparams(("parallel", "arbitrary"), vmem_mib),
        name="res_matmul",
    )(*a_list, *([w] * n_pairs), x)


def _ffn(x, norm_g, wg, wu, wd):
    h = _rmsnorm(x, norm_g, BF16, ROW_TM)
    act = _gate_up(h, wg, wu)
    return _res_matmul([act], wd.astype(BF16), x, 0.5, tm=544, tn=512, vmem_mib=58)


WIN_TN = 256
WIN_SHIFT_TILES = COL_S5 // WIN_TN


def _win_prompt_kernel(h_ref, w_ref, mu_ref, o_ref):
    p = jnp.dot(h_ref[...], w_ref[...], preferred_element_type=F32)

    @pl.when(pl.program_id(1) < WIN_SHIFT_TILES)
    def _():
        rows = lax.broadcasted_iota(jnp.int32, p.shape, 0)
        prev = jnp.where(rows == 0, 0.0, pltpu.roll(p, 1, 0))
        o_ref[...] = p + (prev - p) * mu_ref[...]

    @pl.when(pl.program_id(1) >= WIN_SHIFT_TILES)
    def _():
        o_ref[...] = p


def _win_sample_kernel(h_ref, w_ref, mu_ref, o_ref):
    p = jnp.dot(h_ref[...], w_ref[...], preferred_element_type=F32)
    cur, prev = p[DEC_BATCH:], p[:M_SAMPLE]
    o_ref[...] = cur + (prev - cur) * mu_ref[...]


def _win_prompt(h, w_pad, mu_pad):
    tn = WIN_TN
    return pl.pallas_call(
        _win_prompt_kernel,
        grid=(BATCH, D_INP // tn),
        in_specs=[pl.BlockSpec((SEQ, D_MODEL), lambda b, j: (b, 0), pipeline_mode=pl.Buffered(1)),
                  pl.BlockSpec((D_MODEL, tn), lambda b, j: (0, j)),
                  pl.BlockSpec((1, tn), lambda b, j: (0, j))],
        out_specs=pl.BlockSpec((SEQ, tn), lambda b, j: (b, j)),
        out_shape=jax.ShapeDtypeStruct((M_PROMPT, D_INP), F32),
        compiler_params=_cparams(("parallel", "arbitrary"), 48),
        name="in_proj_prompt",
    )(h, w_pad, mu_pad)


def _win_sample(h_rows, w_pad, mu_pad):
    tn = WIN_TN
    rows = h_rows.shape[0]
    return pl.pallas_call(
        _win_sample_kernel,
        grid=(D_INP // tn,),
        in_specs=[pl.BlockSpec((rows, D_MODEL), lambda j: (0, 0)),
                  pl.BlockSpec((D_MODEL, tn), lambda j: (0, j)),
                  pl.BlockSpec((1, tn), lambda j: (0, j))],
        out_specs=pl.BlockSpec((M_SAMPLE, tn), lambda j: (0, j)),
        out_shape=jax.ShapeDtypeStruct((M_SAMPLE, D_INP), F32),
        compiler_params=_cparams(("parallel",), 32),
        name="in_proj_sample",
    )(h_rows, w_pad, mu_pad)


def _pad_in_proj(w_in, mu):
    o_w, o_a, o_g, o_s = 3 * RW, 3 * RW + W_LORA, 3 * RW + W_LORA + A_LORA, 3 * RW + W_LORA + A_LORA + G_LORA
    zw = jnp.zeros((D_MODEL, LORA_PAD - W_LORA), w_in.dtype)
    w_pad = jnp.concatenate([w_in[:, :o_w], w_in[:, o_w:o_a], zw, w_in[:, o_a:o_g], zw,
                             w_in[:, o_g:o_s], w_in[:, o_s:]], axis=1).astype(BF16)
    zm = jnp.zeros((LORA_PAD - W_LORA,), mu.dtype)
    mu_pad = jnp.concatenate([mu[:o_w], mu[o_w:o_a], zm, mu[o_a:o_g], zm, mu[o_g:o_s],
                              jnp.zeros((S5W,), mu.dtype)]).reshape(1, D_INP)
    return w_pad, mu_pad


def _split_bf16(x):
    hi = x.astype(BF16)
    return hi, (x - hi.astype(F32)).astype(BF16)


def _head_sum(x):
    r = lax.broadcasted_iota(jnp.int32, (LANES, LANES), 0) // HEAD
    c = lax.broadcasted_iota(jnp.int32, (LANES, LANES), 1) // HEAD
    ones = (r == c).astype(BF16)
    hi = x.astype(BF16)
    r1 = x - hi.astype(F32)
    mid = r1.astype(BF16)
    lo = (r1 - mid.astype(F32)).astype(BF16)
    outs = []
    for s in range(x.shape[1] // LANES):
        sl = slice(s * LANES, (s + 1) * LANES)
        acc = jnp.dot(hi[:, sl], ones, preferred_element_type=F32)
        acc = acc + jnp.dot(mid[:, sl], ones, preferred_element_type=F32)
        acc = acc + jnp.dot(lo[:, sl], ones, preferred_element_type=F32)
        outs.append(acc)
    return jnp.concatenate(outs, axis=1)


def _rwkv_pre_kernel(r_ref, k_ref, v_ref, lora_ref, w0_ref, a0_ref, kk_ref, ka_ref, rk_ref,
                     wup_ref, aup_ref, gup_ref,
                     r_out, w_out, k_out, v_out, kk_out, b_out, g_out, bonus_out):
    r = r_ref[...]
    k = k_ref[...]
    v = v_ref[...]
    r_out[...] = r
    v_out[...] = v
    lora = lora_ref[...]
    wd = lora[:, :LORA_PAD]
    ad = lora[:, LORA_PAD:2 * LORA_PAD]
    gd = lora[:, 2 * LORA_PAD:]
    z = -(w0_ref[...] + jnp.dot(jnp.tanh(wd).astype(BF16), wup_ref[...], preferred_element_type=F32))
    softplus = jnp.maximum(z, 0.0) + jnp.log1p(jnp.exp(-jnp.abs(z)))
    w_out[...] = jnp.exp(-jnp.exp(-softplus - 0.5))
    a = jax.nn.sigmoid(a0_ref[...] + jnp.dot(ad.astype(BF16), aup_ref[...], preferred_element_type=F32))
    g_out[...] = jnp.dot(jax.nn.sigmoid(gd).astype(BF16), gup_ref[...], preferred_element_type=F32)
    kk = k * kk_ref[...]
    k2 = k * (1.0 + (a - 1.0) * ka_ref[...])
    k_out[...] = k2
    kkn = kk / jnp.maximum(jnp.sqrt(_head_sum(kk * kk)), 1e-12)
    kk_out[...] = kkn
    b_out[...] = kkn * a
    bonus_out[...] = _head_sum(r * k2 * rk_ref[...]) * v


def _rwkv_pre(q, w0, a0, k_k, k_a, r_k, w_up, a_up, g_up, tm=ROW_TM):
    m = q.shape[0]
    row = lambda v: v.reshape(1, RW)
    pad_rows = lambda u: jnp.concatenate(
        [u, jnp.zeros((LORA_PAD - u.shape[0], RW), u.dtype)], axis=0).astype(BF16)
    vec = pl.BlockSpec((1, RW), lambda i: (0, 0))
    full = lambda rows: pl.BlockSpec((rows, RW), lambda i: (0, 0))
    out = jax.ShapeDtypeStruct((m, RW), F32)
    return pl.pallas_call(
        _rwkv_pre_kernel,
        grid=(m // tm,),
        in_specs=[pl.BlockSpec((tm, RW), lambda i: (i, 0)),
                  pl.BlockSpec((tm, RW), lambda i: (i, 1)),
                  pl.BlockSpec((tm, RW), lambda i: (i, 2)),
                  pl.BlockSpec((tm, LORA_W), lambda i: (i, COL_LORA // LORA_W)),
                  vec, vec, vec, vec, vec, full(LORA_PAD), full(LORA_PAD), full(G_LORA)],
        out_specs=[pl.BlockSpec((tm, RW), lambda i: (i, 0))] * 8,
        out_shape=[out] * 8,
        compiler_params=_cparams(("parallel",), 56),
        name="rwkv_pre",
    )(q, q, q, q, row(w0), row(a0), row(k_k), row(k_a), row(r_k.reshape(RW)),
      pad_rows(w_up), pad_rows(a_up), g_up.astype(BF16))


def _rwkv_post_kernel(y_ref, bonus_ref, g_ref, lw_ref, lb_ref, *rest):
    o_ref = rest[-1]
    y = y_ref[...]
    mean = _head_sum(y) * (1.0 / HEAD)
    c = y - mean
    var = _head_sum(c * c) * (1.0 / HEAD)
    yn = c * lax.rsqrt(var + EPS_GN) * lw_ref[...] + lb_ref[...]
    o_ref[...] = ((yn + bonus_ref[...]) * g_ref[...]).astype(o_ref.dtype)


def _rwkv_post(y, bonus, g, lnx_w, lnx_b, row0, joined=None, tm=ROW_TM):
    m = y.shape[0]
    blk = pl.BlockSpec((tm, RW), lambda i: (i, 0))
    vec = pl.BlockSpec((1, RW), lambda i: (0, 0))
    in_specs, args, aliases = _joined_out(
        [blk, blk, blk, vec, vec], [y, bonus, g, lnx_w.reshape(1, RW), lnx_b.reshape(1, RW)], joined)
    return pl.pallas_call(
        _rwkv_post_kernel,
        grid=(m // tm,),
        in_specs=in_specs,
        out_specs=pl.BlockSpec((tm, RW), lambda i: (i + row0 // tm, 0)),
        out_shape=jax.ShapeDtypeStruct((M_ALL, RW), BF16),
        input_output_aliases=aliases,
        compiler_params=_cparams(("parallel",), 40),
        name="rwkv_post",
    )(*args)


WKV_UNROLL = 8


def _wkv_kernel(r_ref, w_ref, k_ref, v_ref, kk_ref, b_ref, s0_ref, y_ref, s_ref, wr_ref, *, steps):
    @pl.when(pl.program_id(1) == 0)
    def _():
        s_ref[...] = s0_ref[...]

    def step(t, carry):
        r = r_ref[t]
        wr_ref[...] = w_ref[t] * r
        beta = jnp.sum(b_ref[t] * r, axis=0, keepdims=True)
        kappa = jnp.sum(k_ref[t] * r, axis=0, keepdims=True)
        v = v_ref[t]

        def contract(j, acc):
            sa, u = acc
            sj = s_ref[j]
            return sa - sj * kk_ref[t, pl.ds(j, 1), :], u + sj * wr_ref[pl.ds(j, 1), :]

        zero = jnp.zeros((HEAD, LANES), F32)
        sa, u = lax.fori_loop(0, HEAD, contract, (zero, zero), unroll=WKV_UNROLL)

        def update(j, c):
            s_ref[j] = (s_ref[j] * w_ref[t, pl.ds(j, 1), :] + sa * b_ref[t, pl.ds(j, 1), :]
                        + v * k_ref[t, pl.ds(j, 1), :])
            return c

        lax.fori_loop(0, HEAD, update, 0, unroll=WKV_UNROLL)
        y_ref[t] = u + sa * beta + v * kappa
        return carry

    lax.fori_loop(0, steps, step, 0)


def _wkv_scan(r, w, k, v, kk, b, s0, tc):
    length, _, n = r.shape
    vec = pl.BlockSpec((tc, HEAD, LANES), lambda c, t: (t, 0, c))
    st = pl.BlockSpec((HEAD, HEAD, LANES), lambda c, t: (0, 0, c))
    return pl.pallas_call(
        functools.partial(_wkv_kernel, steps=tc),
        grid=(n // LANES, length // tc),
        in_specs=[vec] * 6 + [st],
        out_specs=[vec, st],
        out_shape=[jax.ShapeDtypeStruct((length, HEAD, n), F32),
                   jax.ShapeDtypeStruct((HEAD, HEAD, n), F32)],
        scratch_shapes=[pltpu.VMEM((HEAD, LANES), F32)],
        compiler_params=_cparams(("parallel", "arbitrary"), 40),
        name="wkv_scan",
    )(r, w, k, v, kk, b, s0)


def _to_scan_prompt(a):
    return a.reshape(BATCH, SEQ, HEADS, HEAD).transpose(1, 3, 0, 2).reshape(SEQ, HEAD, BATCH * HEADS)


def _to_scan_sample(a):
    return a.reshape(DEC_SEQ, DEC_BATCH, HEADS, HEAD).transpose(0, 3, 1, 2).reshape(
        DEC_SEQ, HEAD, DEC_BATCH * HEADS)


def _from_scan_prompt(y):
    return y.reshape(SEQ, HEAD, BATCH, HEADS).transpose(2, 0, 3, 1).reshape(M_PROMPT, RW)


def _from_scan_sample(y):
    return y.reshape(DEC_SEQ, HEAD, DEC_BATCH, HEADS).transpose(0, 2, 3, 1).reshape(M_SAMPLE, RW)


def _s5_discretize(lre_ref, lim_ref, ls_ref, bre_ref, bim_ref):
    lre, lim = lre_ref[...], lim_ref[...]
    step = jnp.exp(ls_ref[...])
    mag = jnp.exp(lre * step)
    ar = mag * jnp.cos(lim * step)
    ai = mag * jnp.sin(lim * step)
    den = lre * lre + lim * lim
    nr = ar - 1.0
    f_re = (nr * lre + ai * lim) / den
    f_im = (ai * lre - nr * lim) / den
    bre, bim = bre_ref[...], bim_ref[...]
    return ar, ai, f_re * bre - f_im * bim, f_re * bim + f_im * bre


def _dot3(a, b):
    a_hi, a_lo = _split_bf16(a)
    b_hi, b_lo = _split_bf16(b)
    acc = jnp.dot(a_hi, b_hi, preferred_element_type=F32)
    acc = acc + jnp.dot(a_lo, b_hi, preferred_element_type=F32)
    return acc + jnp.dot(a_hi, b_lo, preferred_element_type=F32)


def _s5_readout(hr, hi, cre_ref, cim_ref, d_ref, u):
    y = jnp.dot(hr.astype(BF16), cre_ref[...].astype(BF16), preferred_element_type=F32)
    y = y - jnp.dot(hi.astype(BF16), cim_ref[...].astype(BF16), preferred_element_type=F32)
    return y + d_ref[...] * u


def _cmul(ar, ai, br, bi):
    return ar * br - ai * bi, ar * bi + ai * br


def _s5_prompt_kernel(u_ref, lre_ref, lim_ref, ls_ref, bre_ref, bim_ref, cre_ref, cim_ref, d_ref,
                      y_ref, hre_ref, him_ref, hr_scr, hi_scr):
    ar, ai, bbr, bbi = _s5_discretize(lre_ref, lim_ref, ls_ref, bre_ref, bim_ref)
    u = u_ref[...]
    hr_scr[...] = _dot3(u, bbr)
    hi_scr[...] = _dot3(u, bbi)

    rows = lax.broadcasted_iota(jnp.int32, (SUBLANES, S5_HW), 0)
    powers = [(ar, ai)]
    for _ in range(SUBLANES - 1):
        powers.append(_cmul(*powers[-1], ar, ai))
    levels = []
    for sh in (1, 2, 4):
        pr, pi = powers[sh - 1]
        levels.append((sh, jnp.where(rows >= sh, pr, 0.0), jnp.where(rows >= sh, pi, 0.0)))
    cpr = jnp.zeros((SUBLANES, S5_HW), F32)
    cpi = jnp.zeros((SUBLANES, S5_HW), F32)
    for n, (pr, pi) in enumerate(powers):
        cpr = jnp.where(rows == n, pr, cpr)
        cpi = jnp.where(rows == n, pi, cpi)

    def tile(i, carry):
        cr, ci = carry
        r0 = pl.multiple_of(i * SUBLANES, SUBLANES)
        xr = hr_scr[pl.ds(r0, SUBLANES), :]
        xi = hi_scr[pl.ds(r0, SUBLANES), :]
        for sh, mr, mi in levels:
            sr, si = pltpu.roll(xr, sh, 0), pltpu.roll(xi, sh, 0)
            xr, xi = xr + (mr * sr - mi * si), xi + (mr * si + mi * sr)
        xr = xr + (cpr * cr - cpi * ci)
        xi = xi + (cpr * ci + cpi * cr)
        hr_scr[pl.ds(r0, SUBLANES), :] = xr
        hi_scr[pl.ds(r0, SUBLANES), :] = xi
        last = SUBLANES - 1
        return (jnp.broadcast_to(xr[last:, :], (SUBLANES, S5_HW)),
                jnp.broadcast_to(xi[last:, :], (SUBLANES, S5_HW)))

    zero = jnp.zeros((SUBLANES, S5_HW), F32)
    cr, ci = lax.fori_loop(0, SEQ // SUBLANES, tile, (zero, zero), unroll=2)
    hre_ref[...] = cr[:1]
    him_ref[...] = ci[:1]
    y_ref[...] = _s5_readout(hr_scr[...], hi_scr[...], cre_ref, cim_ref, d_ref, u)


def _s5_sample_kernel(u_ref, h0r_ref, h0i_ref, lre_ref, lim_ref, ls_ref, bre_ref, bim_ref, cre_ref, cim_ref,
                      d_ref, y_ref, hre_ref, him_ref, hr_scr, hi_scr):
    ar, ai, bbr, bbi = _s5_discretize(lre_ref, lim_ref, ls_ref, bre_ref, bim_ref)
    u = u_ref[...]
    bu_r = _dot3(u, bbr)
    bu_i = _dot3(u, bbi)
    hr, hi = h0r_ref[...], h0i_ref[...]
    for t in range(DEC_SEQ):
        rows = slice(t * DEC_BATCH, (t + 1) * DEC_BATCH)
        hr, hi = ar * hr - ai * hi + bu_r[rows], ar * hi + ai * hr + bu_i[rows]
        hr_scr[rows, :] = hr
        hi_scr[rows, :] = hi
    hre_ref[...] = hr
    him_ref[...] = hi
    y_ref[...] = _s5_readout(hr_scr[...], hi_scr[...], cre_ref, cim_ref, d_ref, u)


def _s5_params(lam_re, lam_im, log_step, b_re, b_im, c_re, c_im, d):
    eye = jnp.eye(S5_GT, dtype=F32)
    chan = lambda a: a.reshape(S5_TILES, 1, S5_HW)

    def b_blk(b):
        x = b.transpose(0, 2, 1).reshape(S5_TILES, S5_GT, S5_C, S5_P)
        return jnp.einsum('tgcp,gh->tgchp', x, eye).reshape(S5_TILES, S5_UW, S5_HW)

    def c_blk(c):
        x = c.transpose(0, 2, 1).reshape(S5_TILES, S5_GT, S5_P, S5_C)
        return jnp.einsum('tgpc,gh->tgphc', x, eye).reshape(S5_TILES, S5_HW, S5_UW)

    ls = jnp.broadcast_to(log_step[:, None], (S5_G, S5_P))
    return (chan(lam_re), chan(lam_im), chan(ls), b_blk(b_re), b_blk(b_im), c_blk(c_re), c_blk(c_im),
            d.reshape(S5_TILES, 1, S5_UW))


def _s5_param_specs(tile_of):
    chan = pl.BlockSpec((None, 1, S5_HW), lambda *g: (tile_of(*g), 0, 0))
    bspec = pl.BlockSpec((None, S5_UW, S5_HW), lambda *g: (tile_of(*g), 0, 0))
    cspec = pl.BlockSpec((None, S5_HW, S5_UW), lambda *g: (tile_of(*g), 0, 0))
    dspec = pl.BlockSpec((None, 1, S5_UW), lambda *g: (tile_of(*g), 0, 0))
    return [chan, chan, chan, bspec, bspec, cspec, cspec, dspec]


def _s5_prompt(q, params):
    col0 = COL_S5 // S5_UW
    state = jax.ShapeDtypeStruct((BATCH, 1, S5_G * S5_P), F32)
    st_spec = pl.BlockSpec((None, 1, S5_HW), lambda b, g: (b, 0, g))
    return pl.pallas_call(
        _s5_prompt_kernel,
        grid=(BATCH, S5_TILES),
        in_specs=[pl.BlockSpec((SEQ, S5_UW), lambda b, g: (b, col0 + g))] + _s5_param_specs(lambda b, g: g),
        out_specs=[pl.BlockSpec((SEQ, S5_UW), lambda b, g: (b, g)), st_spec, st_spec],
        out_shape=[jax.ShapeDtypeStruct((M_PROMPT, S5W), F32), state, state],
        scratch_shapes=[pltpu.VMEM((SEQ, S5_HW), F32), pltpu.VMEM((SEQ, S5_HW), F32)],
        compiler_params=_cparams(("parallel", "parallel"), 48),
        name="s5_prompt",
    )(q, *params)


def _s5_sample(q, h0_re, h0_im, params):
    col0 = COL_S5 // S5_UW
    state = jax.ShapeDtypeStruct((DEC_BATCH, S5_G * S5_P), F32)
    st_spec = pl.BlockSpec((DEC_BATCH, S5_HW), lambda g: (0, g))
    return pl.pallas_call(
        _s5_sample_kernel,
        grid=(S5_TILES,),
        in_specs=[pl.BlockSpec((M_SAMPLE, S5_UW), lambda g: (0, col0 + g)), st_spec, st_spec]
                 + _s5_param_specs(lambda g: g),
        out_specs=[pl.BlockSpec((M_SAMPLE, S5_UW), lambda g: (0, g)), st_spec, st_spec],
        out_shape=[jax.ShapeDtypeStruct((M_SAMPLE, S5W), F32), state, state],
        scratch_shapes=[pltpu.VMEM((M_SAMPLE, S5_HW), F32), pltpu.VMEM((M_SAMPLE, S5_HW), F32)],
        compiler_params=_cparams(("parallel",), 32),
        name="s5_sample",
    )(q, h0_re, h0_im, *params)


def _glu_norm_kernel(y_ref, w_ref, b_ref, g_ref, *rest):
    o_ref = rest[-1]
    y = y_ref[...]
    z = 0.5 * y * (1.0 + lax.erf(y * math.sqrt(0.5)))
    gate = jax.nn.sigmoid(jnp.dot(z.astype(BF16), w_ref[...], preferred_element_type=F32) + b_ref[...])
    out = z * gate
    ms = jnp.mean(out * out, axis=-1, keepdims=True)
    o_ref[...] = (out * lax.rsqrt(ms + EPS_RMS) * g_ref[...]).astype(o_ref.dtype)


def _glu_norm(y, w_glu_bf16, b_glu, gain, row0, joined=None, tm=ROW_TM):
    m = y.shape[0]
    vec = pl.BlockSpec((1, S5W), lambda i: (0, 0))
    in_specs, args, aliases = _joined_out(
        [pl.BlockSpec((tm, S5W), lambda i: (i, 0)), pl.BlockSpec((S5W, S5W), lambda i: (0, 0)), vec, vec],
        [y, w_glu_bf16, b_glu.reshape(1, S5W), gain.reshape(1, S5W)], joined)
    return pl.pallas_call(
        _glu_norm_kernel,
        grid=(m // tm,),
        in_specs=in_specs,
        out_specs=pl.BlockSpec((tm, S5W), lambda i: (i + row0 // tm, 0)),
        out_shape=jax.ShapeDtypeStruct((M_ALL, S5W), BF16),
        input_output_aliases=aliases,
        compiler_params=_cparams(("parallel",), 48),
        name="s5_glu_norm",
    )(*args)


def kernel(x_prompt, x_sample, state_shift, state_wkv, state_ssm_re, state_ssm_im, ffn1_norm, ffn1_w_gate, ffn1_w_up, ffn1_w_down, mix_norm, w_in, shift_mu, rw_w0, rw_w_up, rw_a0, rw_a_up, rw_g_up, rw_k_k, rw_k_a, rw_r_k, rw_lnx_w, rw_lnx_b, s5_lam_re, s5_lam_im, s5_b_re, s5_b_im, s5_c_re, s5_c_im, s5_d, s5_log_step, s5_w_glu, s5_b_glu, s5_out_norm, w_out, ffn2_norm, ffn2_w_gate, ffn2_w_up, ffn2_w_down, final_norm):
    x = jnp.concatenate([x_prompt.reshape(M_PROMPT, D_MODEL),
                         x_sample.transpose(1, 0, 2).reshape(M_SAMPLE, D_MODEL)], axis=0)
    zero_state = jnp.zeros((HEAD, HEAD, BATCH * HEADS), F32)
    shift_p, wkv_p, re_p, im_p, shift_s, wkv_s, re_s, im_s = ([] for _ in range(8))

    for l in range(DEPTH):
        x = _ffn(x, ffn1_norm[l], ffn1_w_gate[l], ffn1_w_up[l], ffn1_w_down[l])

        h_mix = _rmsnorm(x, mix_norm[l], BF16, ROW_TM)
        n_last = BATCH + DEC_BATCH
        x_last = jnp.concatenate([x[:M_PROMPT].reshape(BATCH, SEQ, D_MODEL)[:, -1],
                                  x[M_ALL - DEC_BATCH:],
                                  jnp.zeros((-n_last % SUBLANES, D_MODEL), F32)], axis=0)
        h_last = _rmsnorm(x_last, mix_norm[l], F32, x_last.shape[0])
        shift_p.append(h_last[:BATCH])
        shift_s.append(h_last[BATCH:n_last])

        w_pad, mu_pad = _pad_in_proj(w_in[l], shift_mu[l])
        q_p = _win_prompt(h_mix, w_pad, mu_pad)
        q_s = _win_sample(jnp.concatenate([state_shift[l].astype(BF16), h_mix[M_PROMPT:]], axis=0),
                          w_pad, mu_pad)

        rw_args = (rw_w0[l], rw_a0[l], rw_k_k[l], rw_k_a[l], rw_r_k[l], rw_w_up[l], rw_a_up[l], rw_g_up[l])
        *scan_p, gate_p, bonus_p = _rwkv_pre(q_p, *rw_args)
        *scan_s, gate_s, bonus_s = _rwkv_pre(q_s, *rw_args)
        y_p, s_p = _wkv_scan(*[_to_scan_prompt(a) for a in scan_p], zero_state, tc=32)
        s0 = state_wkv[l].transpose(3, 2, 0, 1).reshape(HEAD, HEAD, DEC_BATCH * HEADS)
        y_s, s_s = _wkv_scan(*[_to_scan_sample(a) for a in scan_s], s0, tc=DEC_SEQ)
        wkv_p.append(s_p.reshape(HEAD, HEAD, BATCH, HEADS).transpose(2, 3, 1, 0))
        wkv_s.append(s_s.reshape(HEAD, HEAD, DEC_BATCH, HEADS).transpose(2, 3, 1, 0))
        y_rw = _rwkv_post(_from_scan_prompt(y_p), bonus_p, gate_p, rw_lnx_w[l], rw_lnx_b[l], 0)
        y_rw = _rwkv_post(_from_scan_sample(y_s), bonus_s, gate_s, rw_lnx_w[l], rw_lnx_b[l], M_PROMPT, y_rw)

        s5p = _s5_params(s5_lam_re[l], s5_lam_im[l], s5_log_step[l], s5_b_re[l], s5_b_im[l],
                         s5_c_re[l], s5_c_im[l], s5_d[l])
        y5_p, hre_p, him_p = _s5_prompt(q_p, s5p)
        y5_s, hre_s, him_s = _s5_sample(q_s, state_ssm_re[l].reshape(DEC_BATCH, S5_G * S5_P),
                                        state_ssm_im[l].reshape(DEC_BATCH, S5_G * S5_P), s5p)
        re_p.append(hre_p.reshape(BATCH, S5_G, S5_P))
        im_p.append(him_p.reshape(BATCH, S5_G, S5_P))
        re_s.append(hre_s.reshape(DEC_BATCH, S5_G, S5_P))
        im_s.append(him_s.reshape(DEC_BATCH, S5_G, S5_P))
        w_glu = s5_w_glu[l].astype(BF16)
        y_s5 = _glu_norm(y5_p, w_glu, s5_b_glu[l], s5_out_norm[l], 0)
        y_s5 = _glu_norm(y5_s, w_glu, s5_b_glu[l], s5_out_norm[l], M_PROMPT, y_s5)

        x = _res_matmul([y_rw, y_s5], w_out[l].astype(BF16), x, 1.0, tm=1088, tn=512, vmem_mib=48)
        x = _ffn(x, ffn2_norm[l], ffn2_w_gate[l], ffn2_w_up[l], ffn2_w_down[l])

    y = _rmsnorm(x, final_norm, F32, ROW_TM)
    y_prompt = y[:M_PROMPT].reshape(BATCH, SEQ, D_MODEL)
    y_sample = y[M_PROMPT:].reshape(DEC_SEQ, DEC_BATCH, D_MODEL).transpose(1, 0, 2)
    st = jnp.stack
    return (y_prompt, y_sample, st(shift_p), st(wkv_p), st(re_p), st(im_p),
            st(shift_s), st(wkv_s), st(re_s), st(im_s))
```

```python
import functools
import math

import jax
import jax.numpy as jnp
from jax import lax
from jax.experimental import pallas as pl
from jax.experimental.pallas import tpu as pltpu

F32 = jnp.float32
BF16 = jnp.bfloat16

D_MODEL = 4096
BATCH = 4
SEQ = 2048
DEPTH = 2
DEC_BATCH = 128
DEC_SEQ = 4
M_PROMPT = BATCH * SEQ
M_SAMPLE = DEC_BATCH * DEC_SEQ
M_ALL = M_PROMPT + M_SAMPLE

RW = D_MODEL // 2
HEAD = 64
HEADS = RW // HEAD
S5W = D_MODEL - RW
S5_C = 16
S5_G = S5W // S5_C
S5_P = 64
W_LORA = 96
A_LORA = 96
G_LORA = 256
LORA_PAD = 128
LORA_W = 2 * LORA_PAD + G_LORA
D_FF = 11008
EPS_RMS = 1e-6
EPS_GN = 64e-5

COL_LORA = 3 * RW
COL_S5 = COL_LORA + LORA_W
D_INP = COL_S5 + S5W

V7X_VMEM_BYTES = 64 * 1024 * 1024
LANES = 128
SUBLANES = 8

S5_GT = 8
S5_TILES = S5_G // S5_GT
S5_UW = S5_GT * S5_C
S5_HW = S5_GT * S5_P

ROW_TM = 256


def _cparams(semantics, vmem_mib):
    assert vmem_mib * 1024 * 1024 < V7X_VMEM_BYTES
    return pltpu.CompilerParams(dimension_semantics=semantics,
                                vmem_limit_bytes=vmem_mib * 1024 * 1024)


def _joined_out(in_specs, args, joined):
    if joined is None:
        return in_specs, args, {}
    return (in_specs + [pl.BlockSpec(memory_space=pl.ANY)], args + [joined], {len(args): 0})


def _rmsnorm_kernel(x_ref, g_ref, o_ref):
    x = x_ref[...]
    ms = jnp.mean(x * x, axis=-1, keepdims=True)
    o_ref[...] = (x * lax.rsqrt(ms + EPS_RMS) * g_ref[...]).astype(o_ref.dtype)


def _rmsnorm(x, g, out_dtype, tm, row0=0, rows=None):
    d = x.shape[1]
    m = x.shape[0] if rows is None else rows
    return pl.pallas_call(
        _rmsnorm_kernel,
        grid=(m // tm,),
        in_specs=[pl.BlockSpec((tm, d), lambda i: (i + row0 // tm, 0)),
                  pl.BlockSpec((1, d), lambda i: (0, 0))],
        out_specs=pl.BlockSpec((tm, d), lambda i: (i, 0)),
        out_shape=jax.ShapeDtypeStruct((m, d), out_dtype),
        compiler_params=_cparams(("parallel",), 40),
        name="rmsnorm",
    )(x, g.reshape(1, d))


def _gate_up_kernel(h_ref, wg_ref, wu_ref, o_ref):
    h = h_ref[...]
    a = jnp.dot(h, wg_ref[...].astype(BF16), preferred_element_type=F32)
    b = jnp.dot(h, wu_ref[...].astype(BF16), preferred_element_type=F32)
    o_ref[...] = (a * jax.nn.sigmoid(a) * b).astype(o_ref.dtype)


def _gate_up(h, wg, wu, layer, tm=1088, tn=256):
    m, d = h.shape
    f = wg.shape[2]
    w_spec = pl.BlockSpec((None, d, tn), lambda i, j: (layer, 0, j))
    return pl.pallas_call(
        _gate_up_kernel,
        grid=(m // tm, f // tn),
        in_specs=[pl.BlockSpec((tm, d), lambda i, j: (i, 0)), w_spec, w_spec],
        out_specs=pl.BlockSpec((tm, tn), lambda i, j: (i, j)),
        out_shape=jax.ShapeDtypeStruct((m, f), BF16),
        compiler_params=_cparams(("parallel", "arbitrary"), 56),
        name="ffn_gate_up",
    )(h, wg, wu)


def _res_matmul_kernel(*refs, n_pairs, scale):
    a_refs = refs[:n_pairs]
    w_refs = refs[n_pairs:2 * n_pairs]
    x_ref, o_ref = refs[2 * n_pairs], refs[2 * n_pairs + 1]
    acc = jnp.dot(a_refs[0][...], w_refs[0][...], preferred_element_type=F32)
    for a_ref, w_ref in zip(a_refs[1:], w_refs[1:]):
        acc = acc + jnp.dot(a_ref[...], w_ref[...], preferred_element_type=F32)
    if scale != 1.0:
        acc = scale * acc
    o_ref[...] = x_ref[...] + acc


def _res_matmul(a_list, w, layer, x, scale, tm, tn, vmem_mib):
    m, n = x.shape
    n_pairs = len(a_list)
    in_specs = [pl.BlockSpec((tm, a.shape[1]), lambda i, j: (i, 0)) for a in a_list]
    for p, a in enumerate(a_list):
        in_specs.append(pl.BlockSpec((None, a.shape[1], tn), lambda i, j, p=p: (layer, p, j)))
    in_specs.append(pl.BlockSpec((tm, tn), lambda i, j: (i, j)))
    assert sum(a.shape[1] for a in a_list) == w.shape[1]
    return pl.pallas_call(
        functools.partial(_res_matmul_kernel, n_pairs=n_pairs, scale=scale),
        grid=(m // tm, n // tn),
        in_specs=in_specs,
        out_specs=pl.BlockSpec((tm, tn), lambda i, j: (i, j)),
        out_shape=jax.ShapeDtypeStruct((m, n), F32),
        compiler_params=_cparams(("parallel", "arbitrary"), vmem_mib),
        name="res_matmul",
    )(*a_list, *([w] * n_pairs), x)


def _ffn(x, norm_g, wg, wu, wd_bf16, layer):
    h = _rmsnorm(x, norm_g, BF16, ROW_TM)
    act = _gate_up(h, wg, wu, layer)
    return _res_matmul([act], wd_bf16, layer, x, 0.5, tm=544, tn=512, vmem_mib=58)


WIN_TN = 256
WIN_MAIN_TILES = COL_LORA // WIN_TN
WIN_SHIFT_TILES = COL_S5 // WIN_TN


def _win_tile_cases(h_ref, wm_ref, wt_ref, j, emit):
    @pl.when(j < WIN_MAIN_TILES)
    def _():
        emit(jnp.dot(h_ref[...], wm_ref[...], preferred_element_type=F32), True)

    @pl.when((j >= WIN_MAIN_TILES) & (j < WIN_SHIFT_TILES))
    def _():
        emit(jnp.dot(h_ref[...], wt_ref[...], preferred_element_type=F32), True)

    @pl.when(j >= WIN_SHIFT_TILES)
    def _():
        emit(jnp.dot(h_ref[...], wt_ref[...], preferred_element_type=F32), False)


def _win_prompt_kernel(h_ref, wm_ref, wt_ref, mu_ref, o_ref):
    def emit(p, shifted):
        if shifted:
            rows = lax.broadcasted_iota(jnp.int32, p.shape, 0)
            prev = jnp.where(rows == 0, 0.0, pltpu.roll(p, 1, 0))
            o_ref[...] = p + (prev - p) * mu_ref[...]
        else:
            o_ref[...] = p

    _win_tile_cases(h_ref, wm_ref, wt_ref, pl.program_id(1), emit)


def _win_sample_kernel(h_ref, wm_ref, wt_ref, mu_ref, o_ref):
    def emit(p, shifted):
        cur = p[DEC_BATCH:]
        o_ref[...] = cur + (p[:M_SAMPLE] - cur) * mu_ref[...] if shifted else cur

    _win_tile_cases(h_ref, wm_ref, wt_ref, pl.program_id(0), emit)


def _win_weight_specs(layer, col_of):
    main = pl.BlockSpec((None, D_MODEL, WIN_TN),
                        lambda *g: (layer, 0, jnp.minimum(col_of(*g), WIN_MAIN_TILES - 1)))
    tail = pl.BlockSpec((None, D_MODEL, WIN_TN),
                        lambda *g: (layer, 0, jnp.maximum(col_of(*g) - WIN_MAIN_TILES, 0)))
    mu = pl.BlockSpec((None, 1, WIN_TN), lambda *g: (layer, 0, col_of(*g)))
    return [main, tail, mu]


def _win_prompt(h, w_main, w_tail, mu_pad, layer):
    tn = WIN_TN
    return pl.pallas_call(
        _win_prompt_kernel,
        grid=(BATCH, D_INP // tn),
        in_specs=[pl.BlockSpec((SEQ, D_MODEL), lambda b, j: (b, 0), pipeline_mode=pl.Buffered(1))]
                 + _win_weight_specs(layer, lambda b, j: j),
        out_specs=pl.BlockSpec((SEQ, tn), lambda b, j: (b, j)),
        out_shape=jax.ShapeDtypeStruct((M_PROMPT, D_INP), F32),
        compiler_params=_cparams(("parallel", "arbitrary"), 56),
        name="in_proj_prompt",
    )(h, w_main, w_tail, mu_pad)


def _win_sample(h_rows, w_main, w_tail, mu_pad, layer):
    tn = WIN_TN
    rows = h_rows.shape[0]
    return pl.pallas_call(
        _win_sample_kernel,
        grid=(D_INP // tn,),
        in_specs=[pl.BlockSpec((rows, D_MODEL), lambda j: (0, 0))] + _win_weight_specs(layer, lambda j: j),
        out_specs=pl.BlockSpec((M_SAMPLE, tn), lambda j: (0, j)),
        out_shape=jax.ShapeDtypeStruct((M_SAMPLE, D_INP), F32),
        compiler_params=_cparams(("arbitrary",), 32),
        name="in_proj_sample",
    )(h_rows, w_main, w_tail, mu_pad)


def _in_proj_tail(w_in, mu):
    o_w, o_a, o_g = COL_LORA, COL_LORA + W_LORA, COL_LORA + W_LORA + A_LORA
    o_s = o_g + G_LORA
    w_main = w_in[:, :, :o_w].astype(BF16)
    zw = jnp.zeros((DEPTH, D_MODEL, LORA_PAD - W_LORA), w_in.dtype)
    w_tail = jnp.concatenate([w_in[:, :, o_w:o_a], zw, w_in[:, :, o_a:o_g], zw,
                              w_in[:, :, o_g:]], axis=2).astype(BF16)
    zm = jnp.zeros((DEPTH, LORA_PAD - W_LORA), mu.dtype)
    mu_pad = jnp.concatenate([mu[:, :o_w], mu[:, o_w:o_a], zm, mu[:, o_a:o_g], zm, mu[:, o_g:o_s],
                              jnp.zeros((DEPTH, S5W), mu.dtype)], axis=1).reshape(DEPTH, 1, D_INP)
    return w_main, w_tail, mu_pad


def _split_bf16(x):
    hi = x.astype(BF16)
    return hi, (x - hi.astype(F32)).astype(BF16)


def _head_sum(x):
    r = lax.broadcasted_iota(jnp.int32, (LANES, LANES), 0) // HEAD
    c = lax.broadcasted_iota(jnp.int32, (LANES, LANES), 1) // HEAD
    ones = (r == c).astype(BF16)
    hi = x.astype(BF16)
    r1 = x - hi.astype(F32)
    mid = r1.astype(BF16)
    lo = (r1 - mid.astype(F32)).astype(BF16)
    outs = []
    for s in range(x.shape[1] // LANES):
        sl = slice(s * LANES, (s + 1) * LANES)
        acc = jnp.dot(hi[:, sl], ones, preferred_element_type=F32)
        acc = acc + jnp.dot(mid[:, sl], ones, preferred_element_type=F32)
        acc = acc + jnp.dot(lo[:, sl], ones, preferred_element_type=F32)
        outs.append(acc)
    return jnp.concatenate(outs, axis=1)


def _rwkv_pre_kernel(r_ref, k_ref, v_ref, lora_ref, w0_ref, a0_ref, kk_ref, ka_ref, rk_ref,
                     wup_ref, aup_ref, gup_ref,
                     r_out, w_out, k_out, v_out, kk_out, b_out, g_out, bonus_out):
    r = r_ref[...]
    k = k_ref[...]
    v = v_ref[...]
    r_out[...] = r
    v_out[...] = v
    lora = lora_ref[...]
    wd = lora[:, :LORA_PAD]
    ad = lora[:, LORA_PAD:2 * LORA_PAD]
    gd = lora[:, 2 * LORA_PAD:]
    z = -(w0_ref[...] + jnp.dot(jnp.tanh(wd).astype(BF16), wup_ref[...], preferred_element_type=F32))
    softplus = jnp.maximum(z, 0.0) + jnp.log1p(jnp.exp(-jnp.abs(z)))
    w_out[...] = jnp.exp(-jnp.exp(-softplus - 0.5))
    a = jax.nn.sigmoid(a0_ref[...] + jnp.dot(ad.astype(BF16), aup_ref[...], preferred_element_type=F32))
    g_out[...] = jnp.dot(jax.nn.sigmoid(gd).astype(BF16), gup_ref[...], preferred_element_type=F32)
    kk = k * kk_ref[...]
    k2 = k * (1.0 + (a - 1.0) * ka_ref[...])
    k_out[...] = k2
    kkn = kk / jnp.maximum(jnp.sqrt(_head_sum(kk * kk)), 1e-12)
    kk_out[...] = kkn
    b_out[...] = kkn * a
    bonus_out[...] = _head_sum(r * k2 * rk_ref[...]) * v


def _time_major_spec(m, tm, time_major):
    if not time_major:
        return pl.BlockSpec((tm, RW), lambda i: (i, 0)), (m, RW)
    per_seq = SEQ // tm
    return pl.BlockSpec((tm, RW), lambda i: (i % per_seq, i // per_seq)), (SEQ, (m // SEQ) * RW)


def _rwkv_pre(q, w0, a0, k_k, k_a, r_k, w_up, a_up, g_up, time_major, tm=ROW_TM):
    m = q.shape[0]
    out_spec, out_dims = _time_major_spec(m, tm, time_major)
    row = lambda v: v.reshape(1, RW)
    pad_rows = lambda u: jnp.concatenate(
        [u, jnp.zeros((LORA_PAD - u.shape[0], RW), u.dtype)], axis=0).astype(BF16)
    vec = pl.BlockSpec((1, RW), lambda i: (0, 0))
    full = lambda rows: pl.BlockSpec((rows, RW), lambda i: (0, 0))
    out = jax.ShapeDtypeStruct(out_dims, F32)
    return pl.pallas_call(
        _rwkv_pre_kernel,
        grid=(m // tm,),
        in_specs=[pl.BlockSpec((tm, RW), lambda i: (i, 0)),
                  pl.BlockSpec((tm, RW), lambda i: (i, 1)),
                  pl.BlockSpec((tm, RW), lambda i: (i, 2)),
                  pl.BlockSpec((tm, LORA_W), lambda i: (i, COL_LORA // LORA_W)),
                  vec, vec, vec, vec, vec, full(LORA_PAD), full(LORA_PAD), full(G_LORA)],
        out_specs=[out_spec] * 8,
        out_shape=[out] * 8,
        compiler_params=_cparams(("parallel",), 56),
        name="rwkv_pre",
    )(q, q, q, q, row(w0), row(a0), row(k_k), row(k_a), row(r_k.reshape(RW)),
      pad_rows(w_up), pad_rows(a_up), g_up.astype(BF16))


def _rwkv_post_kernel(y_ref, bonus_ref, g_ref, lw_ref, lb_ref, *rest):
    o_ref = rest[-1]
    y = y_ref[...]
    mean = _head_sum(y) * (1.0 / HEAD)
    c = y - mean
    var = _head_sum(c * c) * (1.0 / HEAD)
    yn = c * lax.rsqrt(var + EPS_GN) * lw_ref[...] + lb_ref[...]
    o_ref[...] = ((yn + bonus_ref[...]) * g_ref[...]).astype(o_ref.dtype)


def _rwkv_post(y, bonus, g, lnx_w, lnx_b, row0, time_major, joined=None, tm=ROW_TM):
    m = y.size // RW
    blk, _ = _time_major_spec(m, tm, time_major)
    vec = pl.BlockSpec((1, RW), lambda i: (0, 0))
    in_specs, args, aliases = _joined_out(
        [blk, blk, blk, vec, vec], [y, bonus, g, lnx_w.reshape(1, RW), lnx_b.reshape(1, RW)], joined)
    return pl.pallas_call(
        _rwkv_post_kernel,
        grid=(m // tm,),
        in_specs=in_specs,
        out_specs=pl.BlockSpec((tm, RW), lambda i: (i + row0 // tm, 0)),
        out_shape=jax.ShapeDtypeStruct((M_ALL, RW), BF16),
        input_output_aliases=aliases,
        compiler_params=_cparams(("parallel",), 40),
        name="rwkv_post",
    )(*args)


WKV_UNROLL = 8


def _wkv_kernel(r_ref, w_ref, k_ref, v_ref, kk_ref, b_ref, s0_ref, y_ref, s_ref, wr_ref, *, steps):
    @pl.when(pl.program_id(1) == 0)
    def _():
        s_ref[...] = s0_ref[...]

    def step(t, carry):
        r = r_ref[t]
        wr_ref[...] = w_ref[t] * r
        beta = jnp.sum(b_ref[t] * r, axis=0, keepdims=True)
        kappa = jnp.sum(k_ref[t] * r, axis=0, keepdims=True)
        v = v_ref[t]

        def contract(j, acc):
            sa, u = acc
            sj = s_ref[j]
            return sa - sj * kk_ref[t, pl.ds(j, 1), :], u + sj * wr_ref[pl.ds(j, 1), :]

        zero = jnp.zeros((HEAD, LANES), F32)
        sa, u = lax.fori_loop(0, HEAD, contract, (zero, zero), unroll=WKV_UNROLL)

        def update(j, c):
            s_ref[j] = (s_ref[j] * w_ref[t, pl.ds(j, 1), :] + sa * b_ref[t, pl.ds(j, 1), :]
                        + v * k_ref[t, pl.ds(j, 1), :])
            return c

        lax.fori_loop(0, HEAD, update, 0, unroll=WKV_UNROLL)
        y_ref[t] = u + sa * beta + v * kappa
        return carry

    lax.fori_loop(0, steps, step, 0)


def _wkv_scan(r, w, k, v, kk, b, s0, tc):
    length, _, n = r.shape
    vec = pl.BlockSpec((tc, HEAD, LANES), lambda c, t: (t, 0, c))
    st = pl.BlockSpec((HEAD, HEAD, LANES), lambda c, t: (0, 0, c))
    return pl.pallas_call(
        functools.partial(_wkv_kernel, steps=tc),
        grid=(n // LANES, length // tc),
        in_specs=[vec] * 6 + [st],
        out_specs=[vec, st],
        out_shape=[jax.ShapeDtypeStruct((length, HEAD, n), F32),
                   jax.ShapeDtypeStruct((HEAD, HEAD, n), F32)],
        scratch_shapes=[pltpu.VMEM((HEAD, LANES), F32)],
        compiler_params=_cparams(("parallel", "arbitrary"), 40),
        name="wkv_scan",
    )(r, w, k, v, kk, b, s0)


def _to_scan(a, length):
    return a.reshape(length, -1, HEAD).transpose(0, 2, 1)


def _from_scan(y, length):
    return y.transpose(0, 2, 1).reshape(length, -1)


def _state_to_scan(s):
    n = s.shape[0] * s.shape[1]
    s = lax.optimization_barrier(s.reshape(n, HEAD, HEAD).transpose(0, 2, 1))
    return s.reshape(n, HEAD * HEAD).T.reshape(HEAD, HEAD, n)


def _state_from_scan(s, batch):
    n = s.shape[2]
    s = lax.optimization_barrier(s.reshape(HEAD * HEAD, n).T)
    return s.reshape(n, HEAD, HEAD).transpose(0, 2, 1).reshape(batch, HEADS, HEAD, HEAD)


def _s5_discretize(lre_ref, lim_ref, ls_ref, bre_ref, bim_ref):
    lre, lim = lre_ref[...], lim_ref[...]
    step = jnp.exp(ls_ref[...])
    mag = jnp.exp(lre * step)
    ar = mag * jnp.cos(lim * step)
    ai = mag * jnp.sin(lim * step)
    den = lre * lre + lim * lim
    nr = ar - 1.0
    f_re = (nr * lre + ai * lim) / den
    f_im = (ai * lre - nr * lim) / den
    bre, bim = bre_ref[...], bim_ref[...]
    return ar, ai, f_re * bre - f_im * bim, f_re * bim + f_im * bre


def _dot3(a, b):
    a_hi, a_lo = _split_bf16(a)
    b_hi, b_lo = _split_bf16(b)
    acc = jnp.dot(a_hi, b_hi, preferred_element_type=F32)
    acc = acc + jnp.dot(a_lo, b_hi, preferred_element_type=F32)
    return acc + jnp.dot(a_hi, b_lo, preferred_element_type=F32)


def _s5_readout(hr, hi, cre_ref, cim_ref, d_ref, u):
    y = jnp.dot(hr.astype(BF16), cre_ref[...].astype(BF16), preferred_element_type=F32)
    y = y - jnp.dot(hi.astype(BF16), cim_ref[...].astype(BF16), preferred_element_type=F32)
    return y + d_ref[...] * u


def _cmul(ar, ai, br, bi):
    return ar * br - ai * bi, ar * bi + ai * br


def _s5_prompt_kernel(u_ref, lre_ref, lim_ref, ls_ref, bre_ref, bim_ref, cre_ref, cim_ref, d_ref,
                      y_ref, hre_ref, him_ref, hr_scr, hi_scr):
    ar, ai, bbr, bbi = _s5_discretize(lre_ref, lim_ref, ls_ref, bre_ref, bim_ref)
    u = u_ref[...]
    hr_scr[...] = _dot3(u, bbr)
    hi_scr[...] = _dot3(u, bbi)

    rows = lax.broadcasted_iota(jnp.int32, (SUBLANES, S5_HW), 0)
    powers = [(ar, ai)]
    for _ in range(SUBLANES - 1):
        powers.append(_cmul(*powers[-1], ar, ai))
    levels = []
    for sh in (1, 2, 4):
        pr, pi = powers[sh - 1]
        levels.append((sh, jnp.where(rows >= sh, pr, 0.0), jnp.where(rows >= sh, pi, 0.0)))
    cpr = jnp.zeros((SUBLANES, S5_HW), F32)
    cpi = jnp.zeros((SUBLANES, S5_HW), F32)
    for n, (pr, pi) in enumerate(powers):
        cpr = jnp.where(rows == n, pr, cpr)
        cpi = jnp.where(rows == n, pi, cpi)

    def tile(i, carry):
        cr, ci = carry
        r0 = pl.multiple_of(i * SUBLANES, SUBLANES)
        xr = hr_scr[pl.ds(r0, SUBLANES), :]
        xi = hi_scr[pl.ds(r0, SUBLANES), :]
        for sh, mr, mi in levels:
            sr, si = pltpu.roll(xr, sh, 0), pltpu.roll(xi, sh, 0)
            xr, xi = xr + (mr * sr - mi * si), xi + (mr * si + mi * sr)
        xr = xr + (cpr * cr - cpi * ci)
        xi = xi + (cpr * ci + cpi * cr)
        hr_scr[pl.ds(r0, SUBLANES), :] = xr
        hi_scr[pl.ds(r0, SUBLANES), :] = xi
        last = SUBLANES - 1
        return (jnp.broadcast_to(xr[last:, :], (SUBLANES, S5_HW)),
                jnp.broadcast_to(xi[last:, :], (SUBLANES, S5_HW)))

    zero = jnp.zeros((SUBLANES, S5_HW), F32)
    cr, ci = lax.fori_loop(0, SEQ // SUBLANES, tile, (zero, zero), unroll=2)
    hre_ref[...] = cr[:1]
    him_ref[...] = ci[:1]
    y_ref[...] = _s5_readout(hr_scr[...], hi_scr[...], cre_ref, cim_ref, d_ref, u)


def _s5_sample_kernel(u_ref, h0r_ref, h0i_ref, lre_ref, lim_ref, ls_ref, bre_ref, bim_ref, cre_ref, cim_ref,
                      d_ref, y_ref, hre_ref, him_ref, hr_scr, hi_scr):
    ar, ai, bbr, bbi = _s5_discretize(lre_ref, lim_ref, ls_ref, bre_ref, bim_ref)
    u = u_ref[...]
    bu_r = _dot3(u, bbr)
    bu_i = _dot3(u, bbi)
    hr, hi = h0r_ref[...], h0i_ref[...]
    for t in range(DEC_SEQ):
        rows = slice(t * DEC_BATCH, (t + 1) * DEC_BATCH)
        hr, hi = ar * hr - ai * hi + bu_r[rows], ar * hi + ai * hr + bu_i[rows]
        hr_scr[rows, :] = hr
        hi_scr[rows, :] = hi
    hre_ref[...] = hr
    him_ref[...] = hi
    y_ref[...] = _s5_readout(hr_scr[...], hi_scr[...], cre_ref, cim_ref, d_ref, u)


def _s5_params(lam_re, lam_im, log_step, b_re, b_im, c_re, c_im, d):
    eye = jnp.eye(S5_GT, dtype=F32)
    chan = lambda a: a.reshape(S5_TILES, 1, S5_HW)

    def b_blk(b):
        x = b.transpose(0, 2, 1).reshape(S5_TILES, S5_GT, S5_C, S5_P)
        return jnp.einsum('tgcp,gh->tgchp', x, eye).reshape(S5_TILES, S5_UW, S5_HW)

    def c_blk(c):
        x = c.transpose(0, 2, 1).reshape(S5_TILES, S5_GT, S5_P, S5_C)
        return jnp.einsum('tgpc,gh->tgphc', x, eye).reshape(S5_TILES, S5_HW, S5_UW)

    ls = jnp.broadcast_to(log_step[:, None], (S5_G, S5_P))
    return (chan(lam_re), chan(lam_im), chan(ls), b_blk(b_re), b_blk(b_im), c_blk(c_re), c_blk(c_im),
            d.reshape(S5_TILES, 1, S5_UW))


def _s5_param_specs(tile_of):
    chan = pl.BlockSpec((None, 1, S5_HW), lambda *g: (tile_of(*g), 0, 0))
    bspec = pl.BlockSpec((None, S5_UW, S5_HW), lambda *g: (tile_of(*g), 0, 0))
    cspec = pl.BlockSpec((None, S5_HW, S5_UW), lambda *g: (tile_of(*g), 0, 0))
    dspec = pl.BlockSpec((None, 1, S5_UW), lambda *g: (tile_of(*g), 0, 0))
    return [chan, chan, chan, bspec, bspec, cspec, cspec, dspec]


def _s5_prompt(q, params):
    col0 = COL_S5 // S5_UW
    state = jax.ShapeDtypeStruct((BATCH, 1, S5_G * S5_P), F32)
    st_spec = pl.BlockSpec((None, 1, S5_HW), lambda b, g: (b, 0, g))
    return pl.pallas_call(
        _s5_prompt_kernel,
        grid=(BATCH, S5_TILES),
        in_specs=[pl.BlockSpec((SEQ, S5_UW), lambda b, g: (b, col0 + g))] + _s5_param_specs(lambda b, g: g),
        out_specs=[pl.BlockSpec((SEQ, S5_UW), lambda b, g: (b, g)), st_spec, st_spec],
        out_shape=[jax.ShapeDtypeStruct((M_PROMPT, S5W), F32), state, state],
        scratch_shapes=[pltpu.VMEM((SEQ, S5_HW), F32), pltpu.VMEM((SEQ, S5_HW), F32)],
        compiler_params=_cparams(("parallel", "parallel"), 48),
        name="s5_prompt",
    )(q, *params)


def _s5_sample(q, h0_re, h0_im, params):
    col0 = COL_S5 // S5_UW
    state = jax.ShapeDtypeStruct((DEC_BATCH, S5_G * S5_P), F32)
    st_spec = pl.BlockSpec((DEC_BATCH, S5_HW), lambda g: (0, g))
    return pl.pallas_call(
        _s5_sample_kernel,
        grid=(S5_TILES,),
        in_specs=[pl.BlockSpec((M_SAMPLE, S5_UW), lambda g: (0, col0 + g)), st_spec, st_spec]
                 + _s5_param_specs(lambda g: g),
        out_specs=[pl.BlockSpec((M_SAMPLE, S5_UW), lambda g: (0, g)), st_spec, st_spec],
        out_shape=[jax.ShapeDtypeStruct((M_SAMPLE, S5W), F32), state, state],
        scratch_shapes=[pltpu.VMEM((M_SAMPLE, S5_HW), F32), pltpu.VMEM((M_SAMPLE, S5_HW), F32)],
        compiler_params=_cparams(("parallel",), 32),
        name="s5_sample",
    )(q, h0_re, h0_im, *params)


def _glu_norm_kernel(y_ref, w_ref, b_ref, g_ref, *rest):
    o_ref = rest[-1]
    y = y_ref[...]
    z = 0.5 * y * (1.0 + lax.erf(y * math.sqrt(0.5)))
    gate = jax.nn.sigmoid(jnp.dot(z.astype(BF16), w_ref[...], preferred_element_type=F32) + b_ref[...])
    out = z * gate
    ms = jnp.mean(out * out, axis=-1, keepdims=True)
    o_ref[...] = (out * lax.rsqrt(ms + EPS_RMS) * g_ref[...]).astype(o_ref.dtype)


def _glu_norm(y, w_glu_bf16, layer, b_glu, gain, row0, joined=None, tm=ROW_TM):
    m = y.shape[0]
    vec = pl.BlockSpec((1, S5W), lambda i: (0, 0))
    in_specs, args, aliases = _joined_out(
        [pl.BlockSpec((tm, S5W), lambda i: (i, 0)),
         pl.BlockSpec((None, S5W, S5W), lambda i: (layer, 0, 0)), vec, vec],
        [y, w_glu_bf16, b_glu.reshape(1, S5W), gain.reshape(1, S5W)], joined)
    return pl.pallas_call(
        _glu_norm_kernel,
        grid=(m // tm,),
        in_specs=in_specs,
        out_specs=pl.BlockSpec((tm, S5W), lambda i: (i + row0 // tm, 0)),
        out_shape=jax.ShapeDtypeStruct((M_ALL, S5W), BF16),
        input_output_aliases=aliases,
        compiler_params=_cparams(("parallel",), 48),
        name="s5_glu_norm",
    )(*args)


def kernel(x_prompt, x_sample, state_shift, state_wkv, state_ssm_re, state_ssm_im, ffn1_norm, ffn1_w_gate, ffn1_w_up, ffn1_w_down, mix_norm, w_in, shift_mu, rw_w0, rw_w_up, rw_a0, rw_a_up, rw_g_up, rw_k_k, rw_k_a, rw_r_k, rw_lnx_w, rw_lnx_b, s5_lam_re, s5_lam_im, s5_b_re, s5_b_im, s5_c_re, s5_c_im, s5_d, s5_log_step, s5_w_glu, s5_b_glu, s5_out_norm, w_out, ffn2_norm, ffn2_w_gate, ffn2_w_up, ffn2_w_down, final_norm):
    x = jnp.concatenate([x_prompt.reshape(M_PROMPT, D_MODEL),
                         x_sample.transpose(1, 0, 2).reshape(M_SAMPLE, D_MODEL)], axis=0)
    zero_state = jnp.zeros((HEAD, HEAD, BATCH * HEADS), F32)
    shift_p, wkv_p, re_p, im_p, shift_s, wkv_s, re_s, im_s = ([] for _ in range(8))
    wd1, wd2 = ffn1_w_down.astype(BF16), ffn2_w_down.astype(BF16)
    wo, w_glu = w_out.astype(BF16), s5_w_glu.astype(BF16)
    w_main, w_tail, mu_pad = _in_proj_tail(w_in, shift_mu)

    for l in range(DEPTH):
        x = _ffn(x, ffn1_norm[l], ffn1_w_gate, ffn1_w_up, wd1, l)

        h_mix = _rmsnorm(x, mix_norm[l], BF16, ROW_TM)
        n_last = BATCH + DEC_BATCH
        x_last = jnp.concatenate([x[SEQ - 1:M_PROMPT:SEQ], x[M_ALL - DEC_BATCH:],
                                  jnp.zeros((-n_last % SUBLANES, D_MODEL), F32)], axis=0)
        h_last = _rmsnorm(x_last, mix_norm[l], F32, x_last.shape[0])
        shift_p.append(h_last[:BATCH])
        shift_s.append(h_last[BATCH:n_last])

        q_p = _win_prompt(h_mix, w_main, w_tail, mu_pad, l)
        q_s = _win_sample(jnp.concatenate([state_shift[l].astype(BF16), h_mix[M_PROMPT:]], axis=0),
                          w_main, w_tail, mu_pad, l)

        rw_args = (rw_w0[l], rw_a0[l], rw_k_k[l], rw_k_a[l], rw_r_k[l], rw_w_up[l], rw_a_up[l], rw_g_up[l])
        *scan_p, gate_p, bonus_p = _rwkv_pre(q_p, *rw_args, time_major=True)
        *scan_s, gate_s, bonus_s = _rwkv_pre(q_s, *rw_args, time_major=False)
        y_p, s_p = _wkv_scan(*[_to_scan(a, SEQ) for a in scan_p], zero_state, tc=32)
        y_s, s_s = _wkv_scan(*[_to_scan(a, DEC_SEQ) for a in scan_s], _state_to_scan(state_wkv[l]), tc=DEC_SEQ)
        wkv_p.append(_state_from_scan(s_p, BATCH))
        wkv_s.append(_state_from_scan(s_s, DEC_BATCH))
        y_rw = _rwkv_post(_from_scan(y_p, SEQ), bonus_p, gate_p, rw_lnx_w[l], rw_lnx_b[l], 0, True)
        y_rw = _rwkv_post(_from_scan(y_s, DEC_SEQ).reshape(M_SAMPLE, RW), bonus_s, gate_s,
                          rw_lnx_w[l], rw_lnx_b[l], M_PROMPT, False, y_rw)

        s5p = _s5_params(s5_lam_re[l], s5_lam_im[l], s5_log_step[l], s5_b_re[l], s5_b_im[l],
                         s5_c_re[l], s5_c_im[l], s5_d[l])
        y5_p, hre_p, him_p = _s5_prompt(q_p, s5p)
        y5_s, hre_s, him_s = _s5_sample(q_s, state_ssm_re[l].reshape(DEC_BATCH, S5_G * S5_P),
                                        state_ssm_im[l].reshape(DEC_BATCH, S5_G * S5_P), s5p)
        re_p.append(hre_p.reshape(BATCH, S5_G, S5_P))
        im_p.append(him_p.reshape(BATCH, S5_G, S5_P))
        re_s.append(hre_s.reshape(DEC_BATCH, S5_G, S5_P))
        im_s.append(him_s.reshape(DEC_BATCH, S5_G, S5_P))
        y_s5 = _glu_norm(y5_p, w_glu, l, s5_b_glu[l], s5_out_norm[l], 0)
        y_s5 = _glu_norm(y5_s, w_glu, l, s5_b_glu[l], s5_out_norm[l], M_PROMPT, y_s5)

        x = _res_matmul([y_rw, y_s5], wo, l, x, 1.0, tm=1088, tn=512, vmem_mib=48)
        x = _ffn(x, ffn2_norm[l], ffn2_w_gate, ffn2_w_up, wd2, l)

    y_prompt = _rmsnorm(x, final_norm, F32, ROW_TM, 0, M_PROMPT).reshape(BATCH, SEQ, D_MODEL)
    y_sample = _rmsnorm(x, final_norm, F32, ROW_TM, M_PROMPT, M_SAMPLE).reshape(
        DEC_SEQ, DEC_BATCH, D_MODEL).transpose(1, 0, 2)
    st = jnp.stack
    return (y_prompt, y_sample, st(shift_p), st(wkv_p), st(re_p), st(im_p),
            st(shift_s), st(wkv_s), st(re_s), st(im_s))
```

```python
import functools
import math

import jax
import jax.numpy as jnp
from jax import lax
from jax.experimental import pallas as pl
from jax.experimental.pallas import tpu as pltpu

F32 = jnp.float32
BF16 = jnp.bfloat16

D_MODEL = 4096
BATCH = 4
SEQ = 2048
DEPTH = 2
DEC_BATCH = 128
DEC_SEQ = 4
M_PROMPT = BATCH * SEQ
M_SAMPLE = DEC_BATCH * DEC_SEQ
M_ALL = M_PROMPT + M_SAMPLE

RW = D_MODEL // 2
HEAD = 64
HEADS = RW // HEAD
S5W = D_MODEL - RW
S5_C = 16
S5_G = S5W // S5_C
S5_P = 64
W_LORA = 96
A_LORA = 96
G_LORA = 256
LORA_PAD = 128
LORA_W = 2 * LORA_PAD + G_LORA
D_FF = 11008
EPS_RMS = 1e-6
EPS_GN = 64e-5

COL_LORA = 3 * RW
COL_S5 = COL_LORA + LORA_W
D_INP = COL_S5 + S5W

V7X_VMEM_BYTES = 64 * 1024 * 1024
LANES = 128
SUBLANES = 8

S5_GT = 8
S5_TILES = S5_G // S5_GT
S5_UW = S5_GT * S5_C
S5_HW = S5_GT * S5_P

ROW_TM = 256


def _cparams(semantics, vmem_mib):
    assert vmem_mib * 1024 * 1024 < V7X_VMEM_BYTES
    return pltpu.CompilerParams(dimension_semantics=semantics,
                                vmem_limit_bytes=vmem_mib * 1024 * 1024)


def _joined_out(in_specs, args, joined):
    if joined is None:
        return in_specs, args, {}
    return (in_specs + [pl.BlockSpec(memory_space=pl.ANY)], args + [joined], {len(args): 0})


def _rmsnorm_kernel(x_ref, g_ref, o_ref):
    x = x_ref[...]
    ms = jnp.mean(x * x, axis=-1, keepdims=True)
    o_ref[...] = (x * lax.rsqrt(ms + EPS_RMS) * g_ref[...]).astype(o_ref.dtype)


def _rmsnorm(x, g, out_dtype, tm, row0=0, rows=None, block_stride=1):
    d = x.shape[1]
    m = x.shape[0] if rows is None else rows
    return pl.pallas_call(
        _rmsnorm_kernel,
        grid=(m // tm,),
        in_specs=[pl.BlockSpec((tm, d), lambda i: (i * block_stride + row0 // tm, 0)),
                  pl.BlockSpec((1, d), lambda i: (0, 0))],
        out_specs=pl.BlockSpec((tm, d), lambda i: (i, 0)),
        out_shape=jax.ShapeDtypeStruct((m, d), out_dtype),
        compiler_params=_cparams(("parallel",), 40),
        name="rmsnorm",
    )(x, g.reshape(1, d))


def _gate_up_kernel(h_ref, wg_ref, wu_ref, o_ref):
    h = h_ref[...]
    a = jnp.dot(h, wg_ref[...].astype(BF16), preferred_element_type=F32)
    b = jnp.dot(h, wu_ref[...].astype(BF16), preferred_element_type=F32)
    o_ref[...] = (a * jax.nn.sigmoid(a) * b).astype(o_ref.dtype)


def _gate_up(h, wg, wu, layer, tm=2176, tn=256):
    m, d = h.shape
    f = wg.shape[2]
    w_spec = pl.BlockSpec((None, d, tn), lambda i, j: (layer, 0, j))
    return pl.pallas_call(
        _gate_up_kernel,
        grid=(m // tm, f // tn),
        in_specs=[pl.BlockSpec((tm, d), lambda i, j: (i, 0), pipeline_mode=pl.Buffered(1)), w_spec, w_spec],
        out_specs=pl.BlockSpec((tm, tn), lambda i, j: (i, j)),
        out_shape=jax.ShapeDtypeStruct((m, f), BF16),
        compiler_params=_cparams(("parallel", "arbitrary"), 56),
        name="ffn_gate_up",
    )(h, wg, wu)


def _res_matmul_kernel(*refs, n_pairs, scale):
    a_refs = refs[:n_pairs]
    w_refs = refs[n_pairs:2 * n_pairs]
    x_ref, o_ref = refs[2 * n_pairs], refs[2 * n_pairs + 1]
    acc = jnp.dot(a_refs[0][...], w_refs[0][...], preferred_element_type=F32)
    for a_ref, w_ref in zip(a_refs[1:], w_refs[1:]):
        acc = acc + jnp.dot(a_ref[...], w_ref[...], preferred_element_type=F32)
    if scale != 1.0:
        acc = scale * acc
    o_ref[...] = x_ref[...] + acc


def _res_matmul(a_list, w, layer, x, scale, tm, tn, vmem_mib):
    m, n = x.shape
    n_pairs = len(a_list)
    in_specs = [pl.BlockSpec((tm, a.shape[1]), lambda i, j: (i, 0)) for a in a_list]
    for p, a in enumerate(a_list):
        in_specs.append(pl.BlockSpec((None, a.shape[1], tn), lambda i, j, p=p: (layer, p, j)))
    in_specs.append(pl.BlockSpec((tm, tn), lambda i, j: (i, j)))
    assert sum(a.shape[1] for a in a_list) == w.shape[1]
    return pl.pallas_call(
        functools.partial(_res_matmul_kernel, n_pairs=n_pairs, scale=scale),
        grid=(m // tm, n // tn),
        in_specs=in_specs,
        out_specs=pl.BlockSpec((tm, tn), lambda i, j: (i, j)),
        out_shape=jax.ShapeDtypeStruct((m, n), F32),
        compiler_params=_cparams(("parallel", "arbitrary"), vmem_mib),
        name="res_matmul",
    )(*a_list, *([w] * n_pairs), x)


def _ffn(x, norm_g, wg, wu, wd_bf16, layer):
    h = _rmsnorm(x, norm_g, BF16, ROW_TM)
    act = _gate_up(h, wg, wu, layer)
    return _res_matmul([act], wd_bf16, layer, x, 0.5, tm=544, tn=512, vmem_mib=58)


WIN_TN = 256
WIN_MAIN_TILES = COL_LORA // WIN_TN
WIN_SHIFT_TILES = COL_S5 // WIN_TN


def _win_tile_cases(h_ref, wm_ref, wt_ref, j, emit):
    @pl.when(j < WIN_MAIN_TILES)
    def _():
        emit(jnp.dot(h_ref[...], wm_ref[...], preferred_element_type=F32), True)

    @pl.when((j >= WIN_MAIN_TILES) & (j < WIN_SHIFT_TILES))
    def _():
        emit(jnp.dot(h_ref[...], wt_ref[...], preferred_element_type=F32), True)

    @pl.when(j >= WIN_SHIFT_TILES)
    def _():
        emit(jnp.dot(h_ref[...], wt_ref[...], preferred_element_type=F32), False)


def _win_prompt_kernel(h_ref, wm_ref, wt_ref, mu_ref, o_ref):
    def emit(p, shifted):
        if shifted:
            rows = lax.broadcasted_iota(jnp.int32, p.shape, 0)
            prev = jnp.where(rows == 0, 0.0, pltpu.roll(p, 1, 0))
            o_ref[...] = p + (prev - p) * mu_ref[...]
        else:
            o_ref[...] = p

    _win_tile_cases(h_ref, wm_ref, wt_ref, pl.program_id(1), emit)


def _win_sample_kernel(h_ref, wm_ref, wt_ref, mu_ref, o_ref):
    def emit(p, shifted):
        cur = p[DEC_BATCH:]
        o_ref[...] = cur + (p[:M_SAMPLE] - cur) * mu_ref[...] if shifted else cur

    _win_tile_cases(h_ref, wm_ref, wt_ref, pl.program_id(0), emit)


def _win_weight_specs(layer, col_of):
    main = pl.BlockSpec((None, D_MODEL, WIN_TN),
                        lambda *g: (layer, 0, jnp.minimum(col_of(*g), WIN_MAIN_TILES - 1)))
    tail = pl.BlockSpec((None, D_MODEL, WIN_TN),
                        lambda *g: (layer, 0, jnp.maximum(col_of(*g) - WIN_MAIN_TILES, 0)))
    mu = pl.BlockSpec((None, 1, WIN_TN), lambda *g: (layer, 0, col_of(*g)))
    return [main, tail, mu]


def _win_prompt(h, w_main, w_tail, mu_pad, layer):
    tn = WIN_TN
    return pl.pallas_call(
        _win_prompt_kernel,
        grid=(BATCH, D_INP // tn),
        in_specs=[pl.BlockSpec((SEQ, D_MODEL), lambda b, j: (b, 0), pipeline_mode=pl.Buffered(1))]
                 + _win_weight_specs(layer, lambda b, j: j),
        out_specs=pl.BlockSpec((SEQ, tn), lambda b, j: (b, j)),
        out_shape=jax.ShapeDtypeStruct((M_PROMPT, D_INP), F32),
        compiler_params=_cparams(("parallel", "arbitrary"), 56),
        name="in_proj_prompt",
    )(h, w_main, w_tail, mu_pad)


def _win_sample(h_rows, w_main, w_tail, mu_pad, layer):
    tn = WIN_TN
    rows = h_rows.shape[0]
    return pl.pallas_call(
        _win_sample_kernel,
        grid=(D_INP // tn,),
        in_specs=[pl.BlockSpec((rows, D_MODEL), lambda j: (0, 0))] + _win_weight_specs(layer, lambda j: j),
        out_specs=pl.BlockSpec((M_SAMPLE, tn), lambda j: (0, j)),
        out_shape=jax.ShapeDtypeStruct((M_SAMPLE, D_INP), F32),
        compiler_params=_cparams(("arbitrary",), 32),
        name="in_proj_sample",
    )(h_rows, w_main, w_tail, mu_pad)


def _in_proj_tail(w_in, mu):
    o_w, o_a, o_g = COL_LORA, COL_LORA + W_LORA, COL_LORA + W_LORA + A_LORA
    o_s = o_g + G_LORA
    w_main = w_in.astype(BF16)
    zw = jnp.zeros((DEPTH, D_MODEL, LORA_PAD - W_LORA), BF16)
    w_tail = jnp.concatenate([w_main[:, :, o_w:o_a], zw, w_main[:, :, o_a:o_g], zw,
                              w_main[:, :, o_g:]], axis=2)
    zm = jnp.zeros((DEPTH, LORA_PAD - W_LORA), mu.dtype)
    mu_pad = jnp.concatenate([mu[:, :o_w], mu[:, o_w:o_a], zm, mu[:, o_a:o_g], zm, mu[:, o_g:o_s],
                              jnp.zeros((DEPTH, S5W), mu.dtype)], axis=1).reshape(DEPTH, 1, D_INP)
    return w_main, w_tail, mu_pad


def _split_bf16(x):
    hi = x.astype(BF16)
    return hi, (x - hi.astype(F32)).astype(BF16)


def _head_sum(x):
    r = lax.broadcasted_iota(jnp.int32, (LANES, LANES), 0) // HEAD
    c = lax.broadcasted_iota(jnp.int32, (LANES, LANES), 1) // HEAD
    ones = (r == c).astype(BF16)
    hi = x.astype(BF16)
    r1 = x - hi.astype(F32)
    mid = r1.astype(BF16)
    lo = (r1 - mid.astype(F32)).astype(BF16)
    outs = []
    for s in range(x.shape[1] // LANES):
        sl = slice(s * LANES, (s + 1) * LANES)
        acc = jnp.dot(hi[:, sl], ones, preferred_element_type=F32)
        acc = acc + jnp.dot(mid[:, sl], ones, preferred_element_type=F32)
        acc = acc + jnp.dot(lo[:, sl], ones, preferred_element_type=F32)
        outs.append(acc)
    return jnp.concatenate(outs, axis=1)


def _rwkv_pre_kernel(r_ref, k_ref, v_ref, lora_ref, w0_ref, a0_ref, kk_ref, ka_ref, rk_ref,
                     wup_ref, aup_ref, gup_ref,
                     r_out, w_out, k_out, v_out, kk_out, b_out, g_out, bonus_out):
    r = r_ref[...]
    k = k_ref[...]
    v = v_ref[...]
    r_out[...] = r
    v_out[...] = v
    lora = lora_ref[...]
    wd = lora[:, :LORA_PAD]
    ad = lora[:, LORA_PAD:2 * LORA_PAD]
    gd = lora[:, 2 * LORA_PAD:]
    z = -(w0_ref[...] + jnp.dot(jnp.tanh(wd).astype(BF16), wup_ref[...], preferred_element_type=F32))
    softplus = jnp.maximum(z, 0.0) + jnp.log1p(jnp.exp(-jnp.abs(z)))
    w_out[...] = jnp.exp(-jnp.exp(-softplus - 0.5))
    a = jax.nn.sigmoid(a0_ref[...] + jnp.dot(ad.astype(BF16), aup_ref[...], preferred_element_type=F32))
    g_out[...] = jnp.dot(jax.nn.sigmoid(gd).astype(BF16), gup_ref[...], preferred_element_type=F32)
    kk = k * kk_ref[...]
    k2 = k * (1.0 + (a - 1.0) * ka_ref[...])
    k_out[...] = k2
    kkn = kk / jnp.maximum(jnp.sqrt(_head_sum(kk * kk)), 1e-12)
    kk_out[...] = kkn
    b_out[...] = kkn * a
    bonus_out[...] = _head_sum(r * k2 * rk_ref[...]) * v


def _time_major_spec(m, tm, time_major):
    if not time_major:
        return pl.BlockSpec((tm, RW), lambda i: (i, 0)), (m, RW)
    per_seq = SEQ // tm
    return pl.BlockSpec((tm, RW), lambda i: (i % per_seq, i // per_seq)), (SEQ, (m // SEQ) * RW)


def _rwkv_pre(q, w0, a0, k_k, k_a, r_k, w_up, a_up, g_up, time_major, tm=ROW_TM):
    m = q.shape[0]
    out_spec, out_dims = _time_major_spec(m, tm, time_major)
    row = lambda v: v.reshape(1, RW)
    pad_rows = lambda u: jnp.concatenate(
        [u, jnp.zeros((LORA_PAD - u.shape[0], RW), u.dtype)], axis=0).astype(BF16)
    vec = pl.BlockSpec((1, RW), lambda i: (0, 0))
    full = lambda rows: pl.BlockSpec((rows, RW), lambda i: (0, 0))
    out = jax.ShapeDtypeStruct(out_dims, F32)
    return pl.pallas_call(
        _rwkv_pre_kernel,
        grid=(m // tm,),
        in_specs=[pl.BlockSpec((tm, RW), lambda i: (i, 0)),
                  pl.BlockSpec((tm, RW), lambda i: (i, 1)),
                  pl.BlockSpec((tm, RW), lambda i: (i, 2)),
                  pl.BlockSpec((tm, LORA_W), lambda i: (i, COL_LORA // LORA_W)),
                  vec, vec, vec, vec, vec, full(LORA_PAD), full(LORA_PAD), full(G_LORA)],
        out_specs=[out_spec] * 8,
        out_shape=[out] * 8,
        compiler_params=_cparams(("parallel",), 56),
        name="rwkv_pre",
    )(q, q, q, q, row(w0), row(a0), row(k_k), row(k_a), row(r_k.reshape(RW)),
      pad_rows(w_up), pad_rows(a_up), g_up.astype(BF16))


def _rwkv_post_kernel(y_ref, bonus_ref, g_ref, lw_ref, lb_ref, *rest):
    o_ref = rest[-1]
    y = y_ref[...]
    mean = _head_sum(y) * (1.0 / HEAD)
    c = y - mean
    var = _head_sum(c * c) * (1.0 / HEAD)
    yn = c * lax.rsqrt(var + EPS_GN) * lw_ref[...] + lb_ref[...]
    o_ref[...] = ((yn + bonus_ref[...]) * g_ref[...]).astype(o_ref.dtype)


def _rwkv_post(y, bonus, g, lnx_w, lnx_b, row0, time_major, joined=None, tm=ROW_TM):
    m = y.size // RW
    blk, _ = _time_major_spec(m, tm, time_major)
    vec = pl.BlockSpec((1, RW), lambda i: (0, 0))
    in_specs, args, aliases = _joined_out(
        [blk, blk, blk, vec, vec], [y, bonus, g, lnx_w.reshape(1, RW), lnx_b.reshape(1, RW)], joined)
    return pl.pallas_call(
        _rwkv_post_kernel,
        grid=(m // tm,),
        in_specs=in_specs,
        out_specs=pl.BlockSpec((tm, RW), lambda i: (i + row0 // tm, 0)),
        out_shape=jax.ShapeDtypeStruct((M_ALL, RW), BF16),
        input_output_aliases=aliases,
        compiler_params=_cparams(("parallel",), 40),
        name="rwkv_post",
    )(*args)


WKV_UNROLL = 8


def _wkv_kernel(r_ref, w_ref, k_ref, v_ref, kk_ref, b_ref, s0_ref, y_ref, s_ref,
                g_ref, kq_ref, wr_ref, bt_ref, kt_ref, *, steps):
    @pl.when(pl.program_id(1) == 0)
    def _():
        s_ref[...] = s0_ref[...]

    g_ref[...] = jnp.ones((HEAD, LANES), F32)

    def step(t, carry):
        r, k, b = r_ref[t], k_ref[t], b_ref[t]
        g_prev = g_ref[...]
        g = g_prev * w_ref[t]
        g_inv = 1.0 / g
        g_ref[...] = g
        kq_ref[...] = g_prev * kk_ref[t]
        wr_ref[...] = g * r
        bt_ref[...] = b * g_inv
        kt_ref[...] = k * g_inv
        beta = jnp.sum(b * r, axis=0, keepdims=True)
        kappa = jnp.sum(k * r, axis=0, keepdims=True)
        v = v_ref[t]

        def contract(j, acc):
            sa, u = acc
            sj = s_ref[j]
            return sa - sj * kq_ref[pl.ds(j, 1), :], u + sj * wr_ref[pl.ds(j, 1), :]

        zero = jnp.zeros((HEAD, LANES), F32)
        sa, u = lax.fori_loop(0, HEAD, contract, (zero, zero), unroll=WKV_UNROLL)

        def update(j, c):
            s_ref[j] = s_ref[j] + sa * bt_ref[pl.ds(j, 1), :] + v * kt_ref[pl.ds(j, 1), :]
            return c

        lax.fori_loop(0, HEAD, update, 0, unroll=WKV_UNROLL)
        y_ref[t] = u + sa * beta + v * kappa
        return carry

    lax.fori_loop(0, steps, step, 0)

    def denormalise(j, c):
        s_ref[j] = s_ref[j] * g_ref[pl.ds(j, 1), :]
        return c

    lax.fori_loop(0, HEAD, denormalise, 0, unroll=WKV_UNROLL)


def _wkv_scan(r, w, k, v, kk, b, s0, tc):
    length, _, n = r.shape
    vec = pl.BlockSpec((tc, HEAD, LANES), lambda c, t: (t, 0, c))
    st = pl.BlockSpec((HEAD, HEAD, LANES), lambda c, t: (0, 0, c))
    return pl.pallas_call(
        functools.partial(_wkv_kernel, steps=tc),
        grid=(n // LANES, length // tc),
        in_specs=[vec] * 6 + [st],
        out_specs=[vec, st],
        out_shape=[jax.ShapeDtypeStruct((length, HEAD, n), F32),
                   jax.ShapeDtypeStruct((HEAD, HEAD, n), F32)],
        scratch_shapes=[pltpu.VMEM((HEAD, LANES), F32)] * 5,
        compiler_params=_cparams(("parallel", "arbitrary"), 40),
        name="wkv_scan",
    )(r, w, k, v, kk, b, s0)


def _to_scan(a, length):
    return a.reshape(length, -1, HEAD).transpose(0, 2, 1)


def _from_scan(y, length):
    return y.transpose(0, 2, 1).reshape(length, -1)


def _state_to_scan(s):
    n = s.shape[0] * s.shape[1]
    s = lax.optimization_barrier(s.reshape(n, HEAD, HEAD).transpose(0, 2, 1))
    return s.reshape(n, HEAD * HEAD).T.reshape(HEAD, HEAD, n)


def _state_from_scan(s, batch):
    n = s.shape[2]
    s = lax.optimization_barrier(s.reshape(HEAD * HEAD, n).T)
    return s.reshape(n, HEAD, HEAD).transpose(0, 2, 1).reshape(batch, HEADS, HEAD, HEAD)


def _s5_discretize(lre_ref, lim_ref, ls_ref, bre_ref, bim_ref):
    lre, lim = lre_ref[...], lim_ref[...]
    step = jnp.exp(ls_ref[...])
    mag = jnp.exp(lre * step)
    ar = mag * jnp.cos(lim * step)
    ai = mag * jnp.sin(lim * step)
    den = lre * lre + lim * lim
    nr = ar - 1.0
    f_re = (nr * lre + ai * lim) / den
    f_im = (ai * lre - nr * lim) / den
    bre, bim = bre_ref[...], bim_ref[...]
    return ar, ai, f_re * bre - f_im * bim, f_re * bim + f_im * bre


def _dot3(a, b):
    a_hi, a_lo = _split_bf16(a)
    b_hi, b_lo = _split_bf16(b)
    acc = jnp.dot(a_hi, b_hi, preferred_element_type=F32)
    acc = acc + jnp.dot(a_lo, b_hi, preferred_element_type=F32)
    return acc + jnp.dot(a_hi, b_lo, preferred_element_type=F32)


def _s5_readout(hr, hi, cre_ref, cim_ref, d_ref, u):
    y = jnp.dot(hr.astype(BF16), cre_ref[...].astype(BF16), preferred_element_type=F32)
    y = y - jnp.dot(hi.astype(BF16), cim_ref[...].astype(BF16), preferred_element_type=F32)
    return y + d_ref[...] * u


def _cmul(ar, ai, br, bi):
    return ar * br - ai * bi, ar * bi + ai * br


def _s5_prompt_kernel(u_ref, lre_ref, lim_ref, ls_ref, bre_ref, bim_ref, cre_ref, cim_ref, d_ref,
                      y_ref, hre_ref, him_ref, hr_scr, hi_scr):
    ar, ai, bbr, bbi = _s5_discretize(lre_ref, lim_ref, ls_ref, bre_ref, bim_ref)
    u = u_ref[...]
    hr_scr[...] = _dot3(u, bbr)
    hi_scr[...] = _dot3(u, bbi)

    rows = lax.broadcasted_iota(jnp.int32, (SUBLANES, S5_HW), 0)
    powers = [(ar, ai)]
    for _ in range(SUBLANES - 1):
        powers.append(_cmul(*powers[-1], ar, ai))
    levels = []
    for sh in (1, 2, 4):
        pr, pi = powers[sh - 1]
        levels.append((sh, jnp.where(rows >= sh, pr, 0.0), jnp.where(rows >= sh, pi, 0.0)))
    cpr = jnp.zeros((SUBLANES, S5_HW), F32)
    cpi = jnp.zeros((SUBLANES, S5_HW), F32)
    for n, (pr, pi) in enumerate(powers):
        cpr = jnp.where(rows == n, pr, cpr)
        cpi = jnp.where(rows == n, pi, cpi)

    def tile(i, carry):
        cr, ci = carry
        r0 = pl.multiple_of(i * SUBLANES, SUBLANES)
        xr = hr_scr[pl.ds(r0, SUBLANES), :]
        xi = hi_scr[pl.ds(r0, SUBLANES), :]
        for sh, mr, mi in levels:
            sr, si = pltpu.roll(xr, sh, 0), pltpu.roll(xi, sh, 0)
            xr, xi = xr + (mr * sr - mi * si), xi + (mr * si + mi * sr)
        xr = xr + (cpr * cr - cpi * ci)
        xi = xi + (cpr * ci + cpi * cr)
        hr_scr[pl.ds(r0, SUBLANES), :] = xr
        hi_scr[pl.ds(r0, SUBLANES), :] = xi
        last = SUBLANES - 1
        return (jnp.broadcast_to(xr[last:, :], (SUBLANES, S5_HW)),
                jnp.broadcast_to(xi[last:, :], (SUBLANES, S5_HW)))

    zero = jnp.zeros((SUBLANES, S5_HW), F32)
    cr, ci = lax.fori_loop(0, SEQ // SUBLANES, tile, (zero, zero), unroll=2)
    hre_ref[...] = cr[:1]
    him_ref[...] = ci[:1]
    y_ref[...] = _s5_readout(hr_scr[...], hi_scr[...], cre_ref, cim_ref, d_ref, u)


def _s5_sample_kernel(u_ref, h0r_ref, h0i_ref, lre_ref, lim_ref, ls_ref, bre_ref, bim_ref, cre_ref, cim_ref,
                      d_ref, y_ref, hre_ref, him_ref, hr_scr, hi_scr):
    ar, ai, bbr, bbi = _s5_discretize(lre_ref, lim_ref, ls_ref, bre_ref, bim_ref)
    u = u_ref[...]
    bu_r = _dot3(u, bbr)
    bu_i = _dot3(u, bbi)
    hr, hi = h0r_ref[...], h0i_ref[...]
    for t in range(DEC_SEQ):
        rows = slice(t * DEC_BATCH, (t + 1) * DEC_BATCH)
        hr, hi = ar * hr - ai * hi + bu_r[rows], ar * hi + ai * hr + bu_i[rows]
        hr_scr[rows, :] = hr
        hi_scr[rows, :] = hi
    hre_ref[...] = hr
    him_ref[...] = hi
    y_ref[...] = _s5_readout(hr_scr[...], hi_scr[...], cre_ref, cim_ref, d_ref, u)


def _s5_params(lam_re, lam_im, log_step, b_re, b_im, c_re, c_im, d):
    eye = jnp.eye(S5_GT, dtype=F32)
    chan = lambda a: a.reshape(S5_TILES, 1, S5_HW)

    def b_blk(b):
        x = b.transpose(0, 2, 1).reshape(S5_TILES, S5_GT, S5_C, S5_P)
        return jnp.einsum('tgcp,gh->tgchp', x, eye).reshape(S5_TILES, S5_UW, S5_HW)

    def c_blk(c):
        x = c.transpose(0, 2, 1).reshape(S5_TILES, S5_GT, S5_P, S5_C)
        return jnp.einsum('tgpc,gh->tgphc', x, eye).reshape(S5_TILES, S5_HW, S5_UW)

    ls = jnp.broadcast_to(log_step[:, None], (S5_G, S5_P))
    return (chan(lam_re), chan(lam_im), chan(ls), b_blk(b_re), b_blk(b_im), c_blk(c_re), c_blk(c_im),
            d.reshape(S5_TILES, 1, S5_UW))


def _s5_param_specs(tile_of):
    chan = pl.BlockSpec((None, 1, S5_HW), lambda *g: (tile_of(*g), 0, 0))
    bspec = pl.BlockSpec((None, S5_UW, S5_HW), lambda *g: (tile_of(*g), 0, 0))
    cspec = pl.BlockSpec((None, S5_HW, S5_UW), lambda *g: (tile_of(*g), 0, 0))
    dspec = pl.BlockSpec((None, 1, S5_UW), lambda *g: (tile_of(*g), 0, 0))
    return [chan, chan, chan, bspec, bspec, cspec, cspec, dspec]


def _s5_prompt(q, params):
    col0 = COL_S5 // S5_UW
    state = jax.ShapeDtypeStruct((BATCH, 1, S5_G * S5_P), F32)
    st_spec = pl.BlockSpec((None, 1, S5_HW), lambda b, g: (b, 0, g))
    return pl.pallas_call(
        _s5_prompt_kernel,
        grid=(BATCH, S5_TILES),
        in_specs=[pl.BlockSpec((SEQ, S5_UW), lambda b, g: (b, col0 + g))] + _s5_param_specs(lambda b, g: g),
        out_specs=[pl.BlockSpec((SEQ, S5_UW), lambda b, g: (b, g)), st_spec, st_spec],
        out_shape=[jax.ShapeDtypeStruct((M_PROMPT, S5W), F32), state, state],
        scratch_shapes=[pltpu.VMEM((SEQ, S5_HW), F32), pltpu.VMEM((SEQ, S5_HW), F32)],
        compiler_params=_cparams(("parallel", "parallel"), 48),
        name="s5_prompt",
    )(q, *params)


def _s5_sample(q, h0_re, h0_im, params):
    col0 = COL_S5 // S5_UW
    state = jax.ShapeDtypeStruct((DEC_BATCH, S5_G * S5_P), F32)
    st_spec = pl.BlockSpec((DEC_BATCH, S5_HW), lambda g: (0, g))
    return pl.pallas_call(
        _s5_sample_kernel,
        grid=(S5_TILES,),
        in_specs=[pl.BlockSpec((M_SAMPLE, S5_UW), lambda g: (0, col0 + g)), st_spec, st_spec]
                 + _s5_param_specs(lambda g: g),
        out_specs=[pl.BlockSpec((M_SAMPLE, S5_UW), lambda g: (0, g)), st_spec, st_spec],
        out_shape=[jax.ShapeDtypeStruct((M_SAMPLE, S5W), F32), state, state],
        scratch_shapes=[pltpu.VMEM((M_SAMPLE, S5_HW), F32), pltpu.VMEM((M_SAMPLE, S5_HW), F32)],
        compiler_params=_cparams(("parallel",), 32),
        name="s5_sample",
    )(q, h0_re, h0_im, *params)


def _glu_norm_kernel(y_ref, w_ref, b_ref, g_ref, *rest):
    o_ref = rest[-1]
    y = y_ref[...]
    z = 0.5 * y * (1.0 + lax.erf(y * math.sqrt(0.5)))
    gate = jax.nn.sigmoid(jnp.dot(z.astype(BF16), w_ref[...], preferred_element_type=F32) + b_ref[...])
    out = z * gate
    ms = jnp.mean(out * out, axis=-1, keepdims=True)
    o_ref[...] = (out * lax.rsqrt(ms + EPS_RMS) * g_ref[...]).astype(o_ref.dtype)


def _glu_norm(y, w_glu_bf16, layer, b_glu, gain, row0, joined=None, tm=ROW_TM):
    m = y.shape[0]
    vec = pl.BlockSpec((1, S5W), lambda i: (0, 0))
    in_specs, args, aliases = _joined_out(
        [pl.BlockSpec((tm, S5W), lambda i: (i, 0)),
         pl.BlockSpec((None, S5W, S5W), lambda i: (layer, 0, 0)), vec, vec],
        [y, w_glu_bf16, b_glu.reshape(1, S5W), gain.reshape(1, S5W)], joined)
    return pl.pallas_call(
        _glu_norm_kernel,
        grid=(m // tm,),
        in_specs=in_specs,
        out_specs=pl.BlockSpec((tm, S5W), lambda i: (i + row0 // tm, 0)),
        out_shape=jax.ShapeDtypeStruct((M_ALL, S5W), BF16),
        input_output_aliases=aliases,
        compiler_params=_cparams(("parallel",), 48),
        name="s5_glu_norm",
    )(*args)


def kernel(x_prompt, x_sample, state_shift, state_wkv, state_ssm_re, state_ssm_im, ffn1_norm, ffn1_w_gate, ffn1_w_up, ffn1_w_down, mix_norm, w_in, shift_mu, rw_w0, rw_w_up, rw_a0, rw_a_up, rw_g_up, rw_k_k, rw_k_a, rw_r_k, rw_lnx_w, rw_lnx_b, s5_lam_re, s5_lam_im, s5_b_re, s5_b_im, s5_c_re, s5_c_im, s5_d, s5_log_step, s5_w_glu, s5_b_glu, s5_out_norm, w_out, ffn2_norm, ffn2_w_gate, ffn2_w_up, ffn2_w_down, final_norm):
    x = jnp.concatenate([x_prompt.reshape(M_PROMPT, D_MODEL),
                         x_sample.transpose(1, 0, 2).reshape(M_SAMPLE, D_MODEL)], axis=0)
    zero_state = jnp.zeros((HEAD, HEAD, BATCH * HEADS), F32)
    shift_p, wkv_p, re_p, im_p, shift_s, wkv_s, re_s, im_s = ([] for _ in range(8))
    wd1, wd2 = ffn1_w_down.astype(BF16), ffn2_w_down.astype(BF16)
    wo, w_glu = w_out.astype(BF16), s5_w_glu.astype(BF16)
    w_main, w_tail, mu_pad = _in_proj_tail(w_in, shift_mu)

    for l in range(DEPTH):
        x = _ffn(x, ffn1_norm[l], ffn1_w_gate, ffn1_w_up, wd1, l)

        h_mix = _rmsnorm(x, mix_norm[l], BF16, ROW_TM)
        last8 = _rmsnorm(x, mix_norm[l], F32, SUBLANES, SEQ - SUBLANES, BATCH * SUBLANES, SEQ // SUBLANES)
        shift_p.append(last8[SUBLANES - 1::SUBLANES])
        shift_s.append(_rmsnorm(x, mix_norm[l], F32, DEC_BATCH, M_ALL - DEC_BATCH, DEC_BATCH))

        q_p = _win_prompt(h_mix, w_main, w_tail, mu_pad, l)
        q_s = _win_sample(jnp.concatenate([state_shift[l].astype(BF16), h_mix[M_PROMPT:]], axis=0),
                          w_main, w_tail, mu_pad, l)

        rw_args = (rw_w0[l], rw_a0[l], rw_k_k[l], rw_k_a[l], rw_r_k[l], rw_w_up[l], rw_a_up[l], rw_g_up[l])
        *scan_p, gate_p, bonus_p = _rwkv_pre(q_p, *rw_args, time_major=True)
        *scan_s, gate_s, bonus_s = _rwkv_pre(q_s, *rw_args, time_major=False)
        scan_p = [_to_scan(a, SEQ) for a in scan_p]
        scan_s = [_to_scan(a, DEC_SEQ) for a in scan_s]

        s5p = _s5_params(s5_lam_re[l], s5_lam_im[l], s5_log_step[l], s5_b_re[l], s5_b_im[l],
                         s5_c_re[l], s5_c_im[l], s5_d[l])
        y5_p, hre_p, him_p = _s5_prompt(q_p, s5p)
        y5_s, hre_s, him_s = _s5_sample(q_s, state_ssm_re[l].reshape(DEC_BATCH, S5_G * S5_P),
                                        state_ssm_im[l].reshape(DEC_BATCH, S5_G * S5_P), s5p)
        re_p.append(hre_p.reshape(BATCH, S5_G, S5_P))
        im_p.append(him_p.reshape(BATCH, S5_G, S5_P))
        re_s.append(hre_s.reshape(DEC_BATCH, S5_G, S5_P))
        im_s.append(him_s.reshape(DEC_BATCH, S5_G, S5_P))
        y_s5 = _glu_norm(y5_p, w_glu, l, s5_b_glu[l], s5_out_norm[l], 0)
        y_s5 = _glu_norm(y5_s, w_glu, l, s5_b_glu[l], s5_out_norm[l], M_PROMPT, y_s5)

        y_p, s_p = _wkv_scan(*scan_p, zero_state, tc=32)
        y_s, s_s = _wkv_scan(*scan_s, _state_to_scan(state_wkv[l]), tc=DEC_SEQ)
        wkv_p.append(_state_from_scan(s_p, BATCH))
        wkv_s.append(_state_from_scan(s_s, DEC_BATCH))
        y_rw = _rwkv_post(_from_scan(y_p, SEQ), bonus_p, gate_p, rw_lnx_w[l], rw_lnx_b[l], 0, True)
        y_rw = _rwkv_post(_from_scan(y_s, DEC_SEQ).reshape(M_SAMPLE, RW), bonus_s, gate_s,
                          rw_lnx_w[l], rw_lnx_b[l], M_PROMPT, False, y_rw)

        x = _res_matmul([y_rw, y_s5], wo, l, x, 1.0, tm=1088, tn=512, vmem_mib=48)
        x = _ffn(x, ffn2_norm[l], ffn2_w_gate, ffn2_w_up, wd2, l)

    y_prompt = _rmsnorm(x, final_norm, F32, ROW_TM, 0, M_PROMPT).reshape(BATCH, SEQ, D_MODEL)
    y_sample = _rmsnorm(x, final_norm, F32, ROW_TM, M_PROMPT, M_SAMPLE).reshape(
        DEC_SEQ, DEC_BATCH, D_MODEL).transpose(1, 0, 2)
    st = jnp.stack
    return (y_prompt, y_sample, st(shift_p), st(wkv_p), st(re_p), st(im_p),
            st(shift_s), st(wkv_s), st(re_s), st(im_s))
```

```python
import functools
import math

import jax
import jax.numpy as jnp
from jax import lax
from jax.experimental import pallas as pl
from jax.experimental.pallas import tpu as pltpu

F32 = jnp.float32
BF16 = jnp.bfloat16

D_MODEL = 4096
BATCH = 4
SEQ = 2048
DEPTH = 2
DEC_BATCH = 128
DEC_SEQ = 4
M_PROMPT = BATCH * SEQ
M_SAMPLE = DEC_BATCH * DEC_SEQ
M_ALL = M_PROMPT + M_SAMPLE

RW = D_MODEL // 2
HEAD = 64
HEADS = RW // HEAD
S5W = D_MODEL - RW
S5_C = 16
S5_G = S5W // S5_C
S5_P = 64
W_LORA = 96
A_LORA = 96
G_LORA = 256
LORA_PAD = 128
LORA_W = 2 * LORA_PAD + G_LORA
D_FF = 11008
EPS_RMS = 1e-6
EPS_GN = 64e-5

COL_LORA = 3 * RW
COL_S5 = COL_LORA + LORA_W
D_INP = COL_S5 + S5W

V7X_VMEM_BYTES = 64 * 1024 * 1024
LANES = 128
SUBLANES = 8

S5_GT = 8
S5_TILES = S5_G // S5_GT
S5_UW = S5_GT * S5_C
S5_HW = S5_GT * S5_P

ROW_TM = 256


def _cparams(semantics, vmem_mib):
    assert vmem_mib * 1024 * 1024 < V7X_VMEM_BYTES
    return pltpu.CompilerParams(dimension_semantics=semantics,
                                vmem_limit_bytes=vmem_mib * 1024 * 1024)


def _joined_out(in_specs, args, joined):
    if joined is None:
        return in_specs, args, {}
    return (in_specs + [pl.BlockSpec(memory_space=pl.ANY)], args + [joined], {len(args): 0})


def _rmsnorm_kernel(x_ref, g_ref, o_ref):
    x = x_ref[...]
    ms = jnp.mean(x * x, axis=-1, keepdims=True)
    o_ref[...] = (x * lax.rsqrt(ms + EPS_RMS) * g_ref[...]).astype(o_ref.dtype)


def _rmsnorm(x, g, out_dtype, tm, row0=0, rows=None, block_stride=1):
    d = x.shape[1]
    m = x.shape[0] if rows is None else rows
    return pl.pallas_call(
        _rmsnorm_kernel,
        grid=(m // tm,),
        in_specs=[pl.BlockSpec((tm, d), lambda i: (i * block_stride + row0 // tm, 0)),
                  pl.BlockSpec((1, d), lambda i: (0, 0))],
        out_specs=pl.BlockSpec((tm, d), lambda i: (i, 0)),
        out_shape=jax.ShapeDtypeStruct((m, d), out_dtype),
        compiler_params=_cparams(("parallel",), 40),
        name="rmsnorm",
    )(x, g.reshape(1, d))


WD_CAST_ROWS = 128


def _gate_up_kernel(h_ref, wg_ref, wu_ref, wd_ref, o_ref, wd_bf16_ref, *, cast_passes):
    h = h_ref[...]
    a = jnp.dot(h, wg_ref[...].astype(BF16), preferred_element_type=F32)
    b = jnp.dot(h, wu_ref[...].astype(BF16), preferred_element_type=F32)
    o_ref[...] = (a * jax.nn.sigmoid(a) * b).astype(o_ref.dtype)

    @pl.when(pl.program_id(0) < cast_passes)
    def _():
        wd_bf16_ref[...] = wd_ref[...].astype(BF16)


def _gate_up(h, wg, wu, wd, layer, tm=2176, tn=256):
    m, d = h.shape
    f = wg.shape[2]
    n_i, n_j = m // tm, f // tn
    cast_blocks = f // WD_CAST_ROWS
    cast_passes = cast_blocks // n_j
    assert cast_passes * n_j == cast_blocks and cast_passes <= n_i

    def cast_block(i, j):
        return jnp.where(i < cast_passes, i * n_j + j, cast_blocks - 1)

    w_spec = pl.BlockSpec((None, d, tn), lambda i, j: (layer, 0, j))
    return pl.pallas_call(
        functools.partial(_gate_up_kernel, cast_passes=cast_passes),
        grid=(n_i, n_j),
        in_specs=[pl.BlockSpec((tm, d), lambda i, j: (i, 0), pipeline_mode=pl.Buffered(1)), w_spec, w_spec,
                  pl.BlockSpec((None, WD_CAST_ROWS, d), lambda i, j: (layer, cast_block(i, j), 0))],
        out_specs=[pl.BlockSpec((tm, tn), lambda i, j: (i, j)),
                   pl.BlockSpec((WD_CAST_ROWS, d), lambda i, j: (cast_block(i, j), 0))],
        out_shape=[jax.ShapeDtypeStruct((m, f), BF16), jax.ShapeDtypeStruct((f, d), BF16)],
        compiler_params=_cparams(("arbitrary", "arbitrary"), 58),
        name="ffn_gate_up",
    )(h, wg, wu, wd)


def _res_matmul_kernel(*refs, n_pairs, scale):
    a_refs = refs[:n_pairs]
    w_refs = refs[n_pairs:2 * n_pairs]
    x_ref, o_ref = refs[2 * n_pairs], refs[2 * n_pairs + 1]
    acc = jnp.dot(a_refs[0][...], w_refs[0][...], preferred_element_type=F32)
    for a_ref, w_ref in zip(a_refs[1:], w_refs[1:]):
        acc = acc + jnp.dot(a_ref[...], w_ref[...], preferred_element_type=F32)
    if scale != 1.0:
        acc = scale * acc
    o_ref[...] = x_ref[...] + acc


def _res_matmul(a_list, w, layer, x, scale, tm, tn, vmem_mib):
    m, n = x.shape
    n_pairs = len(a_list)
    in_specs = [pl.BlockSpec((tm, a.shape[1]), lambda i, j: (i, 0)) for a in a_list]
    for p, a in enumerate(a_list):
        if layer is None:
            in_specs.append(pl.BlockSpec((a.shape[1], tn), lambda i, j, p=p: (p, j)))
        else:
            in_specs.append(pl.BlockSpec((None, a.shape[1], tn), lambda i, j, p=p: (layer, p, j)))
    in_specs.append(pl.BlockSpec((tm, tn), lambda i, j: (i, j)))
    assert sum(a.shape[1] for a in a_list) == w.shape[-2]
    return pl.pallas_call(
        functools.partial(_res_matmul_kernel, n_pairs=n_pairs, scale=scale),
        grid=(m // tm, n // tn),
        in_specs=in_specs,
        out_specs=pl.BlockSpec((tm, tn), lambda i, j: (i, j)),
        out_shape=jax.ShapeDtypeStruct((m, n), F32),
        compiler_params=_cparams(("parallel", "arbitrary"), vmem_mib),
        name="res_matmul",
    )(*a_list, *([w] * n_pairs), x)


def _ffn(x, norm_g, wg, wu, wd, layer):
    h = _rmsnorm(x, norm_g, BF16, ROW_TM)
    act, wd_bf16 = _gate_up(h, wg, wu, wd, layer)
    return _res_matmul([act], wd_bf16, None, x, 0.5, tm=544, tn=512, vmem_mib=58)


WIN_TN = 256
WIN_MAIN_TILES = COL_LORA // WIN_TN
WIN_SHIFT_TILES = COL_S5 // WIN_TN


def _win_tile_cases(h_ref, wm_ref, wt_ref, j, emit):
    @pl.when(j < WIN_MAIN_TILES)
    def _():
        emit(jnp.dot(h_ref[...], wm_ref[...], preferred_element_type=F32), True)

    @pl.when((j >= WIN_MAIN_TILES) & (j < WIN_SHIFT_TILES))
    def _():
        emit(jnp.dot(h_ref[...], wt_ref[...], preferred_element_type=F32), True)

    @pl.when(j >= WIN_SHIFT_TILES)
    def _():
        emit(jnp.dot(h_ref[...], wt_ref[...], preferred_element_type=F32), False)


def _win_prompt_kernel(h_ref, wm_ref, wt_ref, mu_ref, o_ref):
    def emit(p, shifted):
        if shifted:
            rows = lax.broadcasted_iota(jnp.int32, p.shape, 0)
            prev = jnp.where(rows == 0, 0.0, pltpu.roll(p, 1, 0))
            o_ref[...] = p + (prev - p) * mu_ref[...]
        else:
            o_ref[...] = p

    _win_tile_cases(h_ref, wm_ref, wt_ref, pl.program_id(1), emit)


def _win_sample_kernel(h_ref, wm_ref, wt_ref, mu_ref, o_ref):
    def emit(p, shifted):
        cur = p[DEC_BATCH:]
        o_ref[...] = cur + (p[:M_SAMPLE] - cur) * mu_ref[...] if shifted else cur

    _win_tile_cases(h_ref, wm_ref, wt_ref, pl.program_id(0), emit)


def _win_weight_specs(layer, col_of):
    main = pl.BlockSpec((None, D_MODEL, WIN_TN),
                        lambda *g: (layer, 0, jnp.minimum(col_of(*g), WIN_MAIN_TILES - 1)))
    tail = pl.BlockSpec((None, D_MODEL, WIN_TN),
                        lambda *g: (layer, 0, jnp.maximum(col_of(*g) - WIN_MAIN_TILES, 0)))
    mu = pl.BlockSpec((None, 1, WIN_TN), lambda *g: (layer, 0, col_of(*g)))
    return [main, tail, mu]


def _win_prompt(h, w_main, w_tail, mu_pad, layer):
    tn = WIN_TN
    return pl.pallas_call(
        _win_prompt_kernel,
        grid=(BATCH, D_INP // tn),
        in_specs=[pl.BlockSpec((SEQ, D_MODEL), lambda b, j: (b, 0), pipeline_mode=pl.Buffered(1))]
                 + _win_weight_specs(layer, lambda b, j: j),
        out_specs=pl.BlockSpec((SEQ, tn), lambda b, j: (b, j)),
        out_shape=jax.ShapeDtypeStruct((M_PROMPT, D_INP), F32),
        compiler_params=_cparams(("parallel", "arbitrary"), 56),
        name="in_proj_prompt",
    )(h, w_main, w_tail, mu_pad)


def _win_sample(h_rows, w_main, w_tail, mu_pad, layer):
    tn = WIN_TN
    rows = h_rows.shape[0]
    return pl.pallas_call(
        _win_sample_kernel,
        grid=(D_INP // tn,),
        in_specs=[pl.BlockSpec((rows, D_MODEL), lambda j: (0, 0))] + _win_weight_specs(layer, lambda j: j),
        out_specs=pl.BlockSpec((M_SAMPLE, tn), lambda j: (0, j)),
        out_shape=jax.ShapeDtypeStruct((M_SAMPLE, D_INP), F32),
        compiler_params=_cparams(("arbitrary",), 32),
        name="in_proj_sample",
    )(h_rows, w_main, w_tail, mu_pad)


def _in_proj_tail(w_in, mu):
    o_w, o_a, o_g = COL_LORA, COL_LORA + W_LORA, COL_LORA + W_LORA + A_LORA
    o_s = o_g + G_LORA
    w_main = w_in.astype(BF16)
    zw = jnp.zeros((DEPTH, D_MODEL, LORA_PAD - W_LORA), BF16)
    w_tail = jnp.concatenate([w_main[:, :, o_w:o_a], zw, w_main[:, :, o_a:o_g], zw,
                              w_main[:, :, o_g:]], axis=2)
    zm = jnp.zeros((DEPTH, LORA_PAD - W_LORA), mu.dtype)
    mu_pad = jnp.concatenate([mu[:, :o_w], mu[:, o_w:o_a], zm, mu[:, o_a:o_g], zm, mu[:, o_g:o_s],
                              jnp.zeros((DEPTH, S5W), mu.dtype)], axis=1).reshape(DEPTH, 1, D_INP)
    return w_main, w_tail, mu_pad


def _split_bf16(x):
    hi = x.astype(BF16)
    return hi, (x - hi.astype(F32)).astype(BF16)


def _head_sum(x):
    r = lax.broadcasted_iota(jnp.int32, (LANES, LANES), 0) // HEAD
    c = lax.broadcasted_iota(jnp.int32, (LANES, LANES), 1) // HEAD
    ones = (r == c).astype(BF16)
    hi = x.astype(BF16)
    r1 = x - hi.astype(F32)
    mid = r1.astype(BF16)
    lo = (r1 - mid.astype(F32)).astype(BF16)
    outs = []
    for s in range(x.shape[1] // LANES):
        sl = slice(s * LANES, (s + 1) * LANES)
        acc = jnp.dot(hi[:, sl], ones, preferred_element_type=F32)
        acc = acc + jnp.dot(mid[:, sl], ones, preferred_element_type=F32)
        acc = acc + jnp.dot(lo[:, sl], ones, preferred_element_type=F32)
        outs.append(acc)
    return jnp.concatenate(outs, axis=1)


def _rwkv_pre_kernel(r_ref, k_ref, v_ref, lora_ref, w0_ref, a0_ref, kk_ref, ka_ref, rk_ref,
                     wup_ref, aup_ref, gup_ref,
                     r_out, w_out, k_out, v_out, kk_out, b_out, g_out, bonus_out):
    r = r_ref[...]
    k = k_ref[...]
    v = v_ref[...]
    r_out[...] = r
    v_out[...] = v
    lora = lora_ref[...]
    wd = lora[:, :LORA_PAD]
    ad = lora[:, LORA_PAD:2 * LORA_PAD]
    gd = lora[:, 2 * LORA_PAD:]
    z = -(w0_ref[...] + jnp.dot(jnp.tanh(wd).astype(BF16), wup_ref[...], preferred_element_type=F32))
    softplus = jnp.maximum(z, 0.0) + jnp.log1p(jnp.exp(-jnp.abs(z)))
    w_out[...] = jnp.exp(-jnp.exp(-softplus - 0.5))
    a = jax.nn.sigmoid(a0_ref[...] + jnp.dot(ad.astype(BF16), aup_ref[...], preferred_element_type=F32))
    g_out[...] = jnp.dot(jax.nn.sigmoid(gd).astype(BF16), gup_ref[...], preferred_element_type=F32)
    kk = k * kk_ref[...]
    k2 = k * (1.0 + (a - 1.0) * ka_ref[...])
    k_out[...] = k2
    kkn = kk / jnp.maximum(jnp.sqrt(_head_sum(kk * kk)), 1e-12)
    kk_out[...] = kkn
    b_out[...] = kkn * a
    bonus_out[...] = _head_sum(r * k2 * rk_ref[...]) * v


def _time_major_spec(m, tm, time_major):
    if not time_major:
        return pl.BlockSpec((tm, RW), lambda i: (i, 0)), (m, RW)
    per_seq = SEQ // tm
    return pl.BlockSpec((tm, RW), lambda i: (i % per_seq, i // per_seq)), (SEQ, (m // SEQ) * RW)


def _rwkv_pre(q, w0, a0, k_k, k_a, r_k, w_up, a_up, g_up, time_major, tm=ROW_TM):
    m = q.shape[0]
    out_spec, out_dims = _time_major_spec(m, tm, time_major)
    row = lambda v: v.reshape(1, RW)
    pad_rows = lambda u: jnp.concatenate(
        [u, jnp.zeros((LORA_PAD - u.shape[0], RW), u.dtype)], axis=0).astype(BF16)
    vec = pl.BlockSpec((1, RW), lambda i: (0, 0))
    full = lambda rows: pl.BlockSpec((rows, RW), lambda i: (0, 0))
    out = jax.ShapeDtypeStruct(out_dims, F32)
    return pl.pallas_call(
        _rwkv_pre_kernel,
        grid=(m // tm,),
        in_specs=[pl.BlockSpec((tm, RW), lambda i: (i, 0)),
                  pl.BlockSpec((tm, RW), lambda i: (i, 1)),
                  pl.BlockSpec((tm, RW), lambda i: (i, 2)),
                  pl.BlockSpec((tm, LORA_W), lambda i: (i, COL_LORA // LORA_W)),
                  vec, vec, vec, vec, vec, full(LORA_PAD), full(LORA_PAD), full(G_LORA)],
        out_specs=[out_spec] * 8,
        out_shape=[out] * 8,
        compiler_params=_cparams(("parallel",), 56),
        name="rwkv_pre",
    )(q, q, q, q, row(w0), row(a0), row(k_k), row(k_a), row(r_k.reshape(RW)),
      pad_rows(w_up), pad_rows(a_up), g_up.astype(BF16))


def _rwkv_post_kernel(y_ref, bonus_ref, g_ref, lw_ref, lb_ref, *rest):
    o_ref = rest[-1]
    y = y_ref[...]
    mean = _head_sum(y) * (1.0 / HEAD)
    c = y - mean
    var = _head_sum(c * c) * (1.0 / HEAD)
    yn = c * lax.rsqrt(var + EPS_GN) * lw_ref[...] + lb_ref[...]
    o_ref[...] = ((yn + bonus_ref[...]) * g_ref[...]).astype(o_ref.dtype)


def _rwkv_post(y, bonus, g, lnx_w, lnx_b, row0, time_major, joined=None, tm=ROW_TM):
    m = y.size // RW
    blk, _ = _time_major_spec(m, tm, time_major)
    vec = pl.BlockSpec((1, RW), lambda i: (0, 0))
    in_specs, args, aliases = _joined_out(
        [blk, blk, blk, vec, vec], [y, bonus, g, lnx_w.reshape(1, RW), lnx_b.reshape(1, RW)], joined)
    return pl.pallas_call(
        _rwkv_post_kernel,
        grid=(m // tm,),
        in_specs=in_specs,
        out_specs=pl.BlockSpec((tm, RW), lambda i: (i + row0 // tm, 0)),
        out_shape=jax.ShapeDtypeStruct((M_ALL, RW), BF16),
        input_output_aliases=aliases,
        compiler_params=_cparams(("parallel",), 40),
        name="rwkv_post",
    )(*args)


WKV_UNROLL = 8


def _wkv_kernel(r_ref, w_ref, k_ref, v_ref, kk_ref, b_ref, s0_ref, after_ref, y_ref, s_ref,
                g_ref, kq_ref, wr_ref, bt_ref, kt_ref, *, steps):
    @pl.when(pl.program_id(1) == 0)
    def _():
        s_ref[...] = s0_ref[...]

    g_ref[...] = jnp.ones((HEAD, LANES), F32)

    def step(t, carry):
        r, k, b = r_ref[t], k_ref[t], b_ref[t]
        g_prev = g_ref[...]
        g = g_prev * w_ref[t]
        g_inv = 1.0 / g
        g_ref[...] = g
        kq_ref[...] = g_prev * kk_ref[t]
        wr_ref[...] = g * r
        bt_ref[...] = b * g_inv
        kt_ref[...] = k * g_inv
        beta = jnp.sum(b * r, axis=0, keepdims=True)
        kappa = jnp.sum(k * r, axis=0, keepdims=True)
        v = v_ref[t]

        def contract(j, acc):
            sa, u = acc
            sj = s_ref[j]
            return sa - sj * kq_ref[pl.ds(j, 1), :], u + sj * wr_ref[pl.ds(j, 1), :]

        zero = jnp.zeros((HEAD, LANES), F32)
        sa, u = lax.fori_loop(0, HEAD, contract, (zero, zero), unroll=WKV_UNROLL)

        def update(j, c):
            s_ref[j] = s_ref[j] + sa * bt_ref[pl.ds(j, 1), :] + v * kt_ref[pl.ds(j, 1), :]
            return c

        lax.fori_loop(0, HEAD, update, 0, unroll=WKV_UNROLL)
        y_ref[t] = u + sa * beta + v * kappa
        return carry

    lax.fori_loop(0, steps, step, 0)

    def denormalise(j, c):
        s_ref[j] = s_ref[j] * g_ref[pl.ds(j, 1), :]
        return c

    lax.fori_loop(0, HEAD, denormalise, 0, unroll=WKV_UNROLL)


def _wkv_scan(r, w, k, v, kk, b, s0, after, tc):
    length, _, n = r.shape
    vec = pl.BlockSpec((tc, HEAD, LANES), lambda c, t: (t, 0, c))
    st = pl.BlockSpec((HEAD, HEAD, LANES), lambda c, t: (0, 0, c))
    return pl.pallas_call(
        functools.partial(_wkv_kernel, steps=tc),
        grid=(n // LANES, length // tc),
        in_specs=[vec] * 6 + [st, pl.BlockSpec(memory_space=pl.ANY)],
        out_specs=[vec, st],
        out_shape=[jax.ShapeDtypeStruct((length, HEAD, n), F32),
                   jax.ShapeDtypeStruct((HEAD, HEAD, n), F32)],
        scratch_shapes=[pltpu.VMEM((HEAD, LANES), F32)] * 5,
        compiler_params=_cparams(("parallel", "arbitrary"), 40),
        name="wkv_scan",
    )(r, w, k, v, kk, b, s0, after)


def _to_scan(a, length):
    return a.reshape(length, -1, HEAD).transpose(0, 2, 1)


def _from_scan(y, length):
    return y.transpose(0, 2, 1).reshape(length, -1)


def _state_to_scan(s):
    n = s.shape[0] * s.shape[1]
    s = lax.optimization_barrier(s.reshape(n, HEAD, HEAD).transpose(0, 2, 1))
    return s.reshape(n, HEAD * HEAD).T.reshape(HEAD, HEAD, n)


def _state_from_scan(s, batch):
    n = s.shape[2]
    s = lax.optimization_barrier(s.reshape(HEAD * HEAD, n).T)
    return s.reshape(n, HEAD, HEAD).transpose(0, 2, 1).reshape(batch, HEADS, HEAD, HEAD)


def _s5_discretize(lre_ref, lim_ref, ls_ref, bre_ref, bim_ref):
    lre, lim = lre_ref[...], lim_ref[...]
    step = jnp.exp(ls_ref[...])
    mag = jnp.exp(lre * step)
    ar = mag * jnp.cos(lim * step)
    ai = mag * jnp.sin(lim * step)
    den = lre * lre + lim * lim
    nr = ar - 1.0
    f_re = (nr * lre + ai * lim) / den
    f_im = (ai * lre - nr * lim) / den
    bre, bim = bre_ref[...], bim_ref[...]
    return ar, ai, f_re * bre - f_im * bim, f_re * bim + f_im * bre


def _dot3(a, b):
    a_hi, a_lo = _split_bf16(a)
    b_hi, b_lo = _split_bf16(b)
    acc = jnp.dot(a_hi, b_hi, preferred_element_type=F32)
    acc = acc + jnp.dot(a_lo, b_hi, preferred_element_type=F32)
    return acc + jnp.dot(a_hi, b_lo, preferred_element_type=F32)


def _s5_readout(hr, hi, cre_ref, cim_ref, d_ref, u):
    y = jnp.dot(hr.astype(BF16), cre_ref[...].astype(BF16), preferred_element_type=F32)
    y = y - jnp.dot(hi.astype(BF16), cim_ref[...].astype(BF16), preferred_element_type=F32)
    return y + d_ref[...] * u


def _cmul(ar, ai, br, bi):
    return ar * br - ai * bi, ar * bi + ai * br


def _s5_prompt_kernel(u_ref, lre_ref, lim_ref, ls_ref, bre_ref, bim_ref, cre_ref, cim_ref, d_ref,
                      y_ref, hre_ref, him_ref, hr_scr, hi_scr):
    ar, ai, bbr, bbi = _s5_discretize(lre_ref, lim_ref, ls_ref, bre_ref, bim_ref)
    u = u_ref[...]
    hr_scr[...] = _dot3(u, bbr)
    hi_scr[...] = _dot3(u, bbi)

    rows = lax.broadcasted_iota(jnp.int32, (SUBLANES, S5_HW), 0)
    powers = [(ar, ai)]
    for _ in range(SUBLANES - 1):
        powers.append(_cmul(*powers[-1], ar, ai))
    levels = []
    for sh in (1, 2, 4):
        pr, pi = powers[sh - 1]
        levels.append((sh, jnp.where(rows >= sh, pr, 0.0), jnp.where(rows >= sh, pi, 0.0)))
    cpr = jnp.zeros((SUBLANES, S5_HW), F32)
    cpi = jnp.zeros((SUBLANES, S5_HW), F32)
    for n, (pr, pi) in enumerate(powers):
        cpr = jnp.where(rows == n, pr, cpr)
        cpi = jnp.where(rows == n, pi, cpi)

    def tile(i, carry):
        cr, ci = carry
        r0 = pl.multiple_of(i * SUBLANES, SUBLANES)
        xr = hr_scr[pl.ds(r0, SUBLANES), :]
        xi = hi_scr[pl.ds(r0, SUBLANES), :]
        for sh, mr, mi in levels:
            sr, si = pltpu.roll(xr, sh, 0), pltpu.roll(xi, sh, 0)
            xr, xi = xr + (mr * sr - mi * si), xi + (mr * si + mi * sr)
        xr = xr + (cpr * cr - cpi * ci)
        xi = xi + (cpr * ci + cpi * cr)
        hr_scr[pl.ds(r0, SUBLANES), :] = xr
        hi_scr[pl.ds(r0, SUBLANES), :] = xi
        last = SUBLANES - 1
        return (jnp.broadcast_to(xr[last:, :], (SUBLANES, S5_HW)),
                jnp.broadcast_to(xi[last:, :], (SUBLANES, S5_HW)))

    zero = jnp.zeros((SUBLANES, S5_HW), F32)
    cr, ci = lax.fori_loop(0, SEQ // SUBLANES, tile, (zero, zero), unroll=2)
    hre_ref[...] = cr[:1]
    him_ref[...] = ci[:1]
    y_ref[...] = _s5_readout(hr_scr[...], hi_scr[...], cre_ref, cim_ref, d_ref, u)


def _s5_sample_kernel(u_ref, h0r_ref, h0i_ref, lre_ref, lim_ref, ls_ref, bre_ref, bim_ref, cre_ref, cim_ref,
                      d_ref, y_ref, hre_ref, him_ref, hr_scr, hi_scr):
    ar, ai, bbr, bbi = _s5_discretize(lre_ref, lim_ref, ls_ref, bre_ref, bim_ref)
    u = u_ref[...]
    bu_r = _dot3(u, bbr)
    bu_i = _dot3(u, bbi)
    hr, hi = h0r_ref[...], h0i_ref[...]
    for t in range(DEC_SEQ):
        rows = slice(t * DEC_BATCH, (t + 1) * DEC_BATCH)
        hr, hi = ar * hr - ai * hi + bu_r[rows], ar * hi + ai * hr + bu_i[rows]
        hr_scr[rows, :] = hr
        hi_scr[rows, :] = hi
    hre_ref[...] = hr
    him_ref[...] = hi
    y_ref[...] = _s5_readout(hr_scr[...], hi_scr[...], cre_ref, cim_ref, d_ref, u)


def _s5_params(lam_re, lam_im, log_step, b_re, b_im, c_re, c_im, d):
    eye = jnp.eye(S5_GT, dtype=F32)
    chan = lambda a: a.reshape(S5_TILES, 1, S5_HW)

    def b_blk(b):
        x = b.transpose(0, 2, 1).reshape(S5_TILES, S5_GT, S5_C, S5_P)
        return jnp.einsum('tgcp,gh->tgchp', x, eye).reshape(S5_TILES, S5_UW, S5_HW)

    def c_blk(c):
        x = c.transpose(0, 2, 1).reshape(S5_TILES, S5_GT, S5_P, S5_C)
        return jnp.einsum('tgpc,gh->tgphc', x, eye).reshape(S5_TILES, S5_HW, S5_UW)

    ls = jnp.broadcast_to(log_step[:, None], (S5_G, S5_P))
    return (chan(lam_re), chan(lam_im), chan(ls), b_blk(b_re), b_blk(b_im), c_blk(c_re), c_blk(c_im),
            d.reshape(S5_TILES, 1, S5_UW))


def _s5_param_specs(tile_of):
    chan = pl.BlockSpec((None, 1, S5_HW), lambda *g: (tile_of(*g), 0, 0))
    bspec = pl.BlockSpec((None, S5_UW, S5_HW), lambda *g: (tile_of(*g), 0, 0))
    cspec = pl.BlockSpec((None, S5_HW, S5_UW), lambda *g: (tile_of(*g), 0, 0))
    dspec = pl.BlockSpec((None, 1, S5_UW), lambda *g: (tile_of(*g), 0, 0))
    return [chan, chan, chan, bspec, bspec, cspec, cspec, dspec]


def _s5_prompt(q, params):
    col0 = COL_S5 // S5_UW
    state = jax.ShapeDtypeStruct((BATCH, 1, S5_G * S5_P), F32)
    st_spec = pl.BlockSpec((None, 1, S5_HW), lambda b, g: (b, 0, g))
    return pl.pallas_call(
        _s5_prompt_kernel,
        grid=(BATCH, S5_TILES),
        in_specs=[pl.BlockSpec((SEQ, S5_UW), lambda b, g: (b, col0 + g))] + _s5_param_specs(lambda b, g: g),
        out_specs=[pl.BlockSpec((SEQ, S5_UW), lambda b, g: (b, g)), st_spec, st_spec],
        out_shape=[jax.ShapeDtypeStruct((M_PROMPT, S5W), F32), state, state],
        scratch_shapes=[pltpu.VMEM((SEQ, S5_HW), F32), pltpu.VMEM((SEQ, S5_HW), F32)],
        compiler_params=_cparams(("parallel", "parallel"), 48),
        name="s5_prompt",
    )(q, *params)


def _s5_sample(q, h0_re, h0_im, params):
    col0 = COL_S5 // S5_UW
    state = jax.ShapeDtypeStruct((DEC_BATCH, S5_G * S5_P), F32)
    st_spec = pl.BlockSpec((DEC_BATCH, S5_HW), lambda g: (0, g))
    return pl.pallas_call(
        _s5_sample_kernel,
        grid=(S5_TILES,),
        in_specs=[pl.BlockSpec((M_SAMPLE, S5_UW), lambda g: (0, col0 + g)), st_spec, st_spec]
                 + _s5_param_specs(lambda g: g),
        out_specs=[pl.BlockSpec((M_SAMPLE, S5_UW), lambda g: (0, g)), st_spec, st_spec],
        out_shape=[jax.ShapeDtypeStruct((M_SAMPLE, S5W), F32), state, state],
        scratch_shapes=[pltpu.VMEM((M_SAMPLE, S5_HW), F32), pltpu.VMEM((M_SAMPLE, S5_HW), F32)],
        compiler_params=_cparams(("parallel",), 32),
        name="s5_sample",
    )(q, h0_re, h0_im, *params)


def _glu_norm_kernel(y_ref, w_ref, b_ref, g_ref, *rest):
    o_ref = rest[-1]
    y = y_ref[...]
    z = 0.5 * y * (1.0 + lax.erf(y * math.sqrt(0.5)))
    gate = jax.nn.sigmoid(jnp.dot(z.astype(BF16), w_ref[...], preferred_element_type=F32) + b_ref[...])
    out = z * gate
    ms = jnp.mean(out * out, axis=-1, keepdims=True)
    o_ref[...] = (out * lax.rsqrt(ms + EPS_RMS) * g_ref[...]).astype(o_ref.dtype)


def _glu_norm(y, w_glu_bf16, layer, b_glu, gain, row0, joined=None, tm=ROW_TM):
    m = y.shape[0]
    vec = pl.BlockSpec((1, S5W), lambda i: (0, 0))
    in_specs, args, aliases = _joined_out(
        [pl.BlockSpec((tm, S5W), lambda i: (i, 0)),
         pl.BlockSpec((None, S5W, S5W), lambda i: (layer, 0, 0)), vec, vec],
        [y, w_glu_bf16, b_glu.reshape(1, S5W), gain.reshape(1, S5W)], joined)
    return pl.pallas_call(
        _glu_norm_kernel,
        grid=(m // tm,),
        in_specs=in_specs,
        out_specs=pl.BlockSpec((tm, S5W), lambda i: (i + row0 // tm, 0)),
        out_shape=jax.ShapeDtypeStruct((M_ALL, S5W), BF16),
        input_output_aliases=aliases,
        compiler_params=_cparams(("parallel",), 48),
        name="s5_glu_norm",
    )(*args)


def kernel(x_prompt, x_sample, state_shift, state_wkv, state_ssm_re, state_ssm_im, ffn1_norm, ffn1_w_gate, ffn1_w_up, ffn1_w_down, mix_norm, w_in, shift_mu, rw_w0, rw_w_up, rw_a0, rw_a_up, rw_g_up, rw_k_k, rw_k_a, rw_r_k, rw_lnx_w, rw_lnx_b, s5_lam_re, s5_lam_im, s5_b_re, s5_b_im, s5_c_re, s5_c_im, s5_d, s5_log_step, s5_w_glu, s5_b_glu, s5_out_norm, w_out, ffn2_norm, ffn2_w_gate, ffn2_w_up, ffn2_w_down, final_norm):
    x = jnp.concatenate([x_prompt.reshape(M_PROMPT, D_MODEL),
                         x_sample.transpose(1, 0, 2).reshape(M_SAMPLE, D_MODEL)], axis=0)
    zero_state = jnp.zeros((HEAD, HEAD, BATCH * HEADS), F32)
    shift_p, wkv_p, re_p, im_p, shift_s, wkv_s, re_s, im_s = ([] for _ in range(8))
    wo, w_glu = w_out.astype(BF16), s5_w_glu.astype(BF16)
    w_main, w_tail, mu_pad = _in_proj_tail(w_in, shift_mu)

    for l in range(DEPTH):
        x = _ffn(x, ffn1_norm[l], ffn1_w_gate, ffn1_w_up, ffn1_w_down, l)

        h_mix = _rmsnorm(x, mix_norm[l], BF16, ROW_TM)
        last8 = _rmsnorm(x, mix_norm[l], F32, SUBLANES, SEQ - SUBLANES, BATCH * SUBLANES, SEQ // SUBLANES)
        shift_p.append(last8[SUBLANES - 1::SUBLANES])
        shift_s.append(_rmsnorm(x, mix_norm[l], F32, DEC_BATCH, M_ALL - DEC_BATCH, DEC_BATCH))

        q_p = _win_prompt(h_mix, w_main, w_tail, mu_pad, l)
        q_s = _win_sample(jnp.concatenate([state_shift[l].astype(BF16), h_mix[M_PROMPT:]], axis=0),
                          w_main, w_tail, mu_pad, l)

        rw_args = (rw_w0[l], rw_a0[l], rw_k_k[l], rw_k_a[l], rw_r_k[l], rw_w_up[l], rw_a_up[l], rw_g_up[l])
        *scan_p, gate_p, bonus_p = _rwkv_pre(q_p, *rw_args, time_major=True)
        *scan_s, gate_s, bonus_s = _rwkv_pre(q_s, *rw_args, time_major=False)
        scan_p = [_to_scan(a, SEQ) for a in scan_p]
        scan_s = [_to_scan(a, DEC_SEQ) for a in scan_s]

        s5p = _s5_params(s5_lam_re[l], s5_lam_im[l], s5_log_step[l], s5_b_re[l], s5_b_im[l],
                         s5_c_re[l], s5_c_im[l], s5_d[l])
        y5_p, hre_p, him_p = _s5_prompt(q_p, s5p)
        y5_s, hre_s, him_s = _s5_sample(q_s, state_ssm_re[l].reshape(DEC_BATCH, S5_G * S5_P),
                                        state_ssm_im[l].reshape(DEC_BATCH, S5_G * S5_P), s5p)
        re_p.append(hre_p.reshape(BATCH, S5_G, S5_P))
        im_p.append(him_p.reshape(BATCH, S5_G, S5_P))
        re_s.append(hre_s.reshape(DEC_BATCH, S5_G, S5_P))
        im_s.append(him_s.reshape(DEC_BATCH, S5_G, S5_P))
        y_s5 = _glu_norm(y5_p, w_glu, l, s5_b_glu[l], s5_out_norm[l], 0)
        y_s5 = _glu_norm(y5_s, w_glu, l, s5_b_glu[l], s5_out_norm[l], M_PROMPT, y_s5)

        y_p, s_p = _wkv_scan(*scan_p, zero_state, y_s5, tc=32)
        y_s, s_s = _wkv_scan(*scan_s, _state_to_scan(state_wkv[l]), y_s5, tc=DEC_SEQ)
        wkv_p.append(_state_from_scan(s_p, BATCH))
        wkv_s.append(_state_from_scan(s_s, DEC_BATCH))
        y_rw = _rwkv_post(_from_scan(y_p, SEQ), bonus_p, gate_p, rw_lnx_w[l], rw_lnx_b[l], 0, True)
        y_rw = _rwkv_post(_from_scan(y_s, DEC_SEQ).reshape(M_SAMPLE, RW), bonus_s, gate_s,
                          rw_lnx_w[l], rw_lnx_b[l], M_PROMPT, False, y_rw)

        x = _res_matmul([y_rw, y_s5], wo, l, x, 1.0, tm=1088, tn=512, vmem_mib=48)
        x = _ffn(x, ffn2_norm[l], ffn2_w_gate, ffn2_w_up, ffn2_w_down, l)

    y_prompt = _rmsnorm(x, final_norm, F32, ROW_TM, 0, M_PROMPT).reshape(BATCH, SEQ, D_MODEL)
    y_sample = _rmsnorm(x, final_norm, F32, ROW_TM, M_PROMPT, M_SAMPLE).reshape(
        DEC_SEQ, DEC_BATCH, D_MODEL).transpose(1, 0, 2)
    st = jnp.stack
    return (y_prompt, y_sample, st(shift_p), st(wkv_p), st(re_p), st(im_p),
            st(shift_s), st(wkv_s), st(re_s), st(im_s))
```

```python
import functools
import math

import jax
import jax.numpy as jnp
from jax import lax
from jax.experimental import pallas as pl
from jax.experimental.pallas import tpu as pltpu

F32 = jnp.float32
BF16 = jnp.bfloat16

D_MODEL = 4096
BATCH = 4
SEQ = 2048
DEPTH = 2
DEC_BATCH = 128
DEC_SEQ = 4
M_PROMPT = BATCH * SEQ
M_SAMPLE = DEC_BATCH * DEC_SEQ
M_ALL = M_PROMPT + M_SAMPLE

RW = D_MODEL // 2
HEAD = 64
HEADS = RW // HEAD
S5W = D_MODEL - RW
S5_C = 16
S5_G = S5W // S5_C
S5_P = 64
W_LORA = 96
A_LORA = 96
G_LORA = 256
LORA_PAD = 128
LORA_W = 2 * LORA_PAD + G_LORA
D_FF = 11008
EPS_RMS = 1e-6
EPS_GN = 64e-5

COL_LORA = 3 * RW
COL_S5 = COL_LORA + LORA_W
D_INP = COL_S5 + S5W

V7X_VMEM_BYTES = 64 * 1024 * 1024
LANES = 128
SUBLANES = 8

S5_GT = 8
S5_TILES = S5_G // S5_GT
S5_UW = S5_GT * S5_C
S5_HW = S5_GT * S5_P

ROW_TM = 256


def _cparams(semantics, vmem_mib):
    assert vmem_mib * 1024 * 1024 < V7X_VMEM_BYTES
    return pltpu.CompilerParams(dimension_semantics=semantics,
                                vmem_limit_bytes=vmem_mib * 1024 * 1024)


def _joined_out(in_specs, args, joined):
    if joined is None:
        return in_specs, args, {}
    return (in_specs + [pl.BlockSpec(memory_space=pl.ANY)], args + [joined], {len(args): 0})


def _rmsnorm_kernel(x_ref, g_ref, o_ref):
    x = x_ref[...]
    ms = jnp.mean(x * x, axis=-1, keepdims=True)
    o_ref[...] = (x * lax.rsqrt(ms + EPS_RMS) * g_ref[...]).astype(o_ref.dtype)


def _rmsnorm(x, g, out_dtype, tm, row0=0, rows=None, block_stride=1):
    d = x.shape[1]
    m = x.shape[0] if rows is None else rows
    return pl.pallas_call(
        _rmsnorm_kernel,
        grid=(m // tm,),
        in_specs=[pl.BlockSpec((tm, d), lambda i: (i * block_stride + row0 // tm, 0)),
                  pl.BlockSpec((1, d), lambda i: (0, 0))],
        out_specs=pl.BlockSpec((tm, d), lambda i: (i, 0)),
        out_shape=jax.ShapeDtypeStruct((m, d), out_dtype),
        compiler_params=_cparams(("parallel",), 40),
        name="rmsnorm",
    )(x, g.reshape(1, d))


WD_CAST_ROWS = 128


def _gate_up_kernel(h_ref, wg_ref, wu_ref, wd_ref, o_ref, wd_bf16_ref, *, cast_passes):
    h = h_ref[...]
    a = jnp.dot(h, wg_ref[...].astype(BF16), preferred_element_type=F32)
    b = jnp.dot(h, wu_ref[...].astype(BF16), preferred_element_type=F32)
    o_ref[...] = (a * jax.nn.sigmoid(a) * b).astype(o_ref.dtype)

    @pl.when(pl.program_id(0) < cast_passes)
    def _():
        wd_bf16_ref[...] = wd_ref[...].astype(BF16)


def _gate_up(h, wg, wu, wd, layer, tm=2176, tn=256):
    m, d = h.shape
    f = wg.shape[2]
    n_i, n_j = m // tm, f // tn
    cast_blocks = f // WD_CAST_ROWS
    cast_passes = cast_blocks // n_j
    assert cast_passes * n_j == cast_blocks and cast_passes <= n_i

    def cast_block(i, j):
        return jnp.where(i < cast_passes, i * n_j + j, cast_blocks - 1)

    w_spec = pl.BlockSpec((None, d, tn), lambda i, j: (layer, 0, j))
    return pl.pallas_call(
        functools.partial(_gate_up_kernel, cast_passes=cast_passes),
        grid=(n_i, n_j),
        in_specs=[pl.BlockSpec((tm, d), lambda i, j: (i, 0), pipeline_mode=pl.Buffered(1)), w_spec, w_spec,
                  pl.BlockSpec((None, WD_CAST_ROWS, d), lambda i, j: (layer, cast_block(i, j), 0))],
        out_specs=[pl.BlockSpec((tm, tn), lambda i, j: (i, j)),
                   pl.BlockSpec((WD_CAST_ROWS, d), lambda i, j: (cast_block(i, j), 0))],
        out_shape=[jax.ShapeDtypeStruct((m, f), BF16), jax.ShapeDtypeStruct((f, d), BF16)],
        compiler_params=_cparams(("arbitrary", "arbitrary"), 58),
        name="ffn_gate_up",
    )(h, wg, wu, wd)


def _res_matmul_kernel(*refs, n_pairs, scale):
    a_refs = refs[:n_pairs]
    w_refs = refs[n_pairs:2 * n_pairs]
    x_ref, o_ref = refs[2 * n_pairs], refs[2 * n_pairs + 1]
    acc = jnp.dot(a_refs[0][...], w_refs[0][...], preferred_element_type=F32)
    for a_ref, w_ref in zip(a_refs[1:], w_refs[1:]):
        acc = acc + jnp.dot(a_ref[...], w_ref[...], preferred_element_type=F32)
    if scale != 1.0:
        acc = scale * acc
    o_ref[...] = x_ref[...] + acc


def _res_matmul(a_list, w, layer, x, scale, tm, tn, vmem_mib):
    m, n = x.shape
    n_pairs = len(a_list)
    in_specs = [pl.BlockSpec((tm, a.shape[1]), lambda i, j: (i, 0)) for a in a_list]
    for p, a in enumerate(a_list):
        if layer is None:
            in_specs.append(pl.BlockSpec((a.shape[1], tn), lambda i, j, p=p: (p, j)))
        else:
            in_specs.append(pl.BlockSpec((None, a.shape[1], tn), lambda i, j, p=p: (layer, p, j)))
    in_specs.append(pl.BlockSpec((tm, tn), lambda i, j: (i, j)))
    assert sum(a.shape[1] for a in a_list) == w.shape[-2]
    return pl.pallas_call(
        functools.partial(_res_matmul_kernel, n_pairs=n_pairs, scale=scale),
        grid=(m // tm, n // tn),
        in_specs=in_specs,
        out_specs=pl.BlockSpec((tm, tn), lambda i, j: (i, j)),
        out_shape=jax.ShapeDtypeStruct((m, n), F32),
        compiler_params=_cparams(("parallel", "arbitrary"), vmem_mib),
        name="res_matmul",
    )(*a_list, *([w] * n_pairs), x)


def _ffn(x, norm_g, wg, wu, wd, layer):
    h = _rmsnorm(x, norm_g, BF16, ROW_TM)
    act, wd_bf16 = _gate_up(h, wg, wu, wd, layer)
    return _res_matmul([act], wd_bf16, None, x, 0.5, tm=544, tn=512, vmem_mib=58)


WIN_TN = 256
WIN_MAIN_TILES = COL_LORA // WIN_TN
WIN_SHIFT_TILES = COL_S5 // WIN_TN


def _win_tile_cases(h_ref, wm_ref, wt_ref, j, emit):
    @pl.when(j < WIN_MAIN_TILES)
    def _():
        emit(jnp.dot(h_ref[...], wm_ref[...], preferred_element_type=F32), True)

    @pl.when((j >= WIN_MAIN_TILES) & (j < WIN_SHIFT_TILES))
    def _():
        emit(jnp.dot(h_ref[...], wt_ref[...], preferred_element_type=F32), True)

    @pl.when(j >= WIN_SHIFT_TILES)
    def _():
        emit(jnp.dot(h_ref[...], wt_ref[...], preferred_element_type=F32), False)


def _win_prompt_kernel(h_ref, wm_ref, wt_ref, mu_ref, o_ref):
    def emit(p, shifted):
        if shifted:
            rows = lax.broadcasted_iota(jnp.int32, p.shape, 0)
            prev = jnp.where(rows == 0, 0.0, pltpu.roll(p, 1, 0))
            o_ref[...] = p + (prev - p) * mu_ref[...]
        else:
            o_ref[...] = p

    _win_tile_cases(h_ref, wm_ref, wt_ref, pl.program_id(1), emit)


def _win_sample_kernel(h_ref, wm_ref, wt_ref, mu_ref, o_ref):
    def emit(p, shifted):
        cur = p[DEC_BATCH:]
        o_ref[...] = cur + (p[:M_SAMPLE] - cur) * mu_ref[...] if shifted else cur

    _win_tile_cases(h_ref, wm_ref, wt_ref, pl.program_id(0), emit)


def _win_weight_specs(layer, col_of):
    main = pl.BlockSpec((None, D_MODEL, WIN_TN),
                        lambda *g: (layer, 0, jnp.minimum(col_of(*g), WIN_MAIN_TILES - 1)))
    tail = pl.BlockSpec((None, D_MODEL, WIN_TN),
                        lambda *g: (layer, 0, jnp.maximum(col_of(*g) - WIN_MAIN_TILES, 0)))
    mu = pl.BlockSpec((None, 1, WIN_TN), lambda *g: (layer, 0, col_of(*g)))
    return [main, tail, mu]


def _win_prompt(h, w_main, w_tail, mu_pad, layer):
    tn = WIN_TN
    return pl.pallas_call(
        _win_prompt_kernel,
        grid=(BATCH, D_INP // tn),
        in_specs=[pl.BlockSpec((SEQ, D_MODEL), lambda b, j: (b, 0), pipeline_mode=pl.Buffered(1))]
                 + _win_weight_specs(layer, lambda b, j: j),
        out_specs=pl.BlockSpec((SEQ, tn), lambda b, j: (b, j)),
        out_shape=jax.ShapeDtypeStruct((M_PROMPT, D_INP), F32),
        compiler_params=_cparams(("parallel", "arbitrary"), 56),
        name="in_proj_prompt",
    )(h, w_main, w_tail, mu_pad)


def _win_sample(h_rows, w_main, w_tail, mu_pad, layer):
    tn = WIN_TN
    rows = h_rows.shape[0]
    return pl.pallas_call(
        _win_sample_kernel,
        grid=(D_INP // tn,),
        in_specs=[pl.BlockSpec((rows, D_MODEL), lambda j: (0, 0))] + _win_weight_specs(layer, lambda j: j),
        out_specs=pl.BlockSpec((M_SAMPLE, tn), lambda j: (0, j)),
        out_shape=jax.ShapeDtypeStruct((M_SAMPLE, D_INP), F32),
        compiler_params=_cparams(("arbitrary",), 32),
        name="in_proj_sample",
    )(h_rows, w_main, w_tail, mu_pad)


def _in_proj_tail(w_in, mu):
    o_w, o_a, o_g = COL_LORA, COL_LORA + W_LORA, COL_LORA + W_LORA + A_LORA
    o_s = o_g + G_LORA
    w_main = w_in.astype(BF16)
    zw = jnp.zeros((DEPTH, D_MODEL, LORA_PAD - W_LORA), BF16)
    w_tail = jnp.concatenate([w_main[:, :, o_w:o_a], zw, w_main[:, :, o_a:o_g], zw,
                              w_main[:, :, o_g:]], axis=2)
    zm = jnp.zeros((DEPTH, LORA_PAD - W_LORA), mu.dtype)
    mu_pad = jnp.concatenate([mu[:, :o_w], mu[:, o_w:o_a], zm, mu[:, o_a:o_g], zm, mu[:, o_g:o_s],
                              jnp.zeros((DEPTH, S5W), mu.dtype)], axis=1).reshape(DEPTH, 1, D_INP)
    return w_main, w_tail, mu_pad


def _split_bf16(x):
    hi = x.astype(BF16)
    return hi, (x - hi.astype(F32)).astype(BF16)


def _head_sum(x):
    r = lax.broadcasted_iota(jnp.int32, (LANES, LANES), 0) // HEAD
    c = lax.broadcasted_iota(jnp.int32, (LANES, LANES), 1) // HEAD
    ones = (r == c).astype(BF16)
    hi = x.astype(BF16)
    r1 = x - hi.astype(F32)
    mid = r1.astype(BF16)
    lo = (r1 - mid.astype(F32)).astype(BF16)
    outs = []
    for s in range(x.shape[1] // LANES):
        sl = slice(s * LANES, (s + 1) * LANES)
        acc = jnp.dot(hi[:, sl], ones, preferred_element_type=F32)
        acc = acc + jnp.dot(mid[:, sl], ones, preferred_element_type=F32)
        acc = acc + jnp.dot(lo[:, sl], ones, preferred_element_type=F32)
        outs.append(acc)
    return jnp.concatenate(outs, axis=1)


def _rwkv_pre_kernel(r_ref, k_ref, v_ref, lora_ref, w0_ref, a0_ref, kk_ref, ka_ref, rk_ref,
                     wup_ref, aup_ref, gup_ref,
                     r_out, w_out, k_out, v_out, kk_out, b_out, g_out, bonus_out, *, channel_major):
    def emit(ref, val):
        ref[...] = val.T if channel_major else val

    r = r_ref[...]
    k = k_ref[...]
    v = v_ref[...]
    emit(r_out, r)
    emit(v_out, v)
    lora = lora_ref[...]
    wd = lora[:, :LORA_PAD]
    ad = lora[:, LORA_PAD:2 * LORA_PAD]
    gd = lora[:, 2 * LORA_PAD:]
    z = -(w0_ref[...] + jnp.dot(jnp.tanh(wd).astype(BF16), wup_ref[...], preferred_element_type=F32))
    softplus = jnp.maximum(z, 0.0) + jnp.log1p(jnp.exp(-jnp.abs(z)))
    emit(w_out, jnp.exp(-jnp.exp(-softplus - 0.5)))
    a = jax.nn.sigmoid(a0_ref[...] + jnp.dot(ad.astype(BF16), aup_ref[...], preferred_element_type=F32))
    g_out[...] = jnp.dot(jax.nn.sigmoid(gd).astype(BF16), gup_ref[...], preferred_element_type=F32)
    kk = k * kk_ref[...]
    k2 = k * (1.0 + (a - 1.0) * ka_ref[...])
    emit(k_out, k2)
    kkn = kk / jnp.maximum(jnp.sqrt(_head_sum(kk * kk)), 1e-12)
    emit(kk_out, kkn)
    emit(b_out, kkn * a)
    bonus_out[...] = _head_sum(r * k2 * rk_ref[...]) * v


def _time_major_spec(m, tm, time_major):
    if not time_major:
        return pl.BlockSpec((tm, RW), lambda i: (i, 0)), (m, RW)
    per_seq = SEQ // tm
    return pl.BlockSpec((tm, RW), lambda i: (i % per_seq, i // per_seq)), (SEQ, (m // SEQ) * RW)


def _rwkv_pre(q, w0, a0, k_k, k_a, r_k, w_up, a_up, g_up, time_major, tm=ROW_TM):
    m = q.shape[0]
    out_spec, out_dims = _time_major_spec(m, tm, time_major)
    scan_spec, scan_dims = out_spec, out_dims
    if time_major:
        per_seq = SEQ // tm
        scan_spec = pl.BlockSpec((RW, tm), lambda i: (i // per_seq, i % per_seq))
        scan_dims = ((m // SEQ) * RW, SEQ)
    row = lambda v: v.reshape(1, RW)
    pad_rows = lambda u: jnp.concatenate(
        [u, jnp.zeros((LORA_PAD - u.shape[0], RW), u.dtype)], axis=0).astype(BF16)
    vec = pl.BlockSpec((1, RW), lambda i: (0, 0))
    full = lambda rows: pl.BlockSpec((rows, RW), lambda i: (0, 0))
    out = jax.ShapeDtypeStruct(out_dims, F32)
    scan_out = jax.ShapeDtypeStruct(scan_dims, F32)
    return pl.pallas_call(
        functools.partial(_rwkv_pre_kernel, channel_major=time_major),
        grid=(m // tm,),
        in_specs=[pl.BlockSpec((tm, RW), lambda i: (i, 0)),
                  pl.BlockSpec((tm, RW), lambda i: (i, 1)),
                  pl.BlockSpec((tm, RW), lambda i: (i, 2)),
                  pl.BlockSpec((tm, LORA_W), lambda i: (i, COL_LORA // LORA_W)),
                  vec, vec, vec, vec, vec, full(LORA_PAD), full(LORA_PAD), full(G_LORA)],
        out_specs=[scan_spec] * 6 + [out_spec] * 2,
        out_shape=[scan_out] * 6 + [out] * 2,
        compiler_params=_cparams(("parallel",), 56),
        name="rwkv_pre",
    )(q, q, q, q, row(w0), row(a0), row(k_k), row(k_a), row(r_k.reshape(RW)),
      pad_rows(w_up), pad_rows(a_up), g_up.astype(BF16))


def _rwkv_post_kernel(y_ref, bonus_ref, g_ref, lw_ref, lb_ref, *rest):
    o_ref = rest[-1]
    y = y_ref[...]
    mean = _head_sum(y) * (1.0 / HEAD)
    c = y - mean
    var = _head_sum(c * c) * (1.0 / HEAD)
    yn = c * lax.rsqrt(var + EPS_GN) * lw_ref[...] + lb_ref[...]
    o_ref[...] = ((yn + bonus_ref[...]) * g_ref[...]).astype(o_ref.dtype)


def _rwkv_post(y, bonus, g, lnx_w, lnx_b, row0, time_major, joined=None, tm=ROW_TM):
    m = y.size // RW
    blk, _ = _time_major_spec(m, tm, time_major)
    vec = pl.BlockSpec((1, RW), lambda i: (0, 0))
    in_specs, args, aliases = _joined_out(
        [blk, blk, blk, vec, vec], [y, bonus, g, lnx_w.reshape(1, RW), lnx_b.reshape(1, RW)], joined)
    return pl.pallas_call(
        _rwkv_post_kernel,
        grid=(m // tm,),
        in_specs=in_specs,
        out_specs=pl.BlockSpec((tm, RW), lambda i: (i + row0 // tm, 0)),
        out_shape=jax.ShapeDtypeStruct((M_ALL, RW), BF16),
        input_output_aliases=aliases,
        compiler_params=_cparams(("parallel",), 40),
        name="rwkv_post",
    )(*args)


WKV_UNROLL = 8


def _wkv_kernel(r_ref, w_ref, k_ref, v_ref, kk_ref, b_ref, s0_ref, after_ref, y_ref, s_ref,
                g_ref, kq_ref, wr_ref, bt_ref, kt_ref, *, steps):
    @pl.when(pl.program_id(1) == 0)
    def _():
        s_ref[...] = s0_ref[...]

    g_ref[...] = jnp.ones((HEAD, LANES), F32)

    def step(t, carry):
        r, k, b = r_ref[t], k_ref[t], b_ref[t]
        g_prev = g_ref[...]
        g = g_prev * w_ref[t]
        g_inv = 1.0 / g
        g_ref[...] = g
        kq_ref[...] = g_prev * kk_ref[t]
        wr_ref[...] = g * r
        bt_ref[...] = b * g_inv
        kt_ref[...] = k * g_inv
        beta = jnp.sum(b * r, axis=0, keepdims=True)
        kappa = jnp.sum(k * r, axis=0, keepdims=True)
        v = v_ref[t]

        def contract(j, acc):
            sa, u = acc
            sj = s_ref[j]
            return sa - sj * kq_ref[pl.ds(j, 1), :], u + sj * wr_ref[pl.ds(j, 1), :]

        zero = jnp.zeros((HEAD, LANES), F32)
        sa, u = lax.fori_loop(0, HEAD, contract, (zero, zero), unroll=WKV_UNROLL)

        def update(j, c):
            s_ref[j] = s_ref[j] + sa * bt_ref[pl.ds(j, 1), :] + v * kt_ref[pl.ds(j, 1), :]
            return c

        lax.fori_loop(0, HEAD, update, 0, unroll=WKV_UNROLL)
        y_ref[t] = u + sa * beta + v * kappa
        return carry

    lax.fori_loop(0, steps, step, 0)

    def denormalise(j, c):
        s_ref[j] = s_ref[j] * g_ref[pl.ds(j, 1), :]
        return c

    lax.fori_loop(0, HEAD, denormalise, 0, unroll=WKV_UNROLL)


def _wkv_scan(r, w, k, v, kk, b, s0, after, tc):
    length, _, n = r.shape
    vec = pl.BlockSpec((tc, HEAD, LANES), lambda c, t: (t, 0, c))
    st = pl.BlockSpec((HEAD, HEAD, LANES), lambda c, t: (0, 0, c))
    return pl.pallas_call(
        functools.partial(_wkv_kernel, steps=tc),
        grid=(n // LANES, length // tc),
        in_specs=[vec] * 6 + [st, pl.BlockSpec(memory_space=pl.ANY)],
        out_specs=[vec, st],
        out_shape=[jax.ShapeDtypeStruct((length, HEAD, n), F32),
                   jax.ShapeDtypeStruct((HEAD, HEAD, n), F32)],
        scratch_shapes=[pltpu.VMEM((HEAD, LANES), F32)] * 5,
        compiler_params=_cparams(("parallel", "arbitrary"), 40),
        name="wkv_scan",
    )(r, w, k, v, kk, b, s0, after)


def _to_scan(a, length):
    return a.reshape(length, -1, HEAD).transpose(0, 2, 1)


def _to_scan_channel_major(a):
    return a.reshape(-1, HEAD, a.shape[1]).transpose(2, 1, 0)


def _from_scan(y, length):
    return y.transpose(0, 2, 1).reshape(length, -1)


def _state_to_scan(s):
    n = s.shape[0] * s.shape[1]
    s = lax.optimization_barrier(s.reshape(n, HEAD, HEAD).transpose(0, 2, 1))
    return s.reshape(n, HEAD * HEAD).T.reshape(HEAD, HEAD, n)


def _state_from_scan(s, batch):
    n = s.shape[2]
    s = lax.optimization_barrier(s.reshape(HEAD * HEAD, n).T)
    return s.reshape(n, HEAD, HEAD).transpose(0, 2, 1).reshape(batch, HEADS, HEAD, HEAD)


def _s5_discretize(lre_ref, lim_ref, ls_ref, bre_ref, bim_ref):
    lre, lim = lre_ref[...], lim_ref[...]
    step = jnp.exp(ls_ref[...])
    mag = jnp.exp(lre * step)
    ar = mag * jnp.cos(lim * step)
    ai = mag * jnp.sin(lim * step)
    den = lre * lre + lim * lim
    nr = ar - 1.0
    f_re = (nr * lre + ai * lim) / den
    f_im = (ai * lre - nr * lim) / den
    bre, bim = bre_ref[...], bim_ref[...]
    return ar, ai, f_re * bre - f_im * bim, f_re * bim + f_im * bre


def _dot3(a, b):
    a_hi, a_lo = _split_bf16(a)
    b_hi, b_lo = _split_bf16(b)
    acc = jnp.dot(a_hi, b_hi, preferred_element_type=F32)
    acc = acc + jnp.dot(a_lo, b_hi, preferred_element_type=F32)
    return acc + jnp.dot(a_hi, b_lo, preferred_element_type=F32)


def _s5_readout(hr, hi, cre_ref, cim_ref, d_ref, u):
    y = jnp.dot(hr.astype(BF16), cre_ref[...].astype(BF16), preferred_element_type=F32)
    y = y - jnp.dot(hi.astype(BF16), cim_ref[...].astype(BF16), preferred_element_type=F32)
    return y + d_ref[...] * u


def _cmul(ar, ai, br, bi):
    return ar * br - ai * bi, ar * bi + ai * br


def _s5_prompt_kernel(u_ref, lre_ref, lim_ref, ls_ref, bre_ref, bim_ref, cre_ref, cim_ref, d_ref,
                      y_ref, hre_ref, him_ref, hr_scr, hi_scr):
    ar, ai, bbr, bbi = _s5_discretize(lre_ref, lim_ref, ls_ref, bre_ref, bim_ref)
    u = u_ref[...]
    hr_scr[...] = _dot3(u, bbr)
    hi_scr[...] = _dot3(u, bbi)

    rows = lax.broadcasted_iota(jnp.int32, (SUBLANES, S5_HW), 0)
    powers = [(ar, ai)]
    for _ in range(SUBLANES - 1):
        powers.append(_cmul(*powers[-1], ar, ai))
    levels = []
    for sh in (1, 2, 4):
        pr, pi = powers[sh - 1]
        levels.append((sh, jnp.where(rows >= sh, pr, 0.0), jnp.where(rows >= sh, pi, 0.0)))
    cpr = jnp.zeros((SUBLANES, S5_HW), F32)
    cpi = jnp.zeros((SUBLANES, S5_HW), F32)
    for n, (pr, pi) in enumerate(powers):
        cpr = jnp.where(rows == n, pr, cpr)
        cpi = jnp.where(rows == n, pi, cpi)

    def tile(i, carry):
        cr, ci = carry
        r0 = pl.multiple_of(i * SUBLANES, SUBLANES)
        xr = hr_scr[pl.ds(r0, SUBLANES), :]
        xi = hi_scr[pl.ds(r0, SUBLANES), :]
        for sh, mr, mi in levels:
            sr, si = pltpu.roll(xr, sh, 0), pltpu.roll(xi, sh, 0)
            xr, xi = xr + (mr * sr - mi * si), xi + (mr * si + mi * sr)
        xr = xr + (cpr * cr - cpi * ci)
        xi = xi + (cpr * ci + cpi * cr)
        hr_scr[pl.ds(r0, SUBLANES), :] = xr
        hi_scr[pl.ds(r0, SUBLANES), :] = xi
        last = SUBLANES - 1
        return (jnp.broadcast_to(xr[last:, :], (SUBLANES, S5_HW)),
                jnp.broadcast_to(xi[last:, :], (SUBLANES, S5_HW)))

    zero = jnp.zeros((SUBLANES, S5_HW), F32)
    cr, ci = lax.fori_loop(0, SEQ // SUBLANES, tile, (zero, zero), unroll=2)
    hre_ref[...] = cr[:1]
    him_ref[...] = ci[:1]
    y_ref[...] = _s5_readout(hr_scr[...], hi_scr[...], cre_ref, cim_ref, d_ref, u)


def _s5_sample_kernel(u_ref, h0r_ref, h0i_ref, lre_ref, lim_ref, ls_ref, bre_ref, bim_ref, cre_ref, cim_ref,
                      d_ref, y_ref, hre_ref, him_ref, hr_scr, hi_scr):
    ar, ai, bbr, bbi = _s5_discretize(lre_ref, lim_ref, ls_ref, bre_ref, bim_ref)
    u = u_ref[...]
    bu_r = _dot3(u, bbr)
    bu_i = _dot3(u, bbi)
    hr, hi = h0r_ref[...], h0i_ref[...]
    for t in range(DEC_SEQ):
        rows = slice(t * DEC_BATCH, (t + 1) * DEC_BATCH)
        hr, hi = ar * hr - ai * hi + bu_r[rows], ar * hi + ai * hr + bu_i[rows]
        hr_scr[rows, :] = hr
        hi_scr[rows, :] = hi
    hre_ref[...] = hr
    him_ref[...] = hi
    y_ref[...] = _s5_readout(hr_scr[...], hi_scr[...], cre_ref, cim_ref, d_ref, u)


def _s5_params(lam_re, lam_im, log_step, b_re, b_im, c_re, c_im, d):
    eye = jnp.eye(S5_GT, dtype=F32)
    chan = lambda a: a.reshape(S5_TILES, 1, S5_HW)

    def b_blk(b):
        x = b.transpose(0, 2, 1).reshape(S5_TILES, S5_GT, S5_C, S5_P)
        return jnp.einsum('tgcp,gh->tgchp', x, eye).reshape(S5_TILES, S5_UW, S5_HW)

    def c_blk(c):
        x = c.transpose(0, 2, 1).reshape(S5_TILES, S5_GT, S5_P, S5_C)
        return jnp.einsum('tgpc,gh->tgphc', x, eye).reshape(S5_TILES, S5_HW, S5_UW)

    ls = jnp.broadcast_to(log_step[:, None], (S5_G, S5_P))
    return (chan(lam_re), chan(lam_im), chan(ls), b_blk(b_re), b_blk(b_im), c_blk(c_re), c_blk(c_im),
            d.reshape(S5_TILES, 1, S5_UW))


def _s5_param_specs(tile_of):
    chan = pl.BlockSpec((None, 1, S5_HW), lambda *g: (tile_of(*g), 0, 0))
    bspec = pl.BlockSpec((None, S5_UW, S5_HW), lambda *g: (tile_of(*g), 0, 0))
    cspec = pl.BlockSpec((None, S5_HW, S5_UW), lambda *g: (tile_of(*g), 0, 0))
    dspec = pl.BlockSpec((None, 1, S5_UW), lambda *g: (tile_of(*g), 0, 0))
    return [chan, chan, chan, bspec, bspec, cspec, cspec, dspec]


def _s5_prompt(q, params):
    col0 = COL_S5 // S5_UW
    state = jax.ShapeDtypeStruct((BATCH, 1, S5_G * S5_P), F32)
    st_spec = pl.BlockSpec((None, 1, S5_HW), lambda b, g: (b, 0, g))
    return pl.pallas_call(
        _s5_prompt_kernel,
        grid=(BATCH, S5_TILES),
        in_specs=[pl.BlockSpec((SEQ, S5_UW), lambda b, g: (b, col0 + g))] + _s5_param_specs(lambda b, g: g),
        out_specs=[pl.BlockSpec((SEQ, S5_UW), lambda b, g: (b, g)), st_spec, st_spec],
        out_shape=[jax.ShapeDtypeStruct((M_PROMPT, S5W), F32), state, state],
        scratch_shapes=[pltpu.VMEM((SEQ, S5_HW), F32), pltpu.VMEM((SEQ, S5_HW), F32)],
        compiler_params=_cparams(("parallel", "parallel"), 48),
        name="s5_prompt",
    )(q, *params)


def _s5_sample(q, h0_re, h0_im, params):
    col0 = COL_S5 // S5_UW
    state = jax.ShapeDtypeStruct((DEC_BATCH, S5_G * S5_P), F32)
    st_spec = pl.BlockSpec((DEC_BATCH, S5_HW), lambda g: (0, g))
    return pl.pallas_call(
        _s5_sample_kernel,
        grid=(S5_TILES,),
        in_specs=[pl.BlockSpec((M_SAMPLE, S5_UW), lambda g: (0, col0 + g)), st_spec, st_spec]
                 + _s5_param_specs(lambda g: g),
        out_specs=[pl.BlockSpec((M_SAMPLE, S5_UW), lambda g: (0, g)), st_spec, st_spec],
        out_shape=[jax.ShapeDtypeStruct((M_SAMPLE, S5W), F32), state, state],
        scratch_shapes=[pltpu.VMEM((M_SAMPLE, S5_HW), F32), pltpu.VMEM((M_SAMPLE, S5_HW), F32)],
        compiler_params=_cparams(("parallel",), 32),
        name="s5_sample",
    )(q, h0_re, h0_im, *params)


def _glu_norm_kernel(y_ref, w_ref, b_ref, g_ref, *rest):
    o_ref = rest[-1]
    y = y_ref[...]
    z = 0.5 * y * (1.0 + lax.erf(y * math.sqrt(0.5)))
    gate = jax.nn.sigmoid(jnp.dot(z.astype(BF16), w_ref[...], preferred_element_type=F32) + b_ref[...])
    out = z * gate
    ms = jnp.mean(out * out, axis=-1, keepdims=True)
    o_ref[...] = (out * lax.rsqrt(ms + EPS_RMS) * g_ref[...]).astype(o_ref.dtype)


def _glu_norm(y, w_glu_bf16, layer, b_glu, gain, row0, joined=None, tm=ROW_TM):
    m = y.shape[0]
    vec = pl.BlockSpec((1, S5W), lambda i: (0, 0))
    in_specs, args, aliases = _joined_out(
        [pl.BlockSpec((tm, S5W), lambda i: (i, 0)),
         pl.BlockSpec((None, S5W, S5W), lambda i: (layer, 0, 0)), vec, vec],
        [y, w_glu_bf16, b_glu.reshape(1, S5W), gain.reshape(1, S5W)], joined)
    return pl.pallas_call(
        _glu_norm_kernel,
        grid=(m // tm,),
        in_specs=in_specs,
        out_specs=pl.BlockSpec((tm, S5W), lambda i: (i + row0 // tm, 0)),
        out_shape=jax.ShapeDtypeStruct((M_ALL, S5W), BF16),
        input_output_aliases=aliases,
        compiler_params=_cparams(("parallel",), 48),
        name="s5_glu_norm",
    )(*args)


def kernel(x_prompt, x_sample, state_shift, state_wkv, state_ssm_re, state_ssm_im, ffn1_norm, ffn1_w_gate, ffn1_w_up, ffn1_w_down, mix_norm, w_in, shift_mu, rw_w0, rw_w_up, rw_a0, rw_a_up, rw_g_up, rw_k_k, rw_k_a, rw_r_k, rw_lnx_w, rw_lnx_b, s5_lam_re, s5_lam_im, s5_b_re, s5_b_im, s5_c_re, s5_c_im, s5_d, s5_log_step, s5_w_glu, s5_b_glu, s5_out_norm, w_out, ffn2_norm, ffn2_w_gate, ffn2_w_up, ffn2_w_down, final_norm):
    x = jnp.concatenate([x_prompt.reshape(M_PROMPT, D_MODEL),
                         x_sample.transpose(1, 0, 2).reshape(M_SAMPLE, D_MODEL)], axis=0)
    zero_state = jnp.zeros((HEAD, HEAD, BATCH * HEADS), F32)
    shift_p, wkv_p, re_p, im_p, shift_s, wkv_s, re_s, im_s = ([] for _ in range(8))
    wo, w_glu = w_out.astype(BF16), s5_w_glu.astype(BF16)
    w_main, w_tail, mu_pad = _in_proj_tail(w_in, shift_mu)

    for l in range(DEPTH):
        x = _ffn(x, ffn1_norm[l], ffn1_w_gate, ffn1_w_up, ffn1_w_down, l)

        h_mix = _rmsnorm(x, mix_norm[l], BF16, ROW_TM)
        last8 = _rmsnorm(x, mix_norm[l], F32, SUBLANES, SEQ - SUBLANES, BATCH * SUBLANES, SEQ // SUBLANES)
        shift_p.append(last8[SUBLANES - 1::SUBLANES])
        shift_s.append(_rmsnorm(x, mix_norm[l], F32, DEC_BATCH, M_ALL - DEC_BATCH, DEC_BATCH))

        q_p = _win_prompt(h_mix, w_main, w_tail, mu_pad, l)
        q_s = _win_sample(jnp.concatenate([state_shift[l].astype(BF16), h_mix[M_PROMPT:]], axis=0),
                          w_main, w_tail, mu_pad, l)

        rw_args = (rw_w0[l], rw_a0[l], rw_k_k[l], rw_k_a[l], rw_r_k[l], rw_w_up[l], rw_a_up[l], rw_g_up[l])
        *scan_p, gate_p, bonus_p = _rwkv_pre(q_p, *rw_args, time_major=True, tm=LANES)
        *scan_s, gate_s, bonus_s = _rwkv_pre(q_s, *rw_args, time_major=False)
        scan_p = [_to_scan_channel_major(a) for a in scan_p]
        scan_s = [_to_scan(a, DEC_SEQ) for a in scan_s]

        s5p = _s5_params(s5_lam_re[l], s5_lam_im[l], s5_log_step[l], s5_b_re[l], s5_b_im[l],
                         s5_c_re[l], s5_c_im[l], s5_d[l])
        y5_p, hre_p, him_p = _s5_prompt(q_p, s5p)
        y5_s, hre_s, him_s = _s5_sample(q_s, state_ssm_re[l].reshape(DEC_BATCH, S5_G * S5_P),
                                        state_ssm_im[l].reshape(DEC_BATCH, S5_G * S5_P), s5p)
        re_p.append(hre_p.reshape(BATCH, S5_G, S5_P))
        im_p.append(him_p.reshape(BATCH, S5_G, S5_P))
        re_s.append(hre_s.reshape(DEC_BATCH, S5_G, S5_P))
        im_s.append(him_s.reshape(DEC_BATCH, S5_G, S5_P))
        y_s5 = _glu_norm(y5_p, w_glu, l, s5_b_glu[l], s5_out_norm[l], 0)
        y_s5 = _glu_norm(y5_s, w_glu, l, s5_b_glu[l], s5_out_norm[l], M_PROMPT, y_s5)

        y_p, s_p = _wkv_scan(*scan_p, zero_state, y_s5, tc=32)
        y_s, s_s = _wkv_scan(*scan_s, _state_to_scan(state_wkv[l]), y_s5, tc=DEC_SEQ)
        wkv_p.append(_state_from_scan(s_p, BATCH))
        wkv_s.append(_state_from_scan(s_s, DEC_BATCH))
        y_rw = _rwkv_post(_from_scan(y_p, SEQ), bonus_p, gate_p, rw_lnx_w[l], rw_lnx_b[l], 0, True)
        y_rw = _rwkv_post(_from_scan(y_s, DEC_SEQ).reshape(M_SAMPLE, RW), bonus_s, gate_s,
                          rw_lnx_w[l], rw_lnx_b[l], M_PROMPT, False, y_rw)

        x = _res_matmul([y_rw, y_s5], wo, l, x, 1.0, tm=1088, tn=512, vmem_mib=48)
        x = _ffn(x, ffn2_norm[l], ffn2_w_gate, ffn2_w_up, ffn2_w_down, l)

    y_prompt = _rmsnorm(x, final_norm, F32, ROW_TM, 0, M_PROMPT).reshape(BATCH, SEQ, D_MODEL)
    y_sample = _rmsnorm(x, final_norm, F32, ROW_TM, M_PROMPT, M_SAMPLE).reshape(
        DEC_SEQ, DEC_BATCH, D_MODEL).transpose(1, 0, 2)
    st = jnp.stack
    return (y_prompt, y_sample, st(shift_p), st(wkv_p), st(re_p), st(im_p),
            st(shift_s), st(wkv_s), st(re_s), st(im_s))
```

```python
import functools
import math

import jax
import jax.numpy as jnp
from jax import lax
from jax.experimental import pallas as pl
from jax.experimental.pallas import tpu as pltpu

F32 = jnp.float32
BF16 = jnp.bfloat16

D_MODEL = 4096
BATCH = 4
SEQ = 2048
DEPTH = 2
DEC_BATCH = 128
DEC_SEQ = 4
M_PROMPT = BATCH * SEQ
M_SAMPLE = DEC_BATCH * DEC_SEQ
M_ALL = M_PROMPT + M_SAMPLE

RW = D_MODEL // 2
HEAD = 64
HEADS = RW // HEAD
S5W = D_MODEL - RW
S5_C = 16
S5_G = S5W // S5_C
S5_P = 64
W_LORA = 96
A_LORA = 96
G_LORA = 256
LORA_PAD = 128
LORA_W = 2 * LORA_PAD + G_LORA
D_FF = 11008
EPS_RMS = 1e-6
EPS_GN = 64e-5

COL_LORA = 3 * RW
COL_S5 = COL_LORA + LORA_W
D_INP = COL_S5 + S5W

V7X_VMEM_BYTES = 64 * 1024 * 1024
LANES = 128
SUBLANES = 8

S5_GT = 8
S5_TILES = S5_G // S5_GT
S5_UW = S5_GT * S5_C
S5_HW = S5_GT * S5_P

ROW_TM = 256


def _cparams(semantics, vmem_mib):
    assert vmem_mib * 1024 * 1024 < V7X_VMEM_BYTES
    return pltpu.CompilerParams(dimension_semantics=semantics,
                                vmem_limit_bytes=vmem_mib * 1024 * 1024)


def _joined_out(in_specs, args, joined):
    if joined is None:
        return in_specs, args, {}
    return (in_specs + [pl.BlockSpec(memory_space=pl.ANY)], args + [joined], {len(args): 0})


def _rmsnorm_kernel(x_ref, g_ref, o_ref):
    x = x_ref[...]
    ms = jnp.mean(x * x, axis=-1, keepdims=True)
    o_ref[...] = (x * lax.rsqrt(ms + EPS_RMS) * g_ref[...]).astype(o_ref.dtype)


def _rmsnorm(x, g, out_dtype, tm, row0=0, rows=None, block_stride=1):
    d = x.shape[1]
    m = x.shape[0] if rows is None else rows
    return pl.pallas_call(
        _rmsnorm_kernel,
        grid=(m // tm,),
        in_specs=[pl.BlockSpec((tm, d), lambda i: (i * block_stride + row0 // tm, 0)),
                  pl.BlockSpec((1, d), lambda i: (0, 0))],
        out_specs=pl.BlockSpec((tm, d), lambda i: (i, 0)),
        out_shape=jax.ShapeDtypeStruct((m, d), out_dtype),
        compiler_params=_cparams(("parallel",), 40),
        name="rmsnorm",
    )(x, g.reshape(1, d))


WD_CAST_ROWS = 128


def _gate_up_kernel(h_ref, wg_ref, wu_ref, wd_ref, o_ref, wd_bf16_ref, *, cast_passes):
    h = h_ref[...]
    a = jnp.dot(h, wg_ref[...].astype(BF16), preferred_element_type=F32)
    b = jnp.dot(h, wu_ref[...].astype(BF16), preferred_element_type=F32)
    o_ref[...] = (a * jax.nn.sigmoid(a) * b).astype(o_ref.dtype)

    @pl.when(pl.program_id(0) < cast_passes)
    def _():
        wd_bf16_ref[...] = wd_ref[...].astype(BF16)


def _gate_up(h, wg, wu, wd, layer, tm=2176, tn=256):
    m, d = h.shape
    f = wg.shape[2]
    n_i, n_j = m // tm, f // tn
    cast_blocks = f // WD_CAST_ROWS
    cast_passes = cast_blocks // n_j
    assert cast_passes * n_j == cast_blocks and cast_passes <= n_i

    def cast_block(i, j):
        return jnp.where(i < cast_passes, i * n_j + j, cast_blocks - 1)

    w_spec = pl.BlockSpec((None, d, tn), lambda i, j: (layer, 0, j))
    return pl.pallas_call(
        functools.partial(_gate_up_kernel, cast_passes=cast_passes),
        grid=(n_i, n_j),
        in_specs=[pl.BlockSpec((tm, d), lambda i, j: (i, 0), pipeline_mode=pl.Buffered(1)), w_spec, w_spec,
                  pl.BlockSpec((None, WD_CAST_ROWS, d), lambda i, j: (layer, cast_block(i, j), 0))],
        out_specs=[pl.BlockSpec((tm, tn), lambda i, j: (i, j)),
                   pl.BlockSpec((WD_CAST_ROWS, d), lambda i, j: (cast_block(i, j), 0))],
        out_shape=[jax.ShapeDtypeStruct((m, f), BF16), jax.ShapeDtypeStruct((f, d), BF16)],
        compiler_params=_cparams(("arbitrary", "arbitrary"), 58),
        name="ffn_gate_up",
    )(h, wg, wu, wd)


def _res_matmul_kernel(*refs, n_pairs, scale, cast_w):
    a_refs = refs[:n_pairs]
    w_refs = refs[n_pairs:2 * n_pairs]
    x_ref, o_ref = refs[2 * n_pairs], refs[2 * n_pairs + 1]
    load_w = (lambda ref: ref[...].astype(BF16)) if cast_w else (lambda ref: ref[...])
    acc = jnp.dot(a_refs[0][...], load_w(w_refs[0]), preferred_element_type=F32)
    for a_ref, w_ref in zip(a_refs[1:], w_refs[1:]):
        acc = acc + jnp.dot(a_ref[...], load_w(w_ref), preferred_element_type=F32)
    if scale != 1.0:
        acc = scale * acc
    o_ref[...] = x_ref[...] + acc


def _res_matmul(a_list, w, layer, x, scale, tm, tn, vmem_mib, cast_w=False):
    m, n = x.shape
    n_pairs = len(a_list)
    in_specs = [pl.BlockSpec((tm, a.shape[1]), lambda i, j: (i, 0)) for a in a_list]
    for p, a in enumerate(a_list):
        if layer is None:
            in_specs.append(pl.BlockSpec((a.shape[1], tn), lambda i, j, p=p: (p, j)))
        else:
            in_specs.append(pl.BlockSpec((None, a.shape[1], tn), lambda i, j, p=p: (layer, p, j)))
    in_specs.append(pl.BlockSpec((tm, tn), lambda i, j: (i, j)))
    assert sum(a.shape[1] for a in a_list) == w.shape[-2]
    return pl.pallas_call(
        functools.partial(_res_matmul_kernel, n_pairs=n_pairs, scale=scale, cast_w=cast_w),
        grid=(m // tm, n // tn),
        in_specs=in_specs,
        out_specs=pl.BlockSpec((tm, tn), lambda i, j: (i, j)),
        out_shape=jax.ShapeDtypeStruct((m, n), F32),
        compiler_params=_cparams(("parallel", "arbitrary"), vmem_mib),
        name="res_matmul",
    )(*a_list, *([w] * n_pairs), x)


def _ffn(x, norm_g, wg, wu, wd, layer):
    h = _rmsnorm(x, norm_g, BF16, ROW_TM)
    act, wd_bf16 = _gate_up(h, wg, wu, wd, layer)
    return _res_matmul([act], wd_bf16, None, x, 0.5, tm=544, tn=512, vmem_mib=58)


WIN_TN = 256
WIN_MAIN_TILES = COL_LORA // WIN_TN
WIN_SHIFT_TILES = COL_S5 // WIN_TN


def _win_tile_cases(h_ref, wm_ref, wt_ref, j, emit):
    @pl.when(j < WIN_MAIN_TILES)
    def _():
        emit(jnp.dot(h_ref[...], wm_ref[...], preferred_element_type=F32), True)

    @pl.when((j >= WIN_MAIN_TILES) & (j < WIN_SHIFT_TILES))
    def _():
        emit(jnp.dot(h_ref[...], wt_ref[...], preferred_element_type=F32), True)

    @pl.when(j >= WIN_SHIFT_TILES)
    def _():
        emit(jnp.dot(h_ref[...], wt_ref[...], preferred_element_type=F32), False)


def _win_prompt_kernel(h_ref, wm_ref, wt_ref, mu_ref, o_ref):
    def emit(p, shifted):
        if shifted:
            rows = lax.broadcasted_iota(jnp.int32, p.shape, 0)
            prev = jnp.where(rows == 0, 0.0, pltpu.roll(p, 1, 0))
            o_ref[...] = p + (prev - p) * mu_ref[...]
        else:
            o_ref[...] = p

    _win_tile_cases(h_ref, wm_ref, wt_ref, pl.program_id(1), emit)


def _win_sample_kernel(h_ref, wm_ref, wt_ref, mu_ref, o_ref):
    def emit(p, shifted):
        cur = p[DEC_BATCH:]
        o_ref[...] = cur + (p[:M_SAMPLE] - cur) * mu_ref[...] if shifted else cur

    _win_tile_cases(h_ref, wm_ref, wt_ref, pl.program_id(0), emit)


def _win_weight_specs(layer, col_of):
    main = pl.BlockSpec((None, D_MODEL, WIN_TN),
                        lambda *g: (layer, 0, jnp.minimum(col_of(*g), WIN_MAIN_TILES - 1)))
    tail = pl.BlockSpec((None, D_MODEL, WIN_TN),
                        lambda *g: (layer, 0, jnp.maximum(col_of(*g) - WIN_MAIN_TILES, 0)))
    mu = pl.BlockSpec((None, 1, WIN_TN), lambda *g: (layer, 0, col_of(*g)))
    return [main, tail, mu]


def _win_prompt(h, w_main, w_tail, mu_pad, layer):
    tn = WIN_TN
    return pl.pallas_call(
        _win_prompt_kernel,
        grid=(BATCH, D_INP // tn),
        in_specs=[pl.BlockSpec((SEQ, D_MODEL), lambda b, j: (b, 0), pipeline_mode=pl.Buffered(1))]
                 + _win_weight_specs(layer, lambda b, j: j),
        out_specs=pl.BlockSpec((SEQ, tn), lambda b, j: (b, j)),
        out_shape=jax.ShapeDtypeStruct((M_PROMPT, D_INP), F32),
        compiler_params=_cparams(("parallel", "arbitrary"), 56),
        name="in_proj_prompt",
    )(h, w_main, w_tail, mu_pad)


def _win_sample(h_rows, w_main, w_tail, mu_pad, layer):
    tn = WIN_TN
    rows = h_rows.shape[0]
    return pl.pallas_call(
        _win_sample_kernel,
        grid=(D_INP // tn,),
        in_specs=[pl.BlockSpec((rows, D_MODEL), lambda j: (0, 0))] + _win_weight_specs(layer, lambda j: j),
        out_specs=pl.BlockSpec((M_SAMPLE, tn), lambda j: (0, j)),
        out_shape=jax.ShapeDtypeStruct((M_SAMPLE, D_INP), F32),
        compiler_params=_cparams(("arbitrary",), 32),
        name="in_proj_sample",
    )(h_rows, w_main, w_tail, mu_pad)


def _in_proj_tail(w_in, mu):
    o_w, o_a, o_g = COL_LORA, COL_LORA + W_LORA, COL_LORA + W_LORA + A_LORA
    o_s = o_g + G_LORA
    w_main = w_in.astype(BF16)
    zw = jnp.zeros((DEPTH, D_MODEL, LORA_PAD - W_LORA), BF16)
    w_tail = jnp.concatenate([w_main[:, :, o_w:o_a], zw, w_main[:, :, o_a:o_g], zw,
                              w_main[:, :, o_g:]], axis=2)
    zm = jnp.zeros((DEPTH, LORA_PAD - W_LORA), mu.dtype)
    mu_pad = jnp.concatenate([mu[:, :o_w], mu[:, o_w:o_a], zm, mu[:, o_a:o_g], zm, mu[:, o_g:o_s],
                              jnp.zeros((DEPTH, S5W), mu.dtype)], axis=1).reshape(DEPTH, 1, D_INP)
    return w_main, w_tail, mu_pad


def _head_sum(x):
    r = lax.broadcasted_iota(jnp.int32, (LANES, LANES), 0) // HEAD
    c = lax.broadcasted_iota(jnp.int32, (LANES, LANES), 1) // HEAD
    ones = (r == c).astype(BF16)
    hi = x.astype(BF16)
    r1 = x - hi.astype(F32)
    mid = r1.astype(BF16)
    lo = (r1 - mid.astype(F32)).astype(BF16)
    outs = []
    for s in range(x.shape[1] // LANES):
        sl = slice(s * LANES, (s + 1) * LANES)
        acc = jnp.dot(hi[:, sl], ones, preferred_element_type=F32)
        acc = acc + jnp.dot(mid[:, sl], ones, preferred_element_type=F32)
        acc = acc + jnp.dot(lo[:, sl], ones, preferred_element_type=F32)
        outs.append(acc)
    return jnp.concatenate(outs, axis=1)


def _rwkv_pre_kernel(r_ref, k_ref, v_ref, lora_ref, w0_ref, a0_ref, kk_ref, ka_ref, rk_ref,
                     wup_ref, aup_ref, gup_ref, *outs, channel_major):
    *scan_outs, g_out, bonus_out = outs
    r_out, w_out, k_out, v_out, kk_out, b_out = range(6)

    def emit(idx, val):
        if channel_major:
            scan_outs[idx][...] = val.T
        else:
            scan_outs[0][idx] = val

    r = r_ref[...]
    k = k_ref[...]
    v = v_ref[...]
    emit(r_out, r)
    emit(v_out, v)
    lora = lora_ref[...]
    wd = lora[:, :LORA_PAD]
    ad = lora[:, LORA_PAD:2 * LORA_PAD]
    gd = lora[:, 2 * LORA_PAD:]
    z = -(w0_ref[...] + jnp.dot(jnp.tanh(wd).astype(BF16), wup_ref[...], preferred_element_type=F32))
    softplus = jnp.maximum(z, 0.0) + jnp.log1p(jnp.exp(-jnp.abs(z)))
    emit(w_out, jnp.exp(-jnp.exp(-softplus - 0.5)))
    a = jax.nn.sigmoid(a0_ref[...] + jnp.dot(ad.astype(BF16), aup_ref[...], preferred_element_type=F32))
    g_out[...] = jnp.dot(jax.nn.sigmoid(gd).astype(BF16), gup_ref[...], preferred_element_type=F32)
    kk = k * kk_ref[...]
    k2 = k * (1.0 + (a - 1.0) * ka_ref[...])
    emit(k_out, k2)
    kkn = kk / jnp.maximum(jnp.sqrt(_head_sum(kk * kk)), 1e-12)
    emit(kk_out, kkn)
    emit(b_out, kkn * a)
    bonus_out[...] = _head_sum(r * k2 * rk_ref[...]) * v


def _time_major_spec(m, tm, time_major):
    if not time_major:
        return pl.BlockSpec((tm, RW), lambda i: (i, 0)), (m, RW)
    per_seq = SEQ // tm
    return pl.BlockSpec((tm, RW), lambda i: (i % per_seq, i // per_seq)), (SEQ, (m // SEQ) * RW)


def _rwkv_pre(q, w0, a0, k_k, k_a, r_k, w_up, a_up, g_up, time_major, tm=ROW_TM):
    m = q.shape[0]
    out_spec, out_dims = _time_major_spec(m, tm, time_major)
    if time_major:
        per_seq = SEQ // tm
        scan_specs = [pl.BlockSpec((RW, tm), lambda i: (i // per_seq, i % per_seq))] * 6
        scan_outs = [jax.ShapeDtypeStruct(((m // SEQ) * RW, SEQ), F32)] * 6
    else:
        scan_specs = [pl.BlockSpec((6, tm, RW), lambda i: (0, i, 0))]
        scan_outs = [jax.ShapeDtypeStruct((6, m, RW), F32)]
    row = lambda v: v.reshape(1, RW)
    pad_rows = lambda u: jnp.concatenate(
        [u, jnp.zeros((LORA_PAD - u.shape[0], RW), u.dtype)], axis=0).astype(BF16)
    vec = pl.BlockSpec((1, RW), lambda i: (0, 0))
    full = lambda rows: pl.BlockSpec((rows, RW), lambda i: (0, 0))
    out = jax.ShapeDtypeStruct(out_dims, F32)
    return pl.pallas_call(
        functools.partial(_rwkv_pre_kernel, channel_major=time_major),
        grid=(m // tm,),
        in_specs=[pl.BlockSpec((tm, RW), lambda i: (i, 0)),
                  pl.BlockSpec((tm, RW), lambda i: (i, 1)),
                  pl.BlockSpec((tm, RW), lambda i: (i, 2)),
                  pl.BlockSpec((tm, LORA_W), lambda i: (i, COL_LORA // LORA_W)),
                  vec, vec, vec, vec, vec, full(LORA_PAD), full(LORA_PAD), full(G_LORA)],
        out_specs=scan_specs + [out_spec] * 2,
        out_shape=scan_outs + [out] * 2,
        compiler_params=_cparams(("parallel",), 56),
        name="rwkv_pre",
    )(q, q, q, q, row(w0), row(a0), row(k_k), row(k_a), row(r_k.reshape(RW)),
      pad_rows(w_up), pad_rows(a_up), g_up.astype(BF16))


def _rwkv_post_kernel(y_ref, bonus_ref, g_ref, lw_ref, lb_ref, *rest):
    o_ref = rest[-1]
    y = y_ref[...]
    mean = _head_sum(y) * (1.0 / HEAD)
    c = y - mean
    var = _head_sum(c * c) * (1.0 / HEAD)
    yn = c * lax.rsqrt(var + EPS_GN) * lw_ref[...] + lb_ref[...]
    o_ref[...] = ((yn + bonus_ref[...]) * g_ref[...]).astype(o_ref.dtype)


def _rwkv_post(y, bonus, g, lnx_w, lnx_b, row0, time_major, joined=None, tm=ROW_TM):
    m = y.size // RW
    blk, _ = _time_major_spec(m, tm, time_major)
    vec = pl.BlockSpec((1, RW), lambda i: (0, 0))
    in_specs, args, aliases = _joined_out(
        [blk, blk, blk, vec, vec], [y, bonus, g, lnx_w.reshape(1, RW), lnx_b.reshape(1, RW)], joined)
    return pl.pallas_call(
        _rwkv_post_kernel,
        grid=(m // tm,),
        in_specs=in_specs,
        out_specs=pl.BlockSpec((tm, RW), lambda i: (i + row0 // tm, 0)),
        out_shape=jax.ShapeDtypeStruct((M_ALL, RW), BF16),
        input_output_aliases=aliases,
        compiler_params=_cparams(("parallel",), 40),
        name="rwkv_post",
    )(*args)


WKV_UNROLL = 8


def _wkv_kernel(r_ref, w_ref, k_ref, v_ref, kk_ref, b_ref, s0_ref, after_ref, y_ref, s_ref,
                g_ref, kq_ref, wr_ref, bt_ref, kt_ref, *, steps):
    @pl.when(pl.program_id(1) == 0)
    def _():
        s_ref[...] = s0_ref[...]

    g_ref[...] = jnp.ones((HEAD, LANES), F32)

    def step(t, carry):
        r, k, b = r_ref[t], k_ref[t], b_ref[t]
        g_prev = g_ref[...]
        g = g_prev * w_ref[t]
        g_inv = 1.0 / g
        g_ref[...] = g
        kq_ref[...] = g_prev * kk_ref[t]
        wr_ref[...] = g * r
        bt_ref[...] = b * g_inv
        kt_ref[...] = k * g_inv
        beta = jnp.sum(b * r, axis=0, keepdims=True)
        kappa = jnp.sum(k * r, axis=0, keepdims=True)
        v = v_ref[t]

        def contract(j, acc):
            sa, u = acc
            sj = s_ref[j]
            return sa - sj * kq_ref[pl.ds(j, 1), :], u + sj * wr_ref[pl.ds(j, 1), :]

        zero = jnp.zeros((HEAD, LANES), F32)
        sa, u = lax.fori_loop(0, HEAD, contract, (zero, zero), unroll=WKV_UNROLL)

        def update(j, c):
            s_ref[j] = s_ref[j] + sa * bt_ref[pl.ds(j, 1), :] + v * kt_ref[pl.ds(j, 1), :]
            return c

        lax.fori_loop(0, HEAD, update, 0, unroll=WKV_UNROLL)
        y_ref[t] = u + sa * beta + v * kappa
        return carry

    lax.fori_loop(0, steps, step, 0)

    def denormalise(j, c):
        s_ref[j] = s_ref[j] * g_ref[pl.ds(j, 1), :]
        return c

    lax.fori_loop(0, HEAD, denormalise, 0, unroll=WKV_UNROLL)


def _wkv_scan(vectors, s0, after, tc):
    vec = pl.BlockSpec((tc, HEAD, LANES), lambda c, t: (t, 0, c))
    if isinstance(vectors, (list, tuple)):
        r, w, k, v, kk, b = vectors
        vec_specs = [vec] * 6
    else:
        r = w = k = v = kk = b = vectors
        vec_specs = [pl.BlockSpec((None, tc, HEAD, LANES), lambda c, t, i=i: (i, t, 0, c)) for i in range(6)]
    length, _, n = r.shape[-3:]
    st = pl.BlockSpec((HEAD, HEAD, LANES), lambda c, t: (0, 0, c))
    return pl.pallas_call(
        functools.partial(_wkv_kernel, steps=tc),
        grid=(n // LANES, length // tc),
        in_specs=vec_specs + [st, pl.BlockSpec(memory_space=pl.ANY)],
        out_specs=[vec, st],
        out_shape=[jax.ShapeDtypeStruct((length, HEAD, n), F32),
                   jax.ShapeDtypeStruct((HEAD, HEAD, n), F32)],
        scratch_shapes=[pltpu.VMEM((HEAD, LANES), F32)] * 5,
        compiler_params=_cparams(("parallel", "arbitrary"), 40),
        name="wkv_scan",
    )(r, w, k, v, kk, b, s0, after)


def _to_scan(a, length):
    return a.reshape(length, -1, HEAD).transpose(0, 2, 1)


def _to_scan_channel_major(a):
    return a.reshape(-1, HEAD, a.shape[1]).transpose(2, 1, 0)


def _from_scan(y, length):
    return y.transpose(0, 2, 1).reshape(length, -1)


def _state_to_scan(s):
    n = s.shape[0] * s.shape[1]
    s = lax.optimization_barrier(s.reshape(n, HEAD, HEAD).transpose(0, 2, 1))
    return s.reshape(n, HEAD * HEAD).T.reshape(HEAD, HEAD, n)


def _state_from_scan(s, batch):
    n = s.shape[2]
    s = lax.optimization_barrier(s.reshape(HEAD * HEAD, n).T)
    return s.reshape(n, HEAD, HEAD).transpose(0, 2, 1).reshape(batch, HEADS, HEAD, HEAD)


def _s5_discretize(lre_ref, lim_ref, ls_ref, bre_ref, bim_ref):
    lre, lim = lre_ref[...], lim_ref[...]
    step = jnp.exp(ls_ref[...])
    mag = jnp.exp(lre * step)
    ar = mag * jnp.cos(lim * step)
    ai = mag * jnp.sin(lim * step)
    den = lre * lre + lim * lim
    nr = ar - 1.0
    f_re = (nr * lre + ai * lim) / den
    f_im = (ai * lre - nr * lim) / den
    bre, bim = bre_ref[...], bim_ref[...]
    return ar, ai, f_re * bre - f_im * bim, f_re * bim + f_im * bre


def _dot_bf16(a, b):
    return jnp.dot(a.astype(BF16), b.astype(BF16), preferred_element_type=F32)


def _s5_readout(hr, hi, cre_ref, cim_ref, d_ref, u):
    return _dot_bf16(hr, cre_ref[...]) - _dot_bf16(hi, cim_ref[...]) + d_ref[...] * u


def _cmul(ar, ai, br, bi):
    return ar * br - ai * bi, ar * bi + ai * br


def _s5_prompt_kernel(u_ref, lre_ref, lim_ref, ls_ref, bre_ref, bim_ref, cre_ref, cim_ref, d_ref,
                      y_ref, hre_ref, him_ref, hr_scr, hi_scr):
    ar, ai, bbr, bbi = _s5_discretize(lre_ref, lim_ref, ls_ref, bre_ref, bim_ref)
    u = u_ref[...]
    hr_scr[...] = _dot_bf16(u, bbr)
    hi_scr[...] = _dot_bf16(u, bbi)

    rows = lax.broadcasted_iota(jnp.int32, (SUBLANES, S5_HW), 0)
    powers = [(ar, ai)]
    for _ in range(SUBLANES - 1):
        powers.append(_cmul(*powers[-1], ar, ai))
    levels = []
    for sh in (1, 2, 4):
        pr, pi = powers[sh - 1]
        levels.append((sh, jnp.where(rows >= sh, pr, 0.0), jnp.where(rows >= sh, pi, 0.0)))
    cpr = jnp.zeros((SUBLANES, S5_HW), F32)
    cpi = jnp.zeros((SUBLANES, S5_HW), F32)
    for n, (pr, pi) in enumerate(powers):
        cpr = jnp.where(rows == n, pr, cpr)
        cpi = jnp.where(rows == n, pi, cpi)

    def tile(i, carry):
        cr, ci = carry
        r0 = pl.multiple_of(i * SUBLANES, SUBLANES)
        xr = hr_scr[pl.ds(r0, SUBLANES), :]
        xi = hi_scr[pl.ds(r0, SUBLANES), :]
        for sh, mr, mi in levels:
            sr, si = pltpu.roll(xr, sh, 0), pltpu.roll(xi, sh, 0)
            xr, xi = xr + (mr * sr - mi * si), xi + (mr * si + mi * sr)
        xr = xr + (cpr * cr - cpi * ci)
        xi = xi + (cpr * ci + cpi * cr)
        hr_scr[pl.ds(r0, SUBLANES), :] = xr
        hi_scr[pl.ds(r0, SUBLANES), :] = xi
        last = SUBLANES - 1
        return (jnp.broadcast_to(xr[last:, :], (SUBLANES, S5_HW)),
                jnp.broadcast_to(xi[last:, :], (SUBLANES, S5_HW)))

    zero = jnp.zeros((SUBLANES, S5_HW), F32)
    cr, ci = lax.fori_loop(0, SEQ // SUBLANES, tile, (zero, zero), unroll=2)
    hre_ref[...] = cr[:1]
    him_ref[...] = ci[:1]
    y_ref[...] = _s5_readout(hr_scr[...], hi_scr[...], cre_ref, cim_ref, d_ref, u)


def _s5_sample_kernel(u_ref, h0r_ref, h0i_ref, lre_ref, lim_ref, ls_ref, bre_ref, bim_ref, cre_ref, cim_ref,
                      d_ref, y_ref, hre_ref, him_ref, hr_scr, hi_scr):
    ar, ai, bbr, bbi = _s5_discretize(lre_ref, lim_ref, ls_ref, bre_ref, bim_ref)
    u = u_ref[...]
    bu_r = _dot_bf16(u, bbr)
    bu_i = _dot_bf16(u, bbi)
    hr, hi = h0r_ref[...], h0i_ref[...]
    for t in range(DEC_SEQ):
        rows = slice(t * DEC_BATCH, (t + 1) * DEC_BATCH)
        hr, hi = ar * hr - ai * hi + bu_r[rows], ar * hi + ai * hr + bu_i[rows]
        hr_scr[rows, :] = hr
        hi_scr[rows, :] = hi
    hre_ref[...] = hr
    him_ref[...] = hi
    y_ref[...] = _s5_readout(hr_scr[...], hi_scr[...], cre_ref, cim_ref, d_ref, u)


def _s5_params(lam_re, lam_im, log_step, b_re, b_im, c_re, c_im, d):
    eye = jnp.eye(S5_GT, dtype=F32)
    tiles = DEPTH * S5_TILES
    chan = lambda a: a.reshape(tiles, 1, S5_HW)

    def b_blk(b):
        x = b.transpose(0, 1, 3, 2).reshape(tiles, S5_GT, S5_C, S5_P)
        return jnp.einsum('tgcp,gh->tgchp', x, eye).reshape(tiles, S5_UW, S5_HW)

    def c_blk(c):
        x = c.transpose(0, 1, 3, 2).reshape(tiles, S5_GT, S5_P, S5_C)
        return jnp.einsum('tgpc,gh->tgphc', x, eye).reshape(tiles, S5_HW, S5_UW)

    ls = jnp.broadcast_to(log_step[:, :, None], (DEPTH, S5_G, S5_P))
    return (chan(lam_re), chan(lam_im), chan(ls), b_blk(b_re), b_blk(b_im), c_blk(c_re), c_blk(c_im),
            d.reshape(tiles, 1, S5_UW))


def _s5_param_specs(tile_of):
    chan = pl.BlockSpec((None, 1, S5_HW), lambda *g: (tile_of(*g), 0, 0))
    bspec = pl.BlockSpec((None, S5_UW, S5_HW), lambda *g: (tile_of(*g), 0, 0))
    cspec = pl.BlockSpec((None, S5_HW, S5_UW), lambda *g: (tile_of(*g), 0, 0))
    dspec = pl.BlockSpec((None, 1, S5_UW), lambda *g: (tile_of(*g), 0, 0))
    return [chan, chan, chan, bspec, bspec, cspec, cspec, dspec]


def _s5_prompt(q, params, layer):
    col0 = COL_S5 // S5_UW
    state = jax.ShapeDtypeStruct((BATCH, 1, S5_G * S5_P), F32)
    st_spec = pl.BlockSpec((None, 1, S5_HW), lambda b, g: (b, 0, g))
    return pl.pallas_call(
        _s5_prompt_kernel,
        grid=(BATCH, S5_TILES),
        in_specs=[pl.BlockSpec((SEQ, S5_UW), lambda b, g: (b, col0 + g))]
                 + _s5_param_specs(lambda b, g: layer * S5_TILES + g),
        out_specs=[pl.BlockSpec((SEQ, S5_UW), lambda b, g: (b, g)), st_spec, st_spec],
        out_shape=[jax.ShapeDtypeStruct((M_PROMPT, S5W), F32), state, state],
        scratch_shapes=[pltpu.VMEM((SEQ, S5_HW), F32), pltpu.VMEM((SEQ, S5_HW), F32)],
        compiler_params=_cparams(("parallel", "parallel"), 48),
        name="s5_prompt",
    )(q, *params)


def _s5_sample(q, h0_re, h0_im, params, layer):
    col0 = COL_S5 // S5_UW
    state = jax.ShapeDtypeStruct((DEC_BATCH, S5_G * S5_P), F32)
    st_spec = pl.BlockSpec((DEC_BATCH, S5_HW), lambda g: (0, g))
    return pl.pallas_call(
        _s5_sample_kernel,
        grid=(S5_TILES,),
        in_specs=[pl.BlockSpec((M_SAMPLE, S5_UW), lambda g: (0, col0 + g)), st_spec, st_spec]
                 + _s5_param_specs(lambda g: layer * S5_TILES + g),
        out_specs=[pl.BlockSpec((M_SAMPLE, S5_UW), lambda g: (0, g)), st_spec, st_spec],
        out_shape=[jax.ShapeDtypeStruct((M_SAMPLE, S5W), F32), state, state],
        scratch_shapes=[pltpu.VMEM((M_SAMPLE, S5_HW), F32), pltpu.VMEM((M_SAMPLE, S5_HW), F32)],
        compiler_params=_cparams(("parallel",), 32),
        name="s5_sample",
    )(q, h0_re, h0_im, *params)


def _glu_norm_kernel(y_ref, w_ref, b_ref, g_ref, *rest):
    o_ref = rest[-1]
    y = y_ref[...]
    z = 0.5 * y * (1.0 + lax.erf(y * math.sqrt(0.5)))
    gate = jax.nn.sigmoid(jnp.dot(z.astype(BF16), w_ref[...], preferred_element_type=F32) + b_ref[...])
    out = z * gate
    ms = jnp.mean(out * out, axis=-1, keepdims=True)
    o_ref[...] = (out * lax.rsqrt(ms + EPS_RMS) * g_ref[...]).astype(o_ref.dtype)


def _glu_norm(y, w_glu_bf16, layer, b_glu, gain, row0, joined=None, tm=ROW_TM):
    m = y.shape[0]
    vec = pl.BlockSpec((1, S5W), lambda i: (0, 0))
    in_specs, args, aliases = _joined_out(
        [pl.BlockSpec((tm, S5W), lambda i: (i, 0)),
         pl.BlockSpec((None, S5W, S5W), lambda i: (layer, 0, 0)), vec, vec],
        [y, w_glu_bf16, b_glu.reshape(1, S5W), gain.reshape(1, S5W)], joined)
    return pl.pallas_call(
        _glu_norm_kernel,
        grid=(m // tm,),
        in_specs=in_specs,
        out_specs=pl.BlockSpec((tm, S5W), lambda i: (i + row0 // tm, 0)),
        out_shape=jax.ShapeDtypeStruct((M_ALL, S5W), BF16),
        input_output_aliases=aliases,
        compiler_params=_cparams(("parallel",), 48),
        name="s5_glu_norm",
    )(*args)


def kernel(x_prompt, x_sample, state_shift, state_wkv, state_ssm_re, state_ssm_im, ffn1_norm, ffn1_w_gate, ffn1_w_up, ffn1_w_down, mix_norm, w_in, shift_mu, rw_w0, rw_w_up, rw_a0, rw_a_up, rw_g_up, rw_k_k, rw_k_a, rw_r_k, rw_lnx_w, rw_lnx_b, s5_lam_re, s5_lam_im, s5_b_re, s5_b_im, s5_c_re, s5_c_im, s5_d, s5_log_step, s5_w_glu, s5_b_glu, s5_out_norm, w_out, ffn2_norm, ffn2_w_gate, ffn2_w_up, ffn2_w_down, final_norm):
    x = jnp.concatenate([x_prompt.reshape(M_PROMPT, D_MODEL),
                         x_sample.transpose(1, 0, 2).reshape(M_SAMPLE, D_MODEL)], axis=0)
    zero_state = jnp.zeros((HEAD, HEAD, BATCH * HEADS), F32)
    shift_p, wkv_p, re_p, im_p, shift_s, wkv_s, re_s, im_s = ([] for _ in range(8))
    w_glu = s5_w_glu.astype(BF16)
    s5p = _s5_params(s5_lam_re, s5_lam_im, s5_log_step, s5_b_re, s5_b_im, s5_c_re, s5_c_im, s5_d)
    w_main, w_tail, mu_pad = _in_proj_tail(w_in, shift_mu)

    for l in range(DEPTH):
        x = _ffn(x, ffn1_norm[l], ffn1_w_gate, ffn1_w_up, ffn1_w_down, l)

        h_mix = _rmsnorm(x, mix_norm[l], BF16, ROW_TM)
        last8 = _rmsnorm(x, mix_norm[l], F32, SUBLANES, SEQ - SUBLANES, BATCH * SUBLANES, SEQ // SUBLANES)
        shift_p.append(last8[SUBLANES - 1::SUBLANES])
        shift_s.append(_rmsnorm(x, mix_norm[l], F32, DEC_BATCH, M_ALL - DEC_BATCH, DEC_BATCH))

        q_p = _win_prompt(h_mix, w_main, w_tail, mu_pad, l)
        q_s = _win_sample(jnp.concatenate([state_shift[l].astype(BF16), h_mix[M_PROMPT:]], axis=0),
                          w_main, w_tail, mu_pad, l)

        rw_args = (rw_w0[l], rw_a0[l], rw_k_k[l], rw_k_a[l], rw_r_k[l], rw_w_up[l], rw_a_up[l], rw_g_up[l])
        *scan_p, gate_p, bonus_p = _rwkv_pre(q_p, *rw_args, time_major=True, tm=LANES)
        scan_s, gate_s, bonus_s = _rwkv_pre(q_s, *rw_args, time_major=False)
        scan_p = [_to_scan_channel_major(a) for a in scan_p]
        scan_s = _to_scan(scan_s, 6 * DEC_SEQ).reshape(6, DEC_SEQ, HEAD, DEC_BATCH * HEADS)

        y5_p, hre_p, him_p = _s5_prompt(q_p, s5p, l)
        y5_s, hre_s, him_s = _s5_sample(q_s, state_ssm_re[l].reshape(DEC_BATCH, S5_G * S5_P),
                                        state_ssm_im[l].reshape(DEC_BATCH, S5_G * S5_P), s5p, l)
        re_p.append(hre_p.reshape(BATCH, S5_G, S5_P))
        im_p.append(him_p.reshape(BATCH, S5_G, S5_P))
        re_s.append(hre_s.reshape(DEC_BATCH, S5_G, S5_P))
        im_s.append(him_s.reshape(DEC_BATCH, S5_G, S5_P))
        y_s5 = _glu_norm(y5_p, w_glu, l, s5_b_glu[l], s5_out_norm[l], 0)
        y_s5 = _glu_norm(y5_s, w_glu, l, s5_b_glu[l], s5_out_norm[l], M_PROMPT, y_s5)

        y_p, s_p = _wkv_scan(scan_p, zero_state, y_s5, tc=32)
        y_s, s_s = _wkv_scan(scan_s, _state_to_scan(state_wkv[l]), y_s5, tc=DEC_SEQ)
        wkv_p.append(_state_from_scan(s_p, BATCH))
        wkv_s.append(_state_from_scan(s_s, DEC_BATCH))
        y_rw = _rwkv_post(_from_scan(y_p, SEQ), bonus_p, gate_p, rw_lnx_w[l], rw_lnx_b[l], 0, True)
        y_rw = _rwkv_post(_from_scan(y_s, DEC_SEQ).reshape(M_SAMPLE, RW), bonus_s, gate_s,
                          rw_lnx_w[l], rw_lnx_b[l], M_PROMPT, False, y_rw)

        x = _res_matmul([y_rw, y_s5], w_out, l, x, 1.0, tm=1088, tn=512, vmem_mib=56, cast_w=True)
        x = _ffn(x, ffn2_norm[l], ffn2_w_gate, ffn2_w_up, ffn2_w_down, l)

    y_prompt = _rmsnorm(x, final_norm, F32, ROW_TM, 0, M_PROMPT).reshape(BATCH, SEQ, D_MODEL)
    y_sample = _rmsnorm(x, final_norm, F32, ROW_TM, M_PROMPT, M_SAMPLE).reshape(
        DEC_SEQ, DEC_BATCH, D_MODEL).transpose(1, 0, 2)
    st = jnp.stack
    return (y_prompt, y_sample, st(shift_p), st(wkv_p), st(re_p), st(im_p),
            st(shift_s), st(wkv_s), st(re_s), st(im_s))
```

```python
import functools
import math

import jax
import jax.numpy as jnp
from jax import lax
from jax.experimental import pallas as pl
from jax.experimental.pallas import tpu as pltpu

F32 = jnp.float32
BF16 = jnp.bfloat16

D_MODEL = 4096
BATCH = 4
SEQ = 2048
DEPTH = 2
DEC_BATCH = 128
DEC_SEQ = 4
M_PROMPT = BATCH * SEQ
M_SAMPLE = DEC_BATCH * DEC_SEQ
M_ALL = M_PROMPT + M_SAMPLE

RW = D_MODEL // 2
HEAD = 64
HEADS = RW // HEAD
S5W = D_MODEL - RW
S5_C = 16
S5_G = S5W // S5_C
S5_P = 64
W_LORA = 96
A_LORA = 96
G_LORA = 256
LORA_PAD = 128
LORA_W = 2 * LORA_PAD + G_LORA
D_FF = 11008
EPS_RMS = 1e-6
EPS_GN = 64e-5

COL_LORA = 3 * RW
COL_S5 = COL_LORA + LORA_W
D_INP = COL_S5 + S5W

V7X_VMEM_BYTES = 64 * 1024 * 1024
LANES = 128
SUBLANES = 8

S5_GT = 8
S5_TILES = S5_G // S5_GT
S5_UW = S5_GT * S5_C
S5_HW = S5_GT * S5_P

ROW_TM = 256


def _cparams(semantics, vmem_mib):
    assert vmem_mib * 1024 * 1024 < V7X_VMEM_BYTES
    return pltpu.CompilerParams(dimension_semantics=semantics,
                                vmem_limit_bytes=vmem_mib * 1024 * 1024)


def _joined_out(in_specs, args, joined):
    if joined is None:
        return in_specs, args, {}
    return (in_specs + [pl.BlockSpec(memory_space=pl.ANY)], args + [joined], {len(args): 0})


def _rmsnorm_kernel(x_ref, g_ref, o_ref):
    x = x_ref[...]
    ms = jnp.mean(x * x, axis=-1, keepdims=True)
    o_ref[...] = (x * lax.rsqrt(ms + EPS_RMS) * g_ref[...]).astype(o_ref.dtype)


def _rmsnorm(x, g, out_dtype, tm, row0=0, rows=None, block_stride=1):
    d = x.shape[1]
    m = x.shape[0] if rows is None else rows
    return pl.pallas_call(
        _rmsnorm_kernel,
        grid=(m // tm,),
        in_specs=[pl.BlockSpec((tm, d), lambda i: (i * block_stride + row0 // tm, 0)),
                  pl.BlockSpec((1, d), lambda i: (0, 0))],
        out_specs=pl.BlockSpec((tm, d), lambda i: (i, 0)),
        out_shape=jax.ShapeDtypeStruct((m, d), out_dtype),
        compiler_params=_cparams(("parallel",), 40),
        name="rmsnorm",
    )(x, g.reshape(1, d))


WD_CAST_ROWS = 128


def _gate_up_kernel(h_ref, wg_ref, wu_ref, wd_ref, o_ref, wd_bf16_ref, *, cast_passes):
    h = h_ref[...]
    a = jnp.dot(h, wg_ref[...].astype(BF16), preferred_element_type=F32)
    b = jnp.dot(h, wu_ref[...].astype(BF16), preferred_element_type=F32)
    o_ref[...] = (a * jax.nn.sigmoid(a) * b).astype(o_ref.dtype)

    @pl.when(pl.program_id(0) < cast_passes)
    def _():
        wd_bf16_ref[...] = wd_ref[...].astype(BF16)


def _gate_up(h, wg, wu, wd, layer, tm=2176, tn=256):
    m, d = h.shape
    f = wg.shape[2]
    n_i, n_j = m // tm, f // tn
    cast_blocks = f // WD_CAST_ROWS
    cast_passes = cast_blocks // n_j
    assert cast_passes * n_j == cast_blocks and cast_passes <= n_i

    def cast_block(i, j):
        return jnp.where(i < cast_passes, i * n_j + j, cast_blocks - 1)

    w_spec = pl.BlockSpec((None, d, tn), lambda i, j: (layer, 0, j))
    return pl.pallas_call(
        functools.partial(_gate_up_kernel, cast_passes=cast_passes),
        grid=(n_i, n_j),
        in_specs=[pl.BlockSpec((tm, d), lambda i, j: (i, 0), pipeline_mode=pl.Buffered(1)), w_spec, w_spec,
                  pl.BlockSpec((None, WD_CAST_ROWS, d), lambda i, j: (layer, cast_block(i, j), 0))],
        out_specs=[pl.BlockSpec((tm, tn), lambda i, j: (i, j)),
                   pl.BlockSpec((WD_CAST_ROWS, d), lambda i, j: (cast_block(i, j), 0))],
        out_shape=[jax.ShapeDtypeStruct((m, f), BF16), jax.ShapeDtypeStruct((f, d), BF16)],
        compiler_params=_cparams(("arbitrary", "arbitrary"), 58),
        name="ffn_gate_up",
    )(h, wg, wu, wd)


def _res_matmul_kernel(*refs, n_pairs, scale, cast_w):
    a_refs = refs[:n_pairs]
    w_refs = refs[n_pairs:2 * n_pairs]
    x_ref, o_ref = refs[2 * n_pairs], refs[2 * n_pairs + 1]
    load_w = (lambda ref: ref[...].astype(BF16)) if cast_w else (lambda ref: ref[...])
    acc = jnp.dot(a_refs[0][...], load_w(w_refs[0]), preferred_element_type=F32)
    for a_ref, w_ref in zip(a_refs[1:], w_refs[1:]):
        acc = acc + jnp.dot(a_ref[...], load_w(w_ref), preferred_element_type=F32)
    if scale != 1.0:
        acc = scale * acc
    o_ref[...] = x_ref[...] + acc


def _res_matmul(a_list, w, layer, x, scale, tm, tn, vmem_mib, cast_w=False):
    m, n = x.shape
    n_pairs = len(a_list)
    in_specs = [pl.BlockSpec((tm, a.shape[1]), lambda i, j: (i, 0)) for a in a_list]
    for p, a in enumerate(a_list):
        if layer is None:
            in_specs.append(pl.BlockSpec((a.shape[1], tn), lambda i, j, p=p: (p, j)))
        else:
            in_specs.append(pl.BlockSpec((None, a.shape[1], tn), lambda i, j, p=p: (layer, p, j)))
    in_specs.append(pl.BlockSpec((tm, tn), lambda i, j: (i, j)))
    assert sum(a.shape[1] for a in a_list) == w.shape[-2]
    return pl.pallas_call(
        functools.partial(_res_matmul_kernel, n_pairs=n_pairs, scale=scale, cast_w=cast_w),
        grid=(m // tm, n // tn),
        in_specs=in_specs,
        out_specs=pl.BlockSpec((tm, tn), lambda i, j: (i, j)),
        out_shape=jax.ShapeDtypeStruct((m, n), F32),
        compiler_params=_cparams(("parallel", "arbitrary"), vmem_mib),
        name="res_matmul",
    )(*a_list, *([w] * n_pairs), x)


def _ffn(x, norm_g, wg, wu, wd, layer):
    h = _rmsnorm(x, norm_g, BF16, ROW_TM)
    act, wd_bf16 = _gate_up(h, wg, wu, wd, layer)
    return _res_matmul([act], wd_bf16, None, x, 0.5, tm=544, tn=512, vmem_mib=58)


WIN_TN = 512
WIN_MAIN_TILES = COL_LORA // WIN_TN
WIN_SHIFT_TILES = COL_S5 // WIN_TN


def _win_tile_cases(h_ref, wm_ref, wt_ref, j, emit):
    @pl.when(j < WIN_MAIN_TILES)
    def _():
        emit(jnp.dot(h_ref[...], wm_ref[...], preferred_element_type=F32), True)

    @pl.when((j >= WIN_MAIN_TILES) & (j < WIN_SHIFT_TILES))
    def _():
        emit(jnp.dot(h_ref[...], wt_ref[...], preferred_element_type=F32), True)

    @pl.when(j >= WIN_SHIFT_TILES)
    def _():
        emit(jnp.dot(h_ref[...], wt_ref[...], preferred_element_type=F32), False)


def _win_prompt_kernel(h_ref, wm_ref, wt_ref, mu_ref, o_ref):
    def emit(p, shifted):
        if shifted:
            rows = lax.broadcasted_iota(jnp.int32, p.shape, 0)
            prev = jnp.where(rows == 0, 0.0, pltpu.roll(p, 1, 0))
            o_ref[...] = p + (prev - p) * mu_ref[...]
        else:
            o_ref[...] = p

    _win_tile_cases(h_ref, wm_ref, wt_ref, pl.program_id(1), emit)


def _win_sample_kernel(h_ref, wm_ref, wt_ref, mu_ref, o_ref):
    def emit(p, shifted):
        cur = p[DEC_BATCH:]
        o_ref[...] = cur + (p[:M_SAMPLE] - cur) * mu_ref[...] if shifted else cur

    _win_tile_cases(h_ref, wm_ref, wt_ref, pl.program_id(0), emit)


def _win_weight_specs(layer, col_of):
    main = pl.BlockSpec((None, D_MODEL, WIN_TN),
                        lambda *g: (layer, 0, jnp.minimum(col_of(*g), WIN_MAIN_TILES - 1)))
    tail = pl.BlockSpec((None, D_MODEL, WIN_TN),
                        lambda *g: (layer, 0, jnp.maximum(col_of(*g) - WIN_MAIN_TILES, 0)))
    mu = pl.BlockSpec((None, 1, WIN_TN), lambda *g: (layer, 0, col_of(*g)))
    return [main, tail, mu]


def _win_prompt(h, w_main, w_tail, mu_pad, layer):
    tn = WIN_TN
    return pl.pallas_call(
        _win_prompt_kernel,
        grid=(BATCH, D_INP // tn),
        in_specs=[pl.BlockSpec((SEQ, D_MODEL), lambda b, j: (b, 0), pipeline_mode=pl.Buffered(1))]
                 + _win_weight_specs(layer, lambda b, j: j),
        out_specs=pl.BlockSpec((SEQ, tn), lambda b, j: (b, j)),
        out_shape=jax.ShapeDtypeStruct((M_PROMPT, D_INP), F32),
        compiler_params=_cparams(("parallel", "arbitrary"), 60),
        name="in_proj_prompt",
    )(h, w_main, w_tail, mu_pad)


def _win_sample(h_rows, w_main, w_tail, mu_pad, layer):
    tn = WIN_TN
    rows = h_rows.shape[0]
    return pl.pallas_call(
        _win_sample_kernel,
        grid=(D_INP // tn,),
        in_specs=[pl.BlockSpec((rows, D_MODEL), lambda j: (0, 0))] + _win_weight_specs(layer, lambda j: j),
        out_specs=pl.BlockSpec((M_SAMPLE, tn), lambda j: (0, j)),
        out_shape=jax.ShapeDtypeStruct((M_SAMPLE, D_INP), F32),
        compiler_params=_cparams(("arbitrary",), 48),
        name="in_proj_sample",
    )(h_rows, w_main, w_tail, mu_pad)


def _in_proj_tail(w_in, mu):
    o_w, o_a, o_g = COL_LORA, COL_LORA + W_LORA, COL_LORA + W_LORA + A_LORA
    o_s = o_g + G_LORA
    w_main = w_in.astype(BF16)
    zw = jnp.zeros((DEPTH, D_MODEL, LORA_PAD - W_LORA), BF16)
    w_tail = jnp.concatenate([w_main[:, :, o_w:o_a], zw, w_main[:, :, o_a:o_g], zw,
                              w_main[:, :, o_g:]], axis=2)
    zm = jnp.zeros((DEPTH, LORA_PAD - W_LORA), mu.dtype)
    mu_pad = jnp.concatenate([mu[:, :o_w], mu[:, o_w:o_a], zm, mu[:, o_a:o_g], zm, mu[:, o_g:o_s],
                              jnp.zeros((DEPTH, S5W), mu.dtype)], axis=1).reshape(DEPTH, 1, D_INP)
    return w_main, w_tail, mu_pad


def _head_sum(x):
    r = lax.broadcasted_iota(jnp.int32, (LANES, LANES), 0) // HEAD
    c = lax.broadcasted_iota(jnp.int32, (LANES, LANES), 1) // HEAD
    ones = (r == c).astype(BF16)
    hi = x.astype(BF16)
    r1 = x - hi.astype(F32)
    mid = r1.astype(BF16)
    lo = (r1 - mid.astype(F32)).astype(BF16)
    outs = []
    for s in range(x.shape[1] // LANES):
        sl = slice(s * LANES, (s + 1) * LANES)
        acc = jnp.dot(hi[:, sl], ones, preferred_element_type=F32)
        acc = acc + jnp.dot(mid[:, sl], ones, preferred_element_type=F32)
        acc = acc + jnp.dot(lo[:, sl], ones, preferred_element_type=F32)
        outs.append(acc)
    return jnp.concatenate(outs, axis=1)


def _rwkv_pre_kernel(r_ref, k_ref, v_ref, lora_ref, w0_ref, a0_ref, kk_ref, ka_ref, rk_ref,
                     wup_ref, aup_ref, gup_ref, *outs, channel_major):
    *scan_outs, g_out, bonus_out = outs
    r_out, w_out, k_out, v_out, kk_out, b_out = range(6)

    def emit(idx, val):
        if channel_major:
            scan_outs[idx][...] = val.T
        else:
            scan_outs[0][idx] = val

    r = r_ref[...]
    k = k_ref[...]
    v = v_ref[...]
    emit(r_out, r)
    emit(v_out, v)
    lora = lora_ref[...]
    wd = lora[:, :LORA_PAD]
    ad = lora[:, LORA_PAD:2 * LORA_PAD]
    gd = lora[:, 2 * LORA_PAD:]
    z = -(w0_ref[...] + jnp.dot(jnp.tanh(wd).astype(BF16), wup_ref[...], preferred_element_type=F32))
    softplus = jnp.maximum(z, 0.0) + jnp.log1p(jnp.exp(-jnp.abs(z)))
    emit(w_out, jnp.exp(-jnp.exp(-softplus - 0.5)))
    a = jax.nn.sigmoid(a0_ref[...] + jnp.dot(ad.astype(BF16), aup_ref[...], preferred_element_type=F32))
    g_out[...] = jnp.dot(jax.nn.sigmoid(gd).astype(BF16), gup_ref[...], preferred_element_type=F32)
    kk = k * kk_ref[...]
    k2 = k * (1.0 + (a - 1.0) * ka_ref[...])
    emit(k_out, k2)
    kkn = kk / jnp.maximum(jnp.sqrt(_head_sum(kk * kk)), 1e-12)
    emit(kk_out, kkn)
    emit(b_out, kkn * a)
    bonus_out[...] = _head_sum(r * k2 * rk_ref[...]) * v


def _time_major_spec(m, tm, time_major):
    if not time_major:
        return pl.BlockSpec((tm, RW), lambda i: (i, 0)), (m, RW)
    per_seq = SEQ // tm
    return pl.BlockSpec((tm, RW), lambda i: (i % per_seq, i // per_seq)), (SEQ, (m // SEQ) * RW)


def _rwkv_pre(q, w0, a0, k_k, k_a, r_k, w_up, a_up, g_up, time_major, tm=ROW_TM):
    m = q.shape[0]
    out_spec, out_dims = _time_major_spec(m, tm, time_major)
    if time_major:
        per_seq = SEQ // tm
        scan_specs = [pl.BlockSpec((RW, tm), lambda i: (i // per_seq, i % per_seq))] * 6
        scan_outs = [jax.ShapeDtypeStruct(((m // SEQ) * RW, SEQ), F32)] * 6
    else:
        scan_specs = [pl.BlockSpec((6, tm, RW), lambda i: (0, i, 0))]
        scan_outs = [jax.ShapeDtypeStruct((6, m, RW), F32)]
    row = lambda v: v.reshape(1, RW)
    pad_rows = lambda u: jnp.concatenate(
        [u, jnp.zeros((LORA_PAD - u.shape[0], RW), u.dtype)], axis=0).astype(BF16)
    vec = pl.BlockSpec((1, RW), lambda i: (0, 0))
    full = lambda rows: pl.BlockSpec((rows, RW), lambda i: (0, 0))
    out = jax.ShapeDtypeStruct(out_dims, F32)
    return pl.pallas_call(
        functools.partial(_rwkv_pre_kernel, channel_major=time_major),
        grid=(m // tm,),
        in_specs=[pl.BlockSpec((tm, RW), lambda i: (i, 0)),
                  pl.BlockSpec((tm, RW), lambda i: (i, 1)),
                  pl.BlockSpec((tm, RW), lambda i: (i, 2)),
                  pl.BlockSpec((tm, LORA_W), lambda i: (i, COL_LORA // LORA_W)),
                  vec, vec, vec, vec, vec, full(LORA_PAD), full(LORA_PAD), full(G_LORA)],
        out_specs=scan_specs + [out_spec] * 2,
        out_shape=scan_outs + [out] * 2,
        compiler_params=_cparams(("parallel",), 56),
        name="rwkv_pre",
    )(q, q, q, q, row(w0), row(a0), row(k_k), row(k_a), row(r_k.reshape(RW)),
      pad_rows(w_up), pad_rows(a_up), g_up.astype(BF16))


def _rwkv_post_kernel(y_ref, bonus_ref, g_ref, lw_ref, lb_ref, *rest):
    o_ref = rest[-1]
    y = y_ref[...]
    mean = _head_sum(y) * (1.0 / HEAD)
    c = y - mean
    var = _head_sum(c * c) * (1.0 / HEAD)
    yn = c * lax.rsqrt(var + EPS_GN) * lw_ref[...] + lb_ref[...]
    o_ref[...] = ((yn + bonus_ref[...]) * g_ref[...]).astype(o_ref.dtype)


def _rwkv_post(y, bonus, g, lnx_w, lnx_b, row0, time_major, joined=None, tm=ROW_TM):
    m = y.size // RW
    blk, _ = _time_major_spec(m, tm, time_major)
    vec = pl.BlockSpec((1, RW), lambda i: (0, 0))
    in_specs, args, aliases = _joined_out(
        [blk, blk, blk, vec, vec], [y, bonus, g, lnx_w.reshape(1, RW), lnx_b.reshape(1, RW)], joined)
    return pl.pallas_call(
        _rwkv_post_kernel,
        grid=(m // tm,),
        in_specs=in_specs,
        out_specs=pl.BlockSpec((tm, RW), lambda i: (i + row0 // tm, 0)),
        out_shape=jax.ShapeDtypeStruct((M_ALL, RW), BF16),
        input_output_aliases=aliases,
        compiler_params=_cparams(("parallel",), 40),
        name="rwkv_post",
    )(*args)


WKV_UNROLL = 8
WKV_CHUNK = 64


def _wkv_kernel(r_ref, w_ref, k_ref, v_ref, kk_ref, b_ref, s0_ref, after_ref, y_ref, s_ref,
                g_ref, kq_ref, wr_ref, bt_ref, kt_ref, *, steps):
    @pl.when(pl.program_id(1) == 0)
    def _():
        s_ref[...] = s0_ref[...]

    g_ref[...] = jnp.ones((HEAD, LANES), F32)

    def step(t, carry):
        r, k, b = r_ref[t], k_ref[t], b_ref[t]
        g_prev = g_ref[...]
        g = g_prev * w_ref[t]
        g_inv = 1.0 / g
        g_ref[...] = g
        kq_ref[...] = g_prev * kk_ref[t]
        wr_ref[...] = g * r
        bt_ref[...] = b * g_inv
        kt_ref[...] = k * g_inv
        beta = jnp.sum(b * r, axis=0, keepdims=True)
        kappa = jnp.sum(k * r, axis=0, keepdims=True)
        v = v_ref[t]

        def contract(j, acc):
            sa, u = acc
            sj = s_ref[j]
            return sa - sj * kq_ref[pl.ds(j, 1), :], u + sj * wr_ref[pl.ds(j, 1), :]

        zero = jnp.zeros((HEAD, LANES), F32)
        sa, u = lax.fori_loop(0, HEAD, contract, (zero, zero), unroll=WKV_UNROLL)

        def update(j, c):
            s_ref[j] = s_ref[j] + sa * bt_ref[pl.ds(j, 1), :] + v * kt_ref[pl.ds(j, 1), :]
            return c

        lax.fori_loop(0, HEAD, update, 0, unroll=WKV_UNROLL)
        y_ref[t] = u + sa * beta + v * kappa
        return carry

    lax.fori_loop(0, steps, step, 0)

    def denormalise(j, c):
        s_ref[j] = s_ref[j] * g_ref[pl.ds(j, 1), :]
        return c

    lax.fori_loop(0, HEAD, denormalise, 0, unroll=WKV_UNROLL)


def _wkv_scan(vectors, s0, after, tc):
    vec = pl.BlockSpec((tc, HEAD, LANES), lambda c, t: (t, 0, c))
    if isinstance(vectors, (list, tuple)):
        r, w, k, v, kk, b = vectors
        vec_specs = [vec] * 6
    else:
        r = w = k = v = kk = b = vectors
        vec_specs = [pl.BlockSpec((None, tc, HEAD, LANES), lambda c, t, i=i: (i, t, 0, c)) for i in range(6)]
    length, _, n = r.shape[-3:]
    st = pl.BlockSpec((HEAD, HEAD, LANES), lambda c, t: (0, 0, c))
    return pl.pallas_call(
        functools.partial(_wkv_kernel, steps=tc),
        grid=(n // LANES, length // tc),
        in_specs=vec_specs + [st, pl.BlockSpec(memory_space=pl.ANY)],
        out_specs=[vec, st],
        out_shape=[jax.ShapeDtypeStruct((length, HEAD, n), F32),
                   jax.ShapeDtypeStruct((HEAD, HEAD, n), F32)],
        scratch_shapes=[pltpu.VMEM((HEAD, LANES), F32)] * 5,
        compiler_params=_cparams(("parallel", "arbitrary"), 40),
        name="wkv_scan",
    )(r, w, k, v, kk, b, s0, after)


def _to_scan(a, length):
    return a.reshape(length, -1, HEAD).transpose(0, 2, 1)


def _to_scan_channel_major(a):
    return a.reshape(-1, HEAD, a.shape[1]).transpose(2, 1, 0)


def _from_scan(y, length):
    return y.transpose(0, 2, 1).reshape(length, -1)


def _state_to_scan(s):
    n = s.shape[0] * s.shape[1]
    s = lax.optimization_barrier(s.reshape(n, HEAD, HEAD).transpose(0, 2, 1))
    return s.reshape(n, HEAD * HEAD).T.reshape(HEAD, HEAD, n)


def _state_from_scan(s, batch):
    n = s.shape[2]
    s = lax.optimization_barrier(s.reshape(HEAD * HEAD, n).T)
    return s.reshape(n, HEAD, HEAD).transpose(0, 2, 1).reshape(batch, HEADS, HEAD, HEAD)


def _s5_discretize(lre_ref, lim_ref, ls_ref, bre_ref, bim_ref):
    lre, lim = lre_ref[...], lim_ref[...]
    step = jnp.exp(ls_ref[...])
    mag = jnp.exp(lre * step)
    ar = mag * jnp.cos(lim * step)
    ai = mag * jnp.sin(lim * step)
    den = lre * lre + lim * lim
    nr = ar - 1.0
    f_re = (nr * lre + ai * lim) / den
    f_im = (ai * lre - nr * lim) / den
    bre, bim = bre_ref[...], bim_ref[...]
    return ar, ai, f_re * bre - f_im * bim, f_re * bim + f_im * bre


def _dot_bf16(a, b):
    return jnp.dot(a.astype(BF16), b.astype(BF16), preferred_element_type=F32)


def _s5_readout(hr, hi, cre_ref, cim_ref, d_ref, u):
    return _dot_bf16(hr, cre_ref[...]) - _dot_bf16(hi, cim_ref[...]) + d_ref[...] * u


def _cmul(ar, ai, br, bi):
    return ar * br - ai * bi, ar * bi + ai * br


def _s5_prompt_kernel(u_ref, lre_ref, lim_ref, ls_ref, bre_ref, bim_ref, cre_ref, cim_ref, d_ref,
                      y_ref, hre_ref, him_ref, hr_scr, hi_scr):
    ar, ai, bbr, bbi = _s5_discretize(lre_ref, lim_ref, ls_ref, bre_ref, bim_ref)
    u = u_ref[...]
    hr_scr[...] = _dot_bf16(u, bbr)
    hi_scr[...] = _dot_bf16(u, bbi)

    rows = lax.broadcasted_iota(jnp.int32, (SUBLANES, S5_HW), 0)
    powers = [(ar, ai)]
    for _ in range(SUBLANES - 1):
        powers.append(_cmul(*powers[-1], ar, ai))
    levels = []
    for sh in (1, 2, 4):
        pr, pi = powers[sh - 1]
        levels.append((sh, jnp.where(rows >= sh, pr, 0.0), jnp.where(rows >= sh, pi, 0.0)))
    cpr = jnp.zeros((SUBLANES, S5_HW), F32)
    cpi = jnp.zeros((SUBLANES, S5_HW), F32)
    for n, (pr, pi) in enumerate(powers):
        cpr = jnp.where(rows == n, pr, cpr)
        cpi = jnp.where(rows == n, pi, cpi)

    def tile(i, carry):
        cr, ci = carry
        r0 = pl.multiple_of(i * SUBLANES, SUBLANES)
        xr = hr_scr[pl.ds(r0, SUBLANES), :]
        xi = hi_scr[pl.ds(r0, SUBLANES), :]
        for sh, mr, mi in levels:
            sr, si = pltpu.roll(xr, sh, 0), pltpu.roll(xi, sh, 0)
            xr, xi = xr + (mr * sr - mi * si), xi + (mr * si + mi * sr)
        xr = xr + (cpr * cr - cpi * ci)
        xi = xi + (cpr * ci + cpi * cr)
        hr_scr[pl.ds(r0, SUBLANES), :] = xr
        hi_scr[pl.ds(r0, SUBLANES), :] = xi
        last = SUBLANES - 1
        return (jnp.broadcast_to(xr[last:, :], (SUBLANES, S5_HW)),
                jnp.broadcast_to(xi[last:, :], (SUBLANES, S5_HW)))

    zero = jnp.zeros((SUBLANES, S5_HW), F32)
    cr, ci = lax.fori_loop(0, SEQ // SUBLANES, tile, (zero, zero), unroll=2)
    hre_ref[...] = cr[:1]
    him_ref[...] = ci[:1]
    y_ref[...] = _s5_readout(hr_scr[...], hi_scr[...], cre_ref, cim_ref, d_ref, u)


def _s5_sample_kernel(u_ref, h0r_ref, h0i_ref, lre_ref, lim_ref, ls_ref, bre_ref, bim_ref, cre_ref, cim_ref,
                      d_ref, y_ref, hre_ref, him_ref, hr_scr, hi_scr):
    ar, ai, bbr, bbi = _s5_discretize(lre_ref, lim_ref, ls_ref, bre_ref, bim_ref)
    u = u_ref[...]
    bu_r = _dot_bf16(u, bbr)
    bu_i = _dot_bf16(u, bbi)
    hr, hi = h0r_ref[...], h0i_ref[...]
    for t in range(DEC_SEQ):
        rows = slice(t * DEC_BATCH, (t + 1) * DEC_BATCH)
        hr, hi = ar * hr - ai * hi + bu_r[rows], ar * hi + ai * hr + bu_i[rows]
        hr_scr[rows, :] = hr
        hi_scr[rows, :] = hi
    hre_ref[...] = hr
    him_ref[...] = hi
    y_ref[...] = _s5_readout(hr_scr[...], hi_scr[...], cre_ref, cim_ref, d_ref, u)


def _s5_params(lam_re, lam_im, log_step, b_re, b_im, c_re, c_im, d):
    eye = jnp.eye(S5_GT, dtype=F32)
    tiles = DEPTH * S5_TILES
    chan = lambda a: a.reshape(tiles, 1, S5_HW)

    def b_blk(b):
        x = b.transpose(0, 1, 3, 2).reshape(tiles, S5_GT, S5_C, S5_P)
        return jnp.einsum('tgcp,gh->tgchp', x, eye).reshape(tiles, S5_UW, S5_HW)

    def c_blk(c):
        x = c.transpose(0, 1, 3, 2).reshape(tiles, S5_GT, S5_P, S5_C)
        return jnp.einsum('tgpc,gh->tgphc', x, eye).reshape(tiles, S5_HW, S5_UW)

    ls = jnp.broadcast_to(log_step[:, :, None], (DEPTH, S5_G, S5_P))
    return (chan(lam_re), chan(lam_im), chan(ls), b_blk(b_re), b_blk(b_im), c_blk(c_re), c_blk(c_im),
            d.reshape(tiles, 1, S5_UW))


def _s5_param_specs(tile_of):
    chan = pl.BlockSpec((None, 1, S5_HW), lambda *g: (tile_of(*g), 0, 0))
    bspec = pl.BlockSpec((None, S5_UW, S5_HW), lambda *g: (tile_of(*g), 0, 0))
    cspec = pl.BlockSpec((None, S5_HW, S5_UW), lambda *g: (tile_of(*g), 0, 0))
    dspec = pl.BlockSpec((None, 1, S5_UW), lambda *g: (tile_of(*g), 0, 0))
    return [chan, chan, chan, bspec, bspec, cspec, cspec, dspec]


def _s5_prompt(q, params, layer):
    col0 = COL_S5 // S5_UW
    state = jax.ShapeDtypeStruct((BATCH, 1, S5_G * S5_P), F32)
    st_spec = pl.BlockSpec((None, 1, S5_HW), lambda b, g: (b, 0, g))
    return pl.pallas_call(
        _s5_prompt_kernel,
        grid=(BATCH, S5_TILES),
        in_specs=[pl.BlockSpec((SEQ, S5_UW), lambda b, g: (b, col0 + g))]
                 + _s5_param_specs(lambda b, g: layer * S5_TILES + g),
        out_specs=[pl.BlockSpec((SEQ, S5_UW), lambda b, g: (b, g)), st_spec, st_spec],
        out_shape=[jax.ShapeDtypeStruct((M_PROMPT, S5W), F32), state, state],
        scratch_shapes=[pltpu.VMEM((SEQ, S5_HW), F32), pltpu.VMEM((SEQ, S5_HW), F32)],
        compiler_params=_cparams(("parallel", "parallel"), 48),
        name="s5_prompt",
    )(q, *params)


def _s5_sample(q, h0_re, h0_im, params, layer):
    col0 = COL_S5 // S5_UW
    state = jax.ShapeDtypeStruct((DEC_BATCH, S5_G * S5_P), F32)
    st_spec = pl.BlockSpec((DEC_BATCH, S5_HW), lambda g: (0, g))
    return pl.pallas_call(
        _s5_sample_kernel,
        grid=(S5_TILES,),
        in_specs=[pl.BlockSpec((M_SAMPLE, S5_UW), lambda g: (0, col0 + g)), st_spec, st_spec]
                 + _s5_param_specs(lambda g: layer * S5_TILES + g),
        out_specs=[pl.BlockSpec((M_SAMPLE, S5_UW), lambda g: (0, g)), st_spec, st_spec],
        out_shape=[jax.ShapeDtypeStruct((M_SAMPLE, S5W), F32), state, state],
        scratch_shapes=[pltpu.VMEM((M_SAMPLE, S5_HW), F32), pltpu.VMEM((M_SAMPLE, S5_HW), F32)],
        compiler_params=_cparams(("parallel",), 32),
        name="s5_sample",
    )(q, h0_re, h0_im, *params)


def _glu_norm_kernel(y_ref, w_ref, b_ref, g_ref, *rest):
    o_ref = rest[-1]
    y = y_ref[...]
    z = 0.5 * y * (1.0 + lax.erf(y * math.sqrt(0.5)))
    gate = jax.nn.sigmoid(jnp.dot(z.astype(BF16), w_ref[...], preferred_element_type=F32) + b_ref[...])
    out = z * gate
    ms = jnp.mean(out * out, axis=-1, keepdims=True)
    o_ref[...] = (out * lax.rsqrt(ms + EPS_RMS) * g_ref[...]).astype(o_ref.dtype)


def _glu_norm(y, w_glu_bf16, layer, b_glu, gain, row0, joined=None, tm=ROW_TM):
    m = y.shape[0]
    vec = pl.BlockSpec((1, S5W), lambda i: (0, 0))
    in_specs, args, aliases = _joined_out(
        [pl.BlockSpec((tm, S5W), lambda i: (i, 0)),
         pl.BlockSpec((None, S5W, S5W), lambda i: (layer, 0, 0)), vec, vec],
        [y, w_glu_bf16, b_glu.reshape(1, S5W), gain.reshape(1, S5W)], joined)
    return pl.pallas_call(
        _glu_norm_kernel,
        grid=(m // tm,),
        in_specs=in_specs,
        out_specs=pl.BlockSpec((tm, S5W), lambda i: (i + row0 // tm, 0)),
        out_shape=jax.ShapeDtypeStruct((M_ALL, S5W), BF16),
        input_output_aliases=aliases,
        compiler_params=_cparams(("parallel",), 48),
        name="s5_glu_norm",
    )(*args)


def kernel(x_prompt, x_sample, state_shift, state_wkv, state_ssm_re, state_ssm_im, ffn1_norm, ffn1_w_gate, ffn1_w_up, ffn1_w_down, mix_norm, w_in, shift_mu, rw_w0, rw_w_up, rw_a0, rw_a_up, rw_g_up, rw_k_k, rw_k_a, rw_r_k, rw_lnx_w, rw_lnx_b, s5_lam_re, s5_lam_im, s5_b_re, s5_b_im, s5_c_re, s5_c_im, s5_d, s5_log_step, s5_w_glu, s5_b_glu, s5_out_norm, w_out, ffn2_norm, ffn2_w_gate, ffn2_w_up, ffn2_w_down, final_norm):
    x = jnp.concatenate([x_prompt.reshape(M_PROMPT, D_MODEL),
                         x_sample.transpose(1, 0, 2).reshape(M_SAMPLE, D_MODEL)], axis=0)
    zero_state = jnp.zeros((HEAD, HEAD, BATCH * HEADS), F32)
    shift_p, wkv_p, re_p, im_p, shift_s, wkv_s, re_s, im_s = ([] for _ in range(8))
    w_glu = s5_w_glu.astype(BF16)
    s5p = _s5_params(s5_lam_re, s5_lam_im, s5_log_step, s5_b_re, s5_b_im, s5_c_re, s5_c_im, s5_d)
    w_main, w_tail, mu_pad = _in_proj_tail(w_in, shift_mu)

    for l in range(DEPTH):
        x = _ffn(x, ffn1_norm[l], ffn1_w_gate, ffn1_w_up, ffn1_w_down, l)

        h_mix = _rmsnorm(x, mix_norm[l], BF16, ROW_TM)
        last8 = _rmsnorm(x, mix_norm[l], F32, SUBLANES, SEQ - SUBLANES, BATCH * SUBLANES, SEQ // SUBLANES)
        shift_p.append(last8[SUBLANES - 1::SUBLANES])
        shift_s.append(_rmsnorm(x, mix_norm[l], F32, DEC_BATCH, M_ALL - DEC_BATCH, DEC_BATCH))

        q_p = _win_prompt(h_mix, w_main, w_tail, mu_pad, l)
        q_s = _win_sample(jnp.concatenate([state_shift[l].astype(BF16), h_mix[M_PROMPT:]], axis=0),
                          w_main, w_tail, mu_pad, l)

        rw_args = (rw_w0[l], rw_a0[l], rw_k_k[l], rw_k_a[l], rw_r_k[l], rw_w_up[l], rw_a_up[l], rw_g_up[l])
        *scan_p, gate_p, bonus_p = _rwkv_pre(q_p, *rw_args, time_major=True, tm=LANES)
        scan_s, gate_s, bonus_s = _rwkv_pre(q_s, *rw_args, time_major=False)
        scan_p = [_to_scan_channel_major(a) for a in scan_p]
        scan_s = _to_scan(scan_s, 6 * DEC_SEQ).reshape(6, DEC_SEQ, HEAD, DEC_BATCH * HEADS)

        y5_p, hre_p, him_p = _s5_prompt(q_p, s5p, l)
        y5_s, hre_s, him_s = _s5_sample(q_s, state_ssm_re[l].reshape(DEC_BATCH, S5_G * S5_P),
                                        state_ssm_im[l].reshape(DEC_BATCH, S5_G * S5_P), s5p, l)
        re_p.append(hre_p.reshape(BATCH, S5_G, S5_P))
        im_p.append(him_p.reshape(BATCH, S5_G, S5_P))
        re_s.append(hre_s.reshape(DEC_BATCH, S5_G, S5_P))
        im_s.append(him_s.reshape(DEC_BATCH, S5_G, S5_P))
        y_s5 = _glu_norm(y5_p, w_glu, l, s5_b_glu[l], s5_out_norm[l], 0)
        y_s5 = _glu_norm(y5_s, w_glu, l, s5_b_glu[l], s5_out_norm[l], M_PROMPT, y_s5)

        y_p, s_p = _wkv_scan(scan_p, zero_state, y_s5, tc=WKV_CHUNK)
        y_s, s_s = _wkv_scan(scan_s, _state_to_scan(state_wkv[l]), y_s5, tc=DEC_SEQ)
        wkv_p.append(_state_from_scan(s_p, BATCH))
        wkv_s.append(_state_from_scan(s_s, DEC_BATCH))
        y_rw = _rwkv_post(_from_scan(y_p, SEQ), bonus_p, gate_p, rw_lnx_w[l], rw_lnx_b[l], 0, True)
        y_rw = _rwkv_post(_from_scan(y_s, DEC_SEQ).reshape(M_SAMPLE, RW), bonus_s, gate_s,
                          rw_lnx_w[l], rw_lnx_b[l], M_PROMPT, False, y_rw)

        x = _res_matmul([y_rw, y_s5], w_out, l, x, 1.0, tm=1088, tn=512, vmem_mib=56, cast_w=True)
        x = _ffn(x, ffn2_norm[l], ffn2_w_gate, ffn2_w_up, ffn2_w_down, l)

    y_prompt = _rmsnorm(x, final_norm, F32, ROW_TM, 0, M_PROMPT).reshape(BATCH, SEQ, D_MODEL)
    y_sample = _rmsnorm(x, final_norm, F32, ROW_TM, M_PROMPT, M_SAMPLE).reshape(
        DEC_SEQ, DEC_BATCH, D_MODEL).transpose(1, 0, 2)
    st = jnp.stack
    return (y_prompt, y_sample, st(shift_p), st(wkv_p), st(re_p), st(im_p),
            st(shift_s), st(wkv_s), st(re_s), st(im_s))
```

```python
import functools
import math

import jax
import jax.numpy as jnp
from jax import lax
from jax.experimental import pallas as pl
from jax.experimental.pallas import tpu as pltpu

F32 = jnp.float32
BF16 = jnp.bfloat16

D_MODEL = 4096
BATCH = 4
SEQ = 2048
DEPTH = 2
DEC_BATCH = 128
DEC_SEQ = 4
M_PROMPT = BATCH * SEQ
M_SAMPLE = DEC_BATCH * DEC_SEQ
M_ALL = M_PROMPT + M_SAMPLE

RW = D_MODEL // 2
HEAD = 64
HEADS = RW // HEAD
S5W = D_MODEL - RW
S5_C = 16
S5_G = S5W // S5_C
S5_P = 64
W_LORA = 96
A_LORA = 96
G_LORA = 256
LORA_PAD = 128
LORA_W = 2 * LORA_PAD + G_LORA
D_FF = 11008
EPS_RMS = 1e-6
EPS_GN = 64e-5

COL_LORA = 3 * RW
COL_S5 = COL_LORA + LORA_W
D_INP = COL_S5 + S5W

V7X_VMEM_BYTES = 64 * 1024 * 1024
LANES = 128
SUBLANES = 8

S5_GT = 8
S5_TILES = S5_G // S5_GT
S5_UW = S5_GT * S5_C
S5_HW = S5_GT * S5_P

ROW_TM = 512
PRE_TM = 256


def _cparams(semantics, vmem_mib):
    assert vmem_mib * 1024 * 1024 < V7X_VMEM_BYTES
    return pltpu.CompilerParams(dimension_semantics=semantics,
                                vmem_limit_bytes=vmem_mib * 1024 * 1024)


def _joined_out(in_specs, args, joined):
    if joined is None:
        return in_specs, args, {}
    return (in_specs + [pl.BlockSpec(memory_space=pl.ANY)], args + [joined], {len(args): 0})


def _rmsnorm_kernel(x_ref, g_ref, o_ref):
    x = x_ref[...]
    ms = jnp.mean(x * x, axis=-1, keepdims=True)
    o_ref[...] = (x * lax.rsqrt(ms + EPS_RMS) * g_ref[...]).astype(o_ref.dtype)


def _rmsnorm(x, g, out_dtype, tm, row0=0, rows=None, block_stride=1):
    d = x.shape[1]
    m = x.shape[0] if rows is None else rows
    return pl.pallas_call(
        _rmsnorm_kernel,
        grid=(m // tm,),
        in_specs=[pl.BlockSpec((tm, d), lambda i: (i * block_stride + row0 // tm, 0)),
                  pl.BlockSpec((1, d), lambda i: (0, 0))],
        out_specs=pl.BlockSpec((tm, d), lambda i: (i, 0)),
        out_shape=jax.ShapeDtypeStruct((m, d), out_dtype),
        compiler_params=_cparams(("parallel",), 40),
        name="rmsnorm",
    )(x, g.reshape(1, d))


WD_CAST_ROWS = 128


def _gate_up_kernel(h_ref, wg_ref, wu_ref, wd_ref, o_ref, wd_bf16_ref, *, cast_passes):
    h = h_ref[...]
    a = jnp.dot(h, wg_ref[...].astype(BF16), preferred_element_type=F32)
    b = jnp.dot(h, wu_ref[...].astype(BF16), preferred_element_type=F32)
    o_ref[...] = (a * jax.nn.sigmoid(a) * b).astype(o_ref.dtype)

    @pl.when(pl.program_id(0) < cast_passes)
    def _():
        wd_bf16_ref[...] = wd_ref[...].astype(BF16)


def _gate_up(h, wg, wu, wd, layer, tm=2176, tn=256):
    m, d = h.shape
    f = wg.shape[2]
    n_i, n_j = m // tm, f // tn
    cast_blocks = f // WD_CAST_ROWS
    cast_passes = cast_blocks // n_j
    assert cast_passes * n_j == cast_blocks and cast_passes <= n_i

    def cast_block(i, j):
        return jnp.where(i < cast_passes, i * n_j + j, cast_blocks - 1)

    w_spec = pl.BlockSpec((None, d, tn), lambda i, j: (layer, 0, j))
    return pl.pallas_call(
        functools.partial(_gate_up_kernel, cast_passes=cast_passes),
        grid=(n_i, n_j),
        in_specs=[pl.BlockSpec((tm, d), lambda i, j: (i, 0), pipeline_mode=pl.Buffered(1)), w_spec, w_spec,
                  pl.BlockSpec((None, WD_CAST_ROWS, d), lambda i, j: (layer, cast_block(i, j), 0))],
        out_specs=[pl.BlockSpec((tm, tn), lambda i, j: (i, j)),
                   pl.BlockSpec((WD_CAST_ROWS, d), lambda i, j: (cast_block(i, j), 0))],
        out_shape=[jax.ShapeDtypeStruct((m, f), BF16), jax.ShapeDtypeStruct((f, d), BF16)],
        compiler_params=_cparams(("arbitrary", "arbitrary"), 58),
        name="ffn_gate_up",
    )(h, wg, wu, wd)


def _res_matmul_kernel(*refs, n_pairs, scale, cast_w):
    a_refs = refs[:n_pairs]
    w_refs = refs[n_pairs:2 * n_pairs]
    x_ref, o_ref = refs[2 * n_pairs], refs[2 * n_pairs + 1]
    load_w = (lambda ref: ref[...].astype(BF16)) if cast_w else (lambda ref: ref[...])
    acc = jnp.dot(a_refs[0][...], load_w(w_refs[0]), preferred_element_type=F32)
    for a_ref, w_ref in zip(a_refs[1:], w_refs[1:]):
        acc = acc + jnp.dot(a_ref[...], load_w(w_ref), preferred_element_type=F32)
    if scale != 1.0:
        acc = scale * acc
    o_ref[...] = x_ref[...] + acc


def _res_matmul(a_list, w, layer, x, scale, tm, tn, vmem_mib, cast_w=False):
    m, n = x.shape
    n_pairs = len(a_list)
    in_specs = [pl.BlockSpec((tm, a.shape[1]), lambda i, j: (i, 0)) for a in a_list]
    for p, a in enumerate(a_list):
        if layer is None:
            in_specs.append(pl.BlockSpec((a.shape[1], tn), lambda i, j, p=p: (p, j)))
        else:
            in_specs.append(pl.BlockSpec((None, a.shape[1], tn), lambda i, j, p=p: (layer, p, j)))
    in_specs.append(pl.BlockSpec((tm, tn), lambda i, j: (i, j)))
    assert sum(a.shape[1] for a in a_list) == w.shape[-2]
    return pl.pallas_call(
        functools.partial(_res_matmul_kernel, n_pairs=n_pairs, scale=scale, cast_w=cast_w),
        grid=(m // tm, n // tn),
        in_specs=in_specs,
        out_specs=pl.BlockSpec((tm, tn), lambda i, j: (i, j)),
        out_shape=jax.ShapeDtypeStruct((m, n), F32),
        compiler_params=_cparams(("parallel", "arbitrary"), vmem_mib),
        name="res_matmul",
    )(*a_list, *([w] * n_pairs), x)


def _ffn(x, norm_g, wg, wu, wd, layer):
    h = _rmsnorm(x, norm_g, BF16, ROW_TM)
    act, wd_bf16 = _gate_up(h, wg, wu, wd, layer)
    return _res_matmul([act], wd_bf16, None, x, 0.5, tm=544, tn=512, vmem_mib=58)


WIN_TN = 512
WIN_MAIN_TILES = COL_LORA // WIN_TN
WIN_SHIFT_TILES = COL_S5 // WIN_TN


def _win_tile_cases(h_ref, wm_ref, wt_ref, j, emit):
    @pl.when(j < WIN_MAIN_TILES)
    def _():
        emit(jnp.dot(h_ref[...], wm_ref[...], preferred_element_type=F32), True)

    @pl.when((j >= WIN_MAIN_TILES) & (j < WIN_SHIFT_TILES))
    def _():
        emit(jnp.dot(h_ref[...], wt_ref[...], preferred_element_type=F32), True)

    @pl.when(j >= WIN_SHIFT_TILES)
    def _():
        emit(jnp.dot(h_ref[...], wt_ref[...], preferred_element_type=F32), False)


def _win_prompt_kernel(h_ref, wm_ref, wt_ref, mu_ref, o_ref):
    def emit(p, shifted):
        if shifted:
            rows = lax.broadcasted_iota(jnp.int32, p.shape, 0)
            prev = jnp.where(rows == 0, 0.0, pltpu.roll(p, 1, 0))
            o_ref[...] = p + (prev - p) * mu_ref[...]
        else:
            o_ref[...] = p

    _win_tile_cases(h_ref, wm_ref, wt_ref, pl.program_id(1), emit)


def _win_sample_kernel(h_ref, wm_ref, wt_ref, mu_ref, o_ref):
    def emit(p, shifted):
        cur = p[DEC_BATCH:]
        o_ref[...] = cur + (p[:M_SAMPLE] - cur) * mu_ref[...] if shifted else cur

    _win_tile_cases(h_ref, wm_ref, wt_ref, pl.program_id(0), emit)


def _win_weight_specs(layer, col_of):
    main = pl.BlockSpec((None, D_MODEL, WIN_TN),
                        lambda *g: (layer, 0, jnp.minimum(col_of(*g), WIN_MAIN_TILES - 1)))
    tail = pl.BlockSpec((None, D_MODEL, WIN_TN),
                        lambda *g: (layer, 0, jnp.maximum(col_of(*g) - WIN_MAIN_TILES, 0)))
    mu = pl.BlockSpec((None, 1, WIN_TN), lambda *g: (layer, 0, col_of(*g)))
    return [main, tail, mu]


def _win_prompt(h, w_main, w_tail, mu_pad, layer):
    tn = WIN_TN
    return pl.pallas_call(
        _win_prompt_kernel,
        grid=(BATCH, D_INP // tn),
        in_specs=[pl.BlockSpec((SEQ, D_MODEL), lambda b, j: (b, 0), pipeline_mode=pl.Buffered(1))]
                 + _win_weight_specs(layer, lambda b, j: j),
        out_specs=pl.BlockSpec((SEQ, tn), lambda b, j: (b, j)),
        out_shape=jax.ShapeDtypeStruct((M_PROMPT, D_INP), F32),
        compiler_params=_cparams(("parallel", "arbitrary"), 60),
        name="in_proj_prompt",
    )(h, w_main, w_tail, mu_pad)


def _win_sample(h_rows, w_main, w_tail, mu_pad, layer):
    tn = WIN_TN
    rows = h_rows.shape[0]
    return pl.pallas_call(
        _win_sample_kernel,
        grid=(D_INP // tn,),
        in_specs=[pl.BlockSpec((rows, D_MODEL), lambda j: (0, 0))] + _win_weight_specs(layer, lambda j: j),
        out_specs=pl.BlockSpec((M_SAMPLE, tn), lambda j: (0, j)),
        out_shape=jax.ShapeDtypeStruct((M_SAMPLE, D_INP), F32),
        compiler_params=_cparams(("arbitrary",), 48),
        name="in_proj_sample",
    )(h_rows, w_main, w_tail, mu_pad)


def _in_proj_tail(w_in, mu):
    o_w, o_a, o_g = COL_LORA, COL_LORA + W_LORA, COL_LORA + W_LORA + A_LORA
    o_s = o_g + G_LORA
    w_main = w_in.astype(BF16)
    zw = jnp.zeros((DEPTH, D_MODEL, LORA_PAD - W_LORA), BF16)
    w_tail = jnp.concatenate([w_main[:, :, o_w:o_a], zw, w_main[:, :, o_a:o_g], zw,
                              w_main[:, :, o_g:]], axis=2)
    zm = jnp.zeros((DEPTH, LORA_PAD - W_LORA), mu.dtype)
    mu_pad = jnp.concatenate([mu[:, :o_w], mu[:, o_w:o_a], zm, mu[:, o_a:o_g], zm, mu[:, o_g:o_s],
                              jnp.zeros((DEPTH, S5W), mu.dtype)], axis=1).reshape(DEPTH, 1, D_INP)
    return w_main, w_tail, mu_pad


def _head_sum(x):
    r = lax.broadcasted_iota(jnp.int32, (LANES, LANES), 0) // HEAD
    c = lax.broadcasted_iota(jnp.int32, (LANES, LANES), 1) // HEAD
    ones = (r == c).astype(BF16)
    hi = x.astype(BF16)
    r1 = x - hi.astype(F32)
    mid = r1.astype(BF16)
    lo = (r1 - mid.astype(F32)).astype(BF16)
    outs = []
    for s in range(x.shape[1] // LANES):
        sl = slice(s * LANES, (s + 1) * LANES)
        acc = jnp.dot(hi[:, sl], ones, preferred_element_type=F32)
        acc = acc + jnp.dot(mid[:, sl], ones, preferred_element_type=F32)
        acc = acc + jnp.dot(lo[:, sl], ones, preferred_element_type=F32)
        outs.append(acc)
    return jnp.concatenate(outs, axis=1)


def _rwkv_pre_kernel(r_ref, k_ref, v_ref, lora_ref, w0_ref, a0_ref, kk_ref, ka_ref, rk_ref,
                     wup_ref, aup_ref, gup_ref, *outs, channel_major):
    *scan_outs, g_out, bonus_out = outs
    r_out, w_out, k_out, v_out, kk_out, b_out = range(6)

    def emit(idx, val):
        if channel_major:
            scan_outs[idx][...] = val.T
        else:
            scan_outs[0][idx] = val

    r = r_ref[...]
    k = k_ref[...]
    v = v_ref[...]
    emit(r_out, r)
    emit(v_out, v)
    lora = lora_ref[...]
    wd = lora[:, :LORA_PAD]
    ad = lora[:, LORA_PAD:2 * LORA_PAD]
    gd = lora[:, 2 * LORA_PAD:]
    z = -(w0_ref[...] + jnp.dot(jnp.tanh(wd).astype(BF16), wup_ref[...], preferred_element_type=F32))
    softplus = jnp.maximum(z, 0.0) + jnp.log1p(jnp.exp(-jnp.abs(z)))
    emit(w_out, jnp.exp(-jnp.exp(-softplus - 0.5)))
    a = jax.nn.sigmoid(a0_ref[...] + jnp.dot(ad.astype(BF16), aup_ref[...], preferred_element_type=F32))
    g_out[...] = jnp.dot(jax.nn.sigmoid(gd).astype(BF16), gup_ref[...], preferred_element_type=F32)
    kk = k * kk_ref[...]
    k2 = k * (1.0 + (a - 1.0) * ka_ref[...])
    emit(k_out, k2)
    kkn = kk / jnp.maximum(jnp.sqrt(_head_sum(kk * kk)), 1e-12)
    emit(kk_out, kkn)
    emit(b_out, kkn * a)
    bonus_out[...] = _head_sum(r * k2 * rk_ref[...]) * v


def _time_major_spec(m, tm, time_major):
    if not time_major:
        return pl.BlockSpec((tm, RW), lambda i: (i, 0)), (m, RW)
    per_seq = SEQ // tm
    return pl.BlockSpec((tm, RW), lambda i: (i % per_seq, i // per_seq)), (SEQ, (m // SEQ) * RW)


def _rwkv_pre(q, w0, a0, k_k, k_a, r_k, w_up, a_up, g_up, time_major, tm=PRE_TM):
    m = q.shape[0]
    out_spec, out_dims = _time_major_spec(m, tm, time_major)
    if time_major:
        per_seq = SEQ // tm
        scan_specs = [pl.BlockSpec((RW, tm), lambda i: (i // per_seq, i % per_seq))] * 6
        scan_outs = [jax.ShapeDtypeStruct(((m // SEQ) * RW, SEQ), F32)] * 6
    else:
        scan_specs = [pl.BlockSpec((6, tm, RW), lambda i: (0, i, 0))]
        scan_outs = [jax.ShapeDtypeStruct((6, m, RW), F32)]
    row = lambda v: v.reshape(1, RW)
    pad_rows = lambda u: jnp.concatenate(
        [u, jnp.zeros((LORA_PAD - u.shape[0], RW), u.dtype)], axis=0).astype(BF16)
    vec = pl.BlockSpec((1, RW), lambda i: (0, 0))
    full = lambda rows: pl.BlockSpec((rows, RW), lambda i: (0, 0))
    out = jax.ShapeDtypeStruct(out_dims, F32)
    return pl.pallas_call(
        functools.partial(_rwkv_pre_kernel, channel_major=time_major),
        grid=(m // tm,),
        in_specs=[pl.BlockSpec((tm, RW), lambda i: (i, 0)),
                  pl.BlockSpec((tm, RW), lambda i: (i, 1)),
                  pl.BlockSpec((tm, RW), lambda i: (i, 2)),
                  pl.BlockSpec((tm, LORA_W), lambda i: (i, COL_LORA // LORA_W)),
                  vec, vec, vec, vec, vec, full(LORA_PAD), full(LORA_PAD), full(G_LORA)],
        out_specs=scan_specs + [out_spec] * 2,
        out_shape=scan_outs + [out] * 2,
        compiler_params=_cparams(("parallel",), 56),
        name="rwkv_pre",
    )(q, q, q, q, row(w0), row(a0), row(k_k), row(k_a), row(r_k.reshape(RW)),
      pad_rows(w_up), pad_rows(a_up), g_up.astype(BF16))


def _rwkv_post_kernel(y_ref, bonus_ref, g_ref, lw_ref, lb_ref, *rest):
    o_ref = rest[-1]
    y = y_ref[...]
    mean = _head_sum(y) * (1.0 / HEAD)
    c = y - mean
    var = _head_sum(c * c) * (1.0 / HEAD)
    yn = c * lax.rsqrt(var + EPS_GN) * lw_ref[...] + lb_ref[...]
    o_ref[...] = ((yn + bonus_ref[...]) * g_ref[...]).astype(o_ref.dtype)


def _rwkv_post(y, bonus, g, lnx_w, lnx_b, row0, time_major, joined=None, tm=ROW_TM):
    m = y.size // RW
    blk, _ = _time_major_spec(m, tm, time_major)
    vec = pl.BlockSpec((1, RW), lambda i: (0, 0))
    in_specs, args, aliases = _joined_out(
        [blk, blk, blk, vec, vec], [y, bonus, g, lnx_w.reshape(1, RW), lnx_b.reshape(1, RW)], joined)
    return pl.pallas_call(
        _rwkv_post_kernel,
        grid=(m // tm,),
        in_specs=in_specs,
        out_specs=pl.BlockSpec((tm, RW), lambda i: (i + row0 // tm, 0)),
        out_shape=jax.ShapeDtypeStruct((M_ALL, RW), BF16),
        input_output_aliases=aliases,
        compiler_params=_cparams(("parallel",), 56),
        name="rwkv_post",
    )(*args)


WKV_UNROLL = 8
WKV_CHUNK = 64


def _wkv_kernel(r_ref, w_ref, k_ref, v_ref, kk_ref, b_ref, s0_ref, after_ref, y_ref, s_ref,
                g_ref, kq_ref, wr_ref, bt_ref, kt_ref, *, steps):
    @pl.when(pl.program_id(1) == 0)
    def _():
        s_ref[...] = s0_ref[...]

    g_ref[...] = jnp.ones((HEAD, LANES), F32)

    def step(t, carry):
        r, k, b = r_ref[t], k_ref[t], b_ref[t]
        g_prev = g_ref[...]
        g = g_prev * w_ref[t]
        g_inv = 1.0 / g
        g_ref[...] = g
        kq_ref[...] = g_prev * kk_ref[t]
        wr_ref[...] = g * r
        bt_ref[...] = b * g_inv
        kt_ref[...] = k * g_inv
        beta = jnp.sum(b * r, axis=0, keepdims=True)
        kappa = jnp.sum(k * r, axis=0, keepdims=True)
        v = v_ref[t]

        def contract(j, acc):
            sa, u = acc
            sj = s_ref[j]
            return sa - sj * kq_ref[pl.ds(j, 1), :], u + sj * wr_ref[pl.ds(j, 1), :]

        zero = jnp.zeros((HEAD, LANES), F32)
        sa, u = lax.fori_loop(0, HEAD, contract, (zero, zero), unroll=WKV_UNROLL)

        def update(j, c):
            s_ref[j] = s_ref[j] + sa * bt_ref[pl.ds(j, 1), :] + v * kt_ref[pl.ds(j, 1), :]
            return c

        lax.fori_loop(0, HEAD, update, 0, unroll=WKV_UNROLL)
        y_ref[t] = u + sa * beta + v * kappa
        return carry

    lax.fori_loop(0, steps, step, 0)

    def denormalise(j, c):
        s_ref[j] = s_ref[j] * g_ref[pl.ds(j, 1), :]
        return c

    lax.fori_loop(0, HEAD, denormalise, 0, unroll=WKV_UNROLL)


def _wkv_scan(vectors, s0, after, tc):
    vec = pl.BlockSpec((tc, HEAD, LANES), lambda c, t: (t, 0, c))
    if isinstance(vectors, (list, tuple)):
        r, w, k, v, kk, b = vectors
        vec_specs = [vec] * 6
    else:
        r = w = k = v = kk = b = vectors
        vec_specs = [pl.BlockSpec((None, tc, HEAD, LANES), lambda c, t, i=i: (i, t, 0, c)) for i in range(6)]
    length, _, n = r.shape[-3:]
    st = pl.BlockSpec((HEAD, HEAD, LANES), lambda c, t: (0, 0, c))
    return pl.pallas_call(
        functools.partial(_wkv_kernel, steps=tc),
        grid=(n // LANES, length // tc),
        in_specs=vec_specs + [st, pl.BlockSpec(memory_space=pl.ANY)],
        out_specs=[vec, st],
        out_shape=[jax.ShapeDtypeStruct((length, HEAD, n), F32),
                   jax.ShapeDtypeStruct((HEAD, HEAD, n), F32)],
        scratch_shapes=[pltpu.VMEM((HEAD, LANES), F32)] * 5,
        compiler_params=_cparams(("parallel", "arbitrary"), 40),
        name="wkv_scan",
    )(r, w, k, v, kk, b, s0, after)


def _to_scan(a, length):
    return a.reshape(length, -1, HEAD).transpose(0, 2, 1)


def _to_scan_channel_major(a):
    return a.reshape(-1, HEAD, a.shape[1]).transpose(2, 1, 0)


def _from_scan(y, length):
    return y.transpose(0, 2, 1).reshape(length, -1)


def _state_to_scan(s):
    n = s.shape[0] * s.shape[1]
    s = lax.optimization_barrier(s.reshape(n, HEAD, HEAD).transpose(0, 2, 1))
    return s.reshape(n, HEAD * HEAD).T.reshape(HEAD, HEAD, n)


def _state_from_scan(s, batch):
    n = s.shape[2]
    s = lax.optimization_barrier(s.reshape(HEAD * HEAD, n).T)
    return s.reshape(n, HEAD, HEAD).transpose(0, 2, 1).reshape(batch, HEADS, HEAD, HEAD)


def _s5_discretize(lre_ref, lim_ref, ls_ref, bre_ref, bim_ref):
    lre, lim = lre_ref[...], lim_ref[...]
    step = jnp.exp(ls_ref[...])
    mag = jnp.exp(lre * step)
    ar = mag * jnp.cos(lim * step)
    ai = mag * jnp.sin(lim * step)
    den = lre * lre + lim * lim
    nr = ar - 1.0
    f_re = (nr * lre + ai * lim) / den
    f_im = (ai * lre - nr * lim) / den
    bre, bim = bre_ref[...], bim_ref[...]
    return ar, ai, f_re * bre - f_im * bim, f_re * bim + f_im * bre


def _dot_bf16(a, b):
    return jnp.dot(a.astype(BF16), b.astype(BF16), preferred_element_type=F32)


def _s5_readout(hr, hi, cre_ref, cim_ref, d_ref, u):
    return _dot_bf16(hr, cre_ref[...]) - _dot_bf16(hi, cim_ref[...]) + d_ref[...] * u


def _cmul(ar, ai, br, bi):
    return ar * br - ai * bi, ar * bi + ai * br


def _s5_prompt_kernel(u_ref, lre_ref, lim_ref, ls_ref, bre_ref, bim_ref, cre_ref, cim_ref, d_ref,
                      y_ref, hre_ref, him_ref, hr_scr, hi_scr):
    ar, ai, bbr, bbi = _s5_discretize(lre_ref, lim_ref, ls_ref, bre_ref, bim_ref)
    u = u_ref[...]
    hr_scr[...] = _dot_bf16(u, bbr)
    hi_scr[...] = _dot_bf16(u, bbi)

    rows = lax.broadcasted_iota(jnp.int32, (SUBLANES, S5_HW), 0)
    powers = [(ar, ai)]
    for _ in range(SUBLANES - 1):
        powers.append(_cmul(*powers[-1], ar, ai))
    levels = []
    for sh in (1, 2, 4):
        pr, pi = powers[sh - 1]
        levels.append((sh, jnp.where(rows >= sh, pr, 0.0), jnp.where(rows >= sh, pi, 0.0)))
    cpr = jnp.zeros((SUBLANES, S5_HW), F32)
    cpi = jnp.zeros((SUBLANES, S5_HW), F32)
    for n, (pr, pi) in enumerate(powers):
        cpr = jnp.where(rows == n, pr, cpr)
        cpi = jnp.where(rows == n, pi, cpi)

    def tile(i, carry):
        cr, ci = carry
        r0 = pl.multiple_of(i * SUBLANES, SUBLANES)
        xr = hr_scr[pl.ds(r0, SUBLANES), :]
        xi = hi_scr[pl.ds(r0, SUBLANES), :]
        for sh, mr, mi in levels:
            sr, si = pltpu.roll(xr, sh, 0), pltpu.roll(xi, sh, 0)
            xr, xi = xr + (mr * sr - mi * si), xi + (mr * si + mi * sr)
        xr = xr + (cpr * cr - cpi * ci)
        xi = xi + (cpr * ci + cpi * cr)
        hr_scr[pl.ds(r0, SUBLANES), :] = xr
        hi_scr[pl.ds(r0, SUBLANES), :] = xi
        last = SUBLANES - 1
        return (jnp.broadcast_to(xr[last:, :], (SUBLANES, S5_HW)),
                jnp.broadcast_to(xi[last:, :], (SUBLANES, S5_HW)))

    zero = jnp.zeros((SUBLANES, S5_HW), F32)
    cr, ci = lax.fori_loop(0, SEQ // SUBLANES, tile, (zero, zero), unroll=4)
    hre_ref[...] = cr[:1]
    him_ref[...] = ci[:1]
    y_ref[...] = _s5_readout(hr_scr[...], hi_scr[...], cre_ref, cim_ref, d_ref, u)


def _s5_sample_kernel(u_ref, h0r_ref, h0i_ref, lre_ref, lim_ref, ls_ref, bre_ref, bim_ref, cre_ref, cim_ref,
                      d_ref, y_ref, hre_ref, him_ref, hr_scr, hi_scr):
    ar, ai, bbr, bbi = _s5_discretize(lre_ref, lim_ref, ls_ref, bre_ref, bim_ref)
    u = u_ref[...]
    bu_r = _dot_bf16(u, bbr)
    bu_i = _dot_bf16(u, bbi)
    hr, hi = h0r_ref[...], h0i_ref[...]
    for t in range(DEC_SEQ):
        rows = slice(t * DEC_BATCH, (t + 1) * DEC_BATCH)
        hr, hi = ar * hr - ai * hi + bu_r[rows], ar * hi + ai * hr + bu_i[rows]
        hr_scr[rows, :] = hr
        hi_scr[rows, :] = hi
    hre_ref[...] = hr
    him_ref[...] = hi
    y_ref[...] = _s5_readout(hr_scr[...], hi_scr[...], cre_ref, cim_ref, d_ref, u)


def _s5_params(lam_re, lam_im, log_step, b_re, b_im, c_re, c_im, d):
    eye = jnp.eye(S5_GT, dtype=F32)
    tiles = DEPTH * S5_TILES
    chan = lambda a: a.reshape(tiles, 1, S5_HW)

    def b_blk(b):
        x = b.transpose(0, 1, 3, 2).reshape(tiles, S5_GT, S5_C, S5_P)
        return jnp.einsum('tgcp,gh->tgchp', x, eye).reshape(tiles, S5_UW, S5_HW)

    def c_blk(c):
        x = c.transpose(0, 1, 3, 2).reshape(tiles, S5_GT, S5_P, S5_C)
        return jnp.einsum('tgpc,gh->tgphc', x, eye).reshape(tiles, S5_HW, S5_UW)

    ls = jnp.broadcast_to(log_step[:, :, None], (DEPTH, S5_G, S5_P))
    return (chan(lam_re), chan(lam_im), chan(ls), b_blk(b_re), b_blk(b_im), c_blk(c_re), c_blk(c_im),
            d.reshape(tiles, 1, S5_UW))


def _s5_param_specs(tile_of):
    chan = pl.BlockSpec((None, 1, S5_HW), lambda *g: (tile_of(*g), 0, 0))
    bspec = pl.BlockSpec((None, S5_UW, S5_HW), lambda *g: (tile_of(*g), 0, 0))
    cspec = pl.BlockSpec((None, S5_HW, S5_UW), lambda *g: (tile_of(*g), 0, 0))
    dspec = pl.BlockSpec((None, 1, S5_UW), lambda *g: (tile_of(*g), 0, 0))
    return [chan, chan, chan, bspec, bspec, cspec, cspec, dspec]


def _s5_prompt(q, params, layer):
    col0 = COL_S5 // S5_UW
    state = jax.ShapeDtypeStruct((BATCH, 1, S5_G * S5_P), F32)
    st_spec = pl.BlockSpec((None, 1, S5_HW), lambda b, g: (b, 0, g))
    return pl.pallas_call(
        _s5_prompt_kernel,
        grid=(BATCH, S5_TILES),
        in_specs=[pl.BlockSpec((SEQ, S5_UW), lambda b, g: (b, col0 + g))]
                 + _s5_param_specs(lambda b, g: layer * S5_TILES + g),
        out_specs=[pl.BlockSpec((SEQ, S5_UW), lambda b, g: (b, g)), st_spec, st_spec],
        out_shape=[jax.ShapeDtypeStruct((M_PROMPT, S5W), F32), state, state],
        scratch_shapes=[pltpu.VMEM((SEQ, S5_HW), F32), pltpu.VMEM((SEQ, S5_HW), F32)],
        compiler_params=_cparams(("parallel", "parallel"), 48),
        name="s5_prompt",
    )(q, *params)


def _s5_sample(q, h0_re, h0_im, params, layer):
    col0 = COL_S5 // S5_UW
    state = jax.ShapeDtypeStruct((DEC_BATCH, S5_G * S5_P), F32)
    st_spec = pl.BlockSpec((DEC_BATCH, S5_HW), lambda g: (0, g))
    return pl.pallas_call(
        _s5_sample_kernel,
        grid=(S5_TILES,),
        in_specs=[pl.BlockSpec((M_SAMPLE, S5_UW), lambda g: (0, col0 + g)), st_spec, st_spec]
                 + _s5_param_specs(lambda g: layer * S5_TILES + g),
        out_specs=[pl.BlockSpec((M_SAMPLE, S5_UW), lambda g: (0, g)), st_spec, st_spec],
        out_shape=[jax.ShapeDtypeStruct((M_SAMPLE, S5W), F32), state, state],
        scratch_shapes=[pltpu.VMEM((M_SAMPLE, S5_HW), F32), pltpu.VMEM((M_SAMPLE, S5_HW), F32)],
        compiler_params=_cparams(("parallel",), 32),
        name="s5_sample",
    )(q, h0_re, h0_im, *params)


def _glu_norm_kernel(y_ref, w_ref, b_ref, g_ref, *rest):
    o_ref = rest[-1]
    y = y_ref[...]
    z = 0.5 * y * (1.0 + lax.erf(y * math.sqrt(0.5)))
    gate = jax.nn.sigmoid(jnp.dot(z.astype(BF16), w_ref[...], preferred_element_type=F32) + b_ref[...])
    out = z * gate
    ms = jnp.mean(out * out, axis=-1, keepdims=True)
    o_ref[...] = (out * lax.rsqrt(ms + EPS_RMS) * g_ref[...]).astype(o_ref.dtype)


def _glu_norm(y, w_glu_bf16, layer, b_glu, gain, row0, joined=None, tm=ROW_TM):
    m = y.shape[0]
    vec = pl.BlockSpec((1, S5W), lambda i: (0, 0))
    in_specs, args, aliases = _joined_out(
        [pl.BlockSpec((tm, S5W), lambda i: (i, 0)),
         pl.BlockSpec((None, S5W, S5W), lambda i: (layer, 0, 0)), vec, vec],
        [y, w_glu_bf16, b_glu.reshape(1, S5W), gain.reshape(1, S5W)], joined)
    return pl.pallas_call(
        _glu_norm_kernel,
        grid=(m // tm,),
        in_specs=in_specs,
        out_specs=pl.BlockSpec((tm, S5W), lambda i: (i + row0 // tm, 0)),
        out_shape=jax.ShapeDtypeStruct((M_ALL, S5W), BF16),
        input_output_aliases=aliases,
        compiler_params=_cparams(("parallel",), 56),
        name="s5_glu_norm",
    )(*args)


def kernel(x_prompt, x_sample, state_shift, state_wkv, state_ssm_re, state_ssm_im, ffn1_norm, ffn1_w_gate, ffn1_w_up, ffn1_w_down, mix_norm, w_in, shift_mu, rw_w0, rw_w_up, rw_a0, rw_a_up, rw_g_up, rw_k_k, rw_k_a, rw_r_k, rw_lnx_w, rw_lnx_b, s5_lam_re, s5_lam_im, s5_b_re, s5_b_im, s5_c_re, s5_c_im, s5_d, s5_log_step, s5_w_glu, s5_b_glu, s5_out_norm, w_out, ffn2_norm, ffn2_w_gate, ffn2_w_up, ffn2_w_down, final_norm):
    x = jnp.concatenate([x_prompt.reshape(M_PROMPT, D_MODEL),
                         x_sample.transpose(1, 0, 2).reshape(M_SAMPLE, D_MODEL)], axis=0)
    zero_state = jnp.zeros((HEAD, HEAD, BATCH * HEADS), F32)
    shift_p, wkv_p, re_p, im_p, shift_s, wkv_s, re_s, im_s = ([] for _ in range(8))
    w_glu = s5_w_glu.astype(BF16)
    s5p = _s5_params(s5_lam_re, s5_lam_im, s5_log_step, s5_b_re, s5_b_im, s5_c_re, s5_c_im, s5_d)
    w_main, w_tail, mu_pad = _in_proj_tail(w_in, shift_mu)

    for l in range(DEPTH):
        x = _ffn(x, ffn1_norm[l], ffn1_w_gate, ffn1_w_up, ffn1_w_down, l)

        h_mix = _rmsnorm(x, mix_norm[l], BF16, ROW_TM)
        last8 = _rmsnorm(x, mix_norm[l], F32, SUBLANES, SEQ - SUBLANES, BATCH * SUBLANES, SEQ // SUBLANES)
        shift_p.append(last8[SUBLANES - 1::SUBLANES])
        shift_s.append(_rmsnorm(x, mix_norm[l], F32, DEC_BATCH, M_ALL - DEC_BATCH, DEC_BATCH))

        q_p = _win_prompt(h_mix, w_main, w_tail, mu_pad, l)
        q_s = _win_sample(jnp.concatenate([state_shift[l].astype(BF16), h_mix[M_PROMPT:]], axis=0),
                          w_main, w_tail, mu_pad, l)

        rw_args = (rw_w0[l], rw_a0[l], rw_k_k[l], rw_k_a[l], rw_r_k[l], rw_w_up[l], rw_a_up[l], rw_g_up[l])
        *scan_p, gate_p, bonus_p = _rwkv_pre(q_p, *rw_args, time_major=True, tm=LANES)
        scan_s, gate_s, bonus_s = _rwkv_pre(q_s, *rw_args, time_major=False)
        scan_p = [_to_scan_channel_major(a) for a in scan_p]
        scan_s = _to_scan(scan_s, 6 * DEC_SEQ).reshape(6, DEC_SEQ, HEAD, DEC_BATCH * HEADS)

        y5_p, hre_p, him_p = _s5_prompt(q_p, s5p, l)
        y5_s, hre_s, him_s = _s5_sample(q_s, state_ssm_re[l].reshape(DEC_BATCH, S5_G * S5_P),
                                        state_ssm_im[l].reshape(DEC_BATCH, S5_G * S5_P), s5p, l)
        re_p.append(hre_p.reshape(BATCH, S5_G, S5_P))
        im_p.append(him_p.reshape(BATCH, S5_G, S5_P))
        re_s.append(hre_s.reshape(DEC_BATCH, S5_G, S5_P))
        im_s.append(him_s.reshape(DEC_BATCH, S5_G, S5_P))
        y_s5 = _glu_norm(y5_p, w_glu, l, s5_b_glu[l], s5_out_norm[l], 0)
        y_s5 = _glu_norm(y5_s, w_glu, l, s5_b_glu[l], s5_out_norm[l], M_PROMPT, y_s5)

        y_p, s_p = _wkv_scan(scan_p, zero_state, y_s5, tc=WKV_CHUNK)
        y_s, s_s = _wkv_scan(scan_s, _state_to_scan(state_wkv[l]), y_s5, tc=DEC_SEQ)
        wkv_p.append(_state_from_scan(s_p, BATCH))
        wkv_s.append(_state_from_scan(s_s, DEC_BATCH))
        y_rw = _rwkv_post(_from_scan(y_p, SEQ), bonus_p, gate_p, rw_lnx_w[l], rw_lnx_b[l], 0, True)
        y_rw = _rwkv_post(_from_scan(y_s, DEC_SEQ).reshape(M_SAMPLE, RW), bonus_s, gate_s,
                          rw_lnx_w[l], rw_lnx_b[l], M_PROMPT, False, y_rw)

        x = _res_matmul([y_rw, y_s5], w_out, l, x, 1.0, tm=1088, tn=512, vmem_mib=56, cast_w=True)
        x = _ffn(x, ffn2_norm[l], ffn2_w_gate, ffn2_w_up, ffn2_w_down, l)

    y_prompt = _rmsnorm(x, final_norm, F32, ROW_TM, 0, M_PROMPT).reshape(BATCH, SEQ, D_MODEL)
    y_sample = _rmsnorm(x, final_norm, F32, ROW_TM, M_PROMPT, M_SAMPLE).reshape(
        DEC_SEQ, DEC_BATCH, D_MODEL).transpose(1, 0, 2)
    st = jnp.stack
    return (y_prompt, y_sample, st(shift_p), st(wkv_p), st(re_p), st(im_p),
            st(shift_s), st(wkv_s), st(re_s), st(im_s))
```

```python
import functools
import math

import jax
import jax.numpy as jnp
from jax import lax
from jax.experimental import pallas as pl
from jax.experimental.pallas import tpu as pltpu

F32 = jnp.float32
BF16 = jnp.bfloat16

D_MODEL = 4096
BATCH = 4
SEQ = 2048
DEPTH = 2
DEC_BATCH = 128
DEC_SEQ = 4
M_PROMPT = BATCH * SEQ
M_SAMPLE = DEC_BATCH * DEC_SEQ
M_ALL = M_PROMPT + M_SAMPLE

RW = D_MODEL // 2
HEAD = 64
HEADS = RW // HEAD
S5W = D_MODEL - RW
S5_C = 16
S5_G = S5W // S5_C
S5_P = 64
W_LORA = 96
A_LORA = 96
G_LORA = 256
LORA_PAD = 128
LORA_W = 2 * LORA_PAD + G_LORA
D_FF = 11008
EPS_RMS = 1e-6
EPS_GN = 64e-5

COL_LORA = 3 * RW
COL_S5 = COL_LORA + LORA_W
D_INP = COL_S5 + S5W

V7X_VMEM_BYTES = 64 * 1024 * 1024
LANES = 128
SUBLANES = 8

S5_GT = 8
S5_TILES = S5_G // S5_GT
S5_UW = S5_GT * S5_C
S5_HW = S5_GT * S5_P

ROW_TM = 512
PRE_TM = 256


def _cparams(semantics, vmem_mib):
    assert vmem_mib * 1024 * 1024 < V7X_VMEM_BYTES
    return pltpu.CompilerParams(dimension_semantics=semantics,
                                vmem_limit_bytes=vmem_mib * 1024 * 1024)


def _joined_out(in_specs, args, joined):
    if joined is None:
        return in_specs, args, {}
    return (in_specs + [pl.BlockSpec(memory_space=pl.ANY)], args + [joined], {len(args): 0})


def _rmsnorm_kernel(x_ref, g_ref, o_ref):
    x = x_ref[...]
    ms = jnp.mean(x * x, axis=-1, keepdims=True)
    o_ref[...] = (x * lax.rsqrt(ms + EPS_RMS) * g_ref[...]).astype(o_ref.dtype)


def _rmsnorm(x, g, out_dtype, tm, row0=0, rows=None, block_stride=1):
    d = x.shape[1]
    m = x.shape[0] if rows is None else rows
    return pl.pallas_call(
        _rmsnorm_kernel,
        grid=(m // tm,),
        in_specs=[pl.BlockSpec((tm, d), lambda i: (i * block_stride + row0 // tm, 0)),
                  pl.BlockSpec((1, d), lambda i: (0, 0))],
        out_specs=pl.BlockSpec((tm, d), lambda i: (i, 0)),
        out_shape=jax.ShapeDtypeStruct((m, d), out_dtype),
        compiler_params=_cparams(("parallel",), 40),
        name="rmsnorm",
    )(x, g.reshape(1, d))


WD_CAST_ROWS = 128


def _gate_up_kernel(h_ref, wg_ref, wu_ref, wd_ref, o_ref, wd_bf16_ref, *, cast_passes):
    h = h_ref[...]
    a = jnp.dot(h, wg_ref[...].astype(BF16), preferred_element_type=F32)
    b = jnp.dot(h, wu_ref[...].astype(BF16), preferred_element_type=F32)
    o_ref[...] = (a * jax.nn.sigmoid(a) * b).astype(o_ref.dtype)

    @pl.when(pl.program_id(0) < cast_passes)
    def _():
        wd_bf16_ref[...] = wd_ref[...].astype(BF16)


def _gate_up(h, wg, wu, wd, layer, tm=2176, tn=256):
    m, d = h.shape
    f = wg.shape[2]
    n_i, n_j = m // tm, f // tn
    cast_blocks = f // WD_CAST_ROWS
    cast_passes = cast_blocks // n_j
    assert cast_passes * n_j == cast_blocks and cast_passes <= n_i

    def cast_block(i, j):
        return jnp.where(i < cast_passes, i * n_j + j, cast_blocks - 1)

    w_spec = pl.BlockSpec((None, d, tn), lambda i, j: (layer, 0, j))
    return pl.pallas_call(
        functools.partial(_gate_up_kernel, cast_passes=cast_passes),
        grid=(n_i, n_j),
        in_specs=[pl.BlockSpec((tm, d), lambda i, j: (i, 0), pipeline_mode=pl.Buffered(1)), w_spec, w_spec,
                  pl.BlockSpec((None, WD_CAST_ROWS, d), lambda i, j: (layer, cast_block(i, j), 0))],
        out_specs=[pl.BlockSpec((tm, tn), lambda i, j: (i, j)),
                   pl.BlockSpec((WD_CAST_ROWS, d), lambda i, j: (cast_block(i, j), 0))],
        out_shape=[jax.ShapeDtypeStruct((m, f), BF16), jax.ShapeDtypeStruct((f, d), BF16)],
        compiler_params=_cparams(("arbitrary", "arbitrary"), 58),
        name="ffn_gate_up",
    )(h, wg, wu, wd)


def _res_matmul_kernel(*refs, n_pairs, scale, cast_w):
    a_refs = refs[:n_pairs]
    w_refs = refs[n_pairs:2 * n_pairs]
    x_ref, o_ref = refs[2 * n_pairs], refs[2 * n_pairs + 1]
    load_w = (lambda ref: ref[...].astype(BF16)) if cast_w else (lambda ref: ref[...])
    acc = jnp.dot(a_refs[0][...], load_w(w_refs[0]), preferred_element_type=F32)
    for a_ref, w_ref in zip(a_refs[1:], w_refs[1:]):
        acc = acc + jnp.dot(a_ref[...], load_w(w_ref), preferred_element_type=F32)
    if scale != 1.0:
        acc = scale * acc
    o_ref[...] = x_ref[...] + acc


def _res_matmul(a_list, w, layer, x, scale, tm, tn, vmem_mib, cast_w=False):
    m, n = x.shape
    n_pairs = len(a_list)
    in_specs = [pl.BlockSpec((tm, a.shape[1]), lambda i, j: (i, 0)) for a in a_list]
    for p, a in enumerate(a_list):
        if layer is None:
            in_specs.append(pl.BlockSpec((a.shape[1], tn), lambda i, j, p=p: (p, j)))
        else:
            in_specs.append(pl.BlockSpec((None, a.shape[1], tn), lambda i, j, p=p: (layer, p, j)))
    in_specs.append(pl.BlockSpec((tm, tn), lambda i, j: (i, j)))
    assert sum(a.shape[1] for a in a_list) == w.shape[-2]
    return pl.pallas_call(
        functools.partial(_res_matmul_kernel, n_pairs=n_pairs, scale=scale, cast_w=cast_w),
        grid=(m // tm, n // tn),
        in_specs=in_specs,
        out_specs=pl.BlockSpec((tm, tn), lambda i, j: (i, j)),
        out_shape=jax.ShapeDtypeStruct((m, n), F32),
        compiler_params=_cparams(("parallel", "arbitrary"), vmem_mib),
        name="res_matmul",
    )(*a_list, *([w] * n_pairs), x)


def _ffn(x, norm_g, wg, wu, wd, layer):
    h = _rmsnorm(x, norm_g, BF16, ROW_TM)
    act, wd_bf16 = _gate_up(h, wg, wu, wd, layer)
    return _res_matmul([act], wd_bf16, None, x, 0.5, tm=544, tn=512, vmem_mib=58)


WIN_TN = 512
WIN_MAIN_TILES = COL_LORA // WIN_TN
WIN_SHIFT_TILES = COL_S5 // WIN_TN


def _win_tile_cases(h_ref, wm_ref, wt_ref, j, emit):
    @pl.when(j < WIN_MAIN_TILES)
    def _():
        emit(jnp.dot(h_ref[...], wm_ref[...], preferred_element_type=F32), True)

    @pl.when((j >= WIN_MAIN_TILES) & (j < WIN_SHIFT_TILES))
    def _():
        emit(jnp.dot(h_ref[...], wt_ref[...], preferred_element_type=F32), True)

    @pl.when(j >= WIN_SHIFT_TILES)
    def _():
        emit(jnp.dot(h_ref[...], wt_ref[...], preferred_element_type=F32), False)


def _win_prompt_kernel(h_ref, wm_ref, wt_ref, mu_ref, o_ref):
    def emit(p, shifted):
        if shifted:
            rows = lax.broadcasted_iota(jnp.int32, p.shape, 0)
            prev = jnp.where(rows == 0, 0.0, pltpu.roll(p, 1, 0))
            o_ref[...] = p + (prev - p) * mu_ref[...]
        else:
            o_ref[...] = p

    _win_tile_cases(h_ref, wm_ref, wt_ref, pl.program_id(1), emit)


def _win_sample_kernel(h_ref, wm_ref, wt_ref, mu_ref, o_ref):
    def emit(p, shifted):
        cur = p[DEC_BATCH:]
        o_ref[...] = cur + (p[:M_SAMPLE] - cur) * mu_ref[...] if shifted else cur

    _win_tile_cases(h_ref, wm_ref, wt_ref, pl.program_id(0), emit)


def _win_weight_specs(layer, col_of):
    main = pl.BlockSpec((None, D_MODEL, WIN_TN),
                        lambda *g: (layer, 0, jnp.minimum(col_of(*g), WIN_MAIN_TILES - 1)))
    tail = pl.BlockSpec((None, D_MODEL, WIN_TN),
                        lambda *g: (layer, 0, jnp.maximum(col_of(*g) - WIN_MAIN_TILES, 0)))
    mu = pl.BlockSpec((None, 1, WIN_TN), lambda *g: (layer, 0, col_of(*g)))
    return [main, tail, mu]


def _win_prompt(h, w_main, w_tail, mu_pad, layer):
    tn = WIN_TN
    return pl.pallas_call(
        _win_prompt_kernel,
        grid=(BATCH, D_INP // tn),
        in_specs=[pl.BlockSpec((SEQ, D_MODEL), lambda b, j: (b, 0), pipeline_mode=pl.Buffered(1))]
                 + _win_weight_specs(layer, lambda b, j: j),
        out_specs=pl.BlockSpec((SEQ, tn), lambda b, j: (b, j)),
        out_shape=jax.ShapeDtypeStruct((M_PROMPT, D_INP), F32),
        compiler_params=_cparams(("parallel", "arbitrary"), 60),
        name="in_proj_prompt",
    )(h, w_main, w_tail, mu_pad)


def _win_sample(h_rows, w_main, w_tail, mu_pad, layer):
    tn = WIN_TN
    rows = h_rows.shape[0]
    return pl.pallas_call(
        _win_sample_kernel,
        grid=(D_INP // tn,),
        in_specs=[pl.BlockSpec((rows, D_MODEL), lambda j: (0, 0))] + _win_weight_specs(layer, lambda j: j),
        out_specs=pl.BlockSpec((M_SAMPLE, tn), lambda j: (0, j)),
        out_shape=jax.ShapeDtypeStruct((M_SAMPLE, D_INP), F32),
        compiler_params=_cparams(("arbitrary",), 48),
        name="in_proj_sample",
    )(h_rows, w_main, w_tail, mu_pad)


def _in_proj_tail(w_in, mu):
    o_w, o_a, o_g = COL_LORA, COL_LORA + W_LORA, COL_LORA + W_LORA + A_LORA
    o_s = o_g + G_LORA
    w_main = w_in.astype(BF16)
    zw = jnp.zeros((DEPTH, D_MODEL, LORA_PAD - W_LORA), BF16)
    w_tail = jnp.concatenate([w_main[:, :, o_w:o_a], zw, w_main[:, :, o_a:o_g], zw,
                              w_main[:, :, o_g:]], axis=2)
    zm = jnp.zeros((DEPTH, LORA_PAD - W_LORA), mu.dtype)
    mu_pad = jnp.concatenate([mu[:, :o_w], mu[:, o_w:o_a], zm, mu[:, o_a:o_g], zm, mu[:, o_g:o_s],
                              jnp.zeros((DEPTH, S5W), mu.dtype)], axis=1).reshape(DEPTH, 1, D_INP)
    return w_main, w_tail, mu_pad


def _head_sum(x):
    r = lax.broadcasted_iota(jnp.int32, (LANES, LANES), 0) // HEAD
    c = lax.broadcasted_iota(jnp.int32, (LANES, LANES), 1) // HEAD
    ones = (r == c).astype(BF16)
    hi = x.astype(BF16)
    r1 = x - hi.astype(F32)
    mid = r1.astype(BF16)
    lo = (r1 - mid.astype(F32)).astype(BF16)
    outs = []
    for s in range(x.shape[1] // LANES):
        sl = slice(s * LANES, (s + 1) * LANES)
        acc = jnp.dot(hi[:, sl], ones, preferred_element_type=F32)
        acc = acc + jnp.dot(mid[:, sl], ones, preferred_element_type=F32)
        acc = acc + jnp.dot(lo[:, sl], ones, preferred_element_type=F32)
        outs.append(acc)
    return jnp.concatenate(outs, axis=1)


def _rwkv_pre_kernel(r_ref, k_ref, v_ref, lora_ref, w0_ref, a0_ref, kk_ref, ka_ref, rk_ref,
                     wup_ref, aup_ref, gup_ref, *outs, channel_major):
    *scan_outs, g_out, bonus_out = outs
    r_out, w_out, k_out, v_out, kk_out, b_out = range(6)

    def emit(idx, val):
        if channel_major:
            scan_outs[idx][...] = val.T
        else:
            scan_outs[0][idx] = val

    r = r_ref[...]
    k = k_ref[...]
    v = v_ref[...]
    emit(r_out, r)
    emit(v_out, v)
    lora = lora_ref[...]
    wd = lora[:, :LORA_PAD]
    ad = lora[:, LORA_PAD:2 * LORA_PAD]
    gd = lora[:, 2 * LORA_PAD:]
    z = -(w0_ref[...] + jnp.dot(jnp.tanh(wd).astype(BF16), wup_ref[...], preferred_element_type=F32))
    softplus = jnp.maximum(z, 0.0) + jnp.log1p(jnp.exp(-jnp.abs(z)))
    emit(w_out, jnp.exp(-jnp.exp(-softplus - 0.5)))
    a = jax.nn.sigmoid(a0_ref[...] + jnp.dot(ad.astype(BF16), aup_ref[...], preferred_element_type=F32))
    g_out[...] = jnp.dot(jax.nn.sigmoid(gd).astype(BF16), gup_ref[...], preferred_element_type=F32)
    kk = k * kk_ref[...]
    k2 = k * (1.0 + (a - 1.0) * ka_ref[...])
    emit(k_out, k2)
    kkn = kk / jnp.maximum(jnp.sqrt(_head_sum(kk * kk)), 1e-12)
    emit(kk_out, kkn)
    emit(b_out, kkn * a)
    bonus_out[...] = _head_sum(r * k2 * rk_ref[...]) * v


def _time_major_spec(m, tm, time_major):
    if not time_major:
        return pl.BlockSpec((tm, RW), lambda i: (i, 0)), (m, RW)
    per_seq = SEQ // tm
    return pl.BlockSpec((tm, RW), lambda i: (i % per_seq, i // per_seq)), (SEQ, (m // SEQ) * RW)


def _rwkv_pre(q, w0, a0, k_k, k_a, r_k, w_up, a_up, g_up, time_major, tm=PRE_TM):
    m = q.shape[0]
    out_spec, out_dims = _time_major_spec(m, tm, time_major)
    if time_major:
        per_seq = SEQ // tm
        scan_specs = [pl.BlockSpec((RW, tm), lambda i: (i // per_seq, i % per_seq))] * 6
        scan_outs = [jax.ShapeDtypeStruct(((m // SEQ) * RW, SEQ), F32)] * 6
    else:
        scan_specs = [pl.BlockSpec((6, tm, RW), lambda i: (0, i, 0))]
        scan_outs = [jax.ShapeDtypeStruct((6, m, RW), F32)]
    row = lambda v: v.reshape(1, RW)
    pad_rows = lambda u: jnp.concatenate(
        [u, jnp.zeros((LORA_PAD - u.shape[0], RW), u.dtype)], axis=0).astype(BF16)
    vec = pl.BlockSpec((1, RW), lambda i: (0, 0))
    full = lambda rows: pl.BlockSpec((rows, RW), lambda i: (0, 0))
    out = jax.ShapeDtypeStruct(out_dims, F32)
    return pl.pallas_call(
        functools.partial(_rwkv_pre_kernel, channel_major=time_major),
        grid=(m // tm,),
        in_specs=[pl.BlockSpec((tm, RW), lambda i: (i, 0)),
                  pl.BlockSpec((tm, RW), lambda i: (i, 1)),
                  pl.BlockSpec((tm, RW), lambda i: (i, 2)),
                  pl.BlockSpec((tm, LORA_W), lambda i: (i, COL_LORA // LORA_W)),
                  vec, vec, vec, vec, vec, full(LORA_PAD), full(LORA_PAD), full(G_LORA)],
        out_specs=scan_specs + [out_spec] * 2,
        out_shape=scan_outs + [out] * 2,
        compiler_params=_cparams(("parallel",), 56),
        name="rwkv_pre",
    )(q, q, q, q, row(w0), row(a0), row(k_k), row(k_a), row(r_k.reshape(RW)),
      pad_rows(w_up), pad_rows(a_up), g_up.astype(BF16))


def _rwkv_post_kernel(y_ref, bonus_ref, g_ref, lw_ref, lb_ref, *rest):
    o_ref = rest[-1]
    y = y_ref[...]
    mean = _head_sum(y) * (1.0 / HEAD)
    c = y - mean
    var = _head_sum(c * c) * (1.0 / HEAD)
    yn = c * lax.rsqrt(var + EPS_GN) * lw_ref[...] + lb_ref[...]
    o_ref[...] = ((yn + bonus_ref[...]) * g_ref[...]).astype(o_ref.dtype)


def _rwkv_post(y, bonus, g, lnx_w, lnx_b, row0, time_major, joined=None, tm=ROW_TM):
    m = y.size // RW
    blk, _ = _time_major_spec(m, tm, time_major)
    vec = pl.BlockSpec((1, RW), lambda i: (0, 0))
    in_specs, args, aliases = _joined_out(
        [blk, blk, blk, vec, vec], [y, bonus, g, lnx_w.reshape(1, RW), lnx_b.reshape(1, RW)], joined)
    return pl.pallas_call(
        _rwkv_post_kernel,
        grid=(m // tm,),
        in_specs=in_specs,
        out_specs=pl.BlockSpec((tm, RW), lambda i: (i + row0 // tm, 0)),
        out_shape=jax.ShapeDtypeStruct((M_ALL, RW), BF16),
        input_output_aliases=aliases,
        compiler_params=_cparams(("parallel",), 56),
        name="rwkv_post",
    )(*args)


WKV_UNROLL = 8
WKV_CHUNK = 64


def _wkv_kernel(r_ref, w_ref, k_ref, v_ref, kk_ref, b_ref, s0_ref, after_ref, y_ref, s_ref,
                g_ref, kq_ref, wr_ref, bt_ref, kt_ref, *, steps):
    @pl.when(pl.program_id(1) == 0)
    def _():
        s_ref[...] = s0_ref[...]

    g_ref[...] = jnp.ones((HEAD, LANES), F32)

    def step(t, carry):
        r, k, b = r_ref[t], k_ref[t], b_ref[t]
        g_prev = g_ref[...]
        g = g_prev * w_ref[t]
        g_inv = 1.0 / g
        g_ref[...] = g
        kq_ref[...] = g_prev * kk_ref[t]
        wr_ref[...] = g * r
        bt_ref[...] = b * g_inv
        kt_ref[...] = k * g_inv
        beta = jnp.sum(b * r, axis=0, keepdims=True)
        kappa = jnp.sum(k * r, axis=0, keepdims=True)
        v = v_ref[t]

        def contract(j, acc):
            sa, u = acc
            sj = s_ref[j]
            return sa - sj * kq_ref[pl.ds(j, 1), :], u + sj * wr_ref[pl.ds(j, 1), :]

        zero = jnp.zeros((HEAD, LANES), F32)
        sa, u = lax.fori_loop(0, HEAD, contract, (zero, zero), unroll=WKV_UNROLL)

        def update(j, c):
            s_ref[j] = s_ref[j] + sa * bt_ref[pl.ds(j, 1), :] + v * kt_ref[pl.ds(j, 1), :]
            return c

        lax.fori_loop(0, HEAD, update, 0, unroll=WKV_UNROLL)
        y_ref[t] = u + sa * beta + v * kappa
        return carry

    lax.fori_loop(0, steps, step, 0)

    def denormalise(j, c):
        s_ref[j] = s_ref[j] * g_ref[pl.ds(j, 1), :]
        return c

    lax.fori_loop(0, HEAD, denormalise, 0, unroll=WKV_UNROLL)


def _wkv_scan(vectors, s0, after, tc):
    vec = pl.BlockSpec((tc, HEAD, LANES), lambda c, t: (t, 0, c))
    if isinstance(vectors, (list, tuple)):
        r, w, k, v, kk, b = vectors
        vec_specs = [vec] * 6
    else:
        r = w = k = v = kk = b = vectors
        vec_specs = [pl.BlockSpec((None, tc, HEAD, LANES), lambda c, t, i=i: (i, t, 0, c)) for i in range(6)]
    length, _, n = r.shape[-3:]
    st = pl.BlockSpec((HEAD, HEAD, LANES), lambda c, t: (0, 0, c))
    return pl.pallas_call(
        functools.partial(_wkv_kernel, steps=tc),
        grid=(n // LANES, length // tc),
        in_specs=vec_specs + [st, pl.BlockSpec(memory_space=pl.ANY)],
        out_specs=[vec, st],
        out_shape=[jax.ShapeDtypeStruct((length, HEAD, n), F32),
                   jax.ShapeDtypeStruct((HEAD, HEAD, n), F32)],
        scratch_shapes=[pltpu.VMEM((HEAD, LANES), F32)] * 5,
        compiler_params=_cparams(("parallel", "arbitrary"), 40),
        name="wkv_scan",
    )(r, w, k, v, kk, b, s0, after)


def _wkv_rows_kernel(r_ref, w_ref, k_ref, v_ref, kk_ref, b_ref, s0_ref, after_ref, *rest, steps):
    y_ref, s_ref = rest[-2], rest[-1]
    for t in range(steps):
        r, w, k, kk, b = r_ref[t], w_ref[t], k_ref[t], kk_ref[t], b_ref[t]
        src = s0_ref if t == 0 else s_ref

        def row(i, carry, t=t, src=src, r=r, w=w, k=k, kk=kk, b=b):
            si = src[i]
            sa = -jnp.sum(si * kk, axis=0, keepdims=True)
            sn = si * w + sa * b + v_ref[t, pl.ds(i, 1), :] * k
            s_ref[i] = sn
            y_ref[t, pl.ds(i, 1), :] = jnp.sum(sn * r, axis=0, keepdims=True)
            return carry

        lax.fori_loop(0, HEAD, row, 0, unroll=4)


def _wkv_rows(vectors, s_all, layer, after, s_prev=None):
    _, length, _, n = vectors.shape
    vec_specs = [pl.BlockSpec((None, length, HEAD, LANES), lambda c, i=i: (i, 0, 0, c)) for i in range(6)]
    st = pl.BlockSpec((None, HEAD, HEAD, LANES), lambda c: (layer, 0, 0, c))
    in_specs = vec_specs + [st, pl.BlockSpec(memory_space=pl.ANY)]
    args = [vectors] * 6 + [s_all, after]
    in_specs, args, aliases = _joined_out(in_specs, args, s_prev)
    if aliases:
        aliases = {len(args) - 1: 1}
    return pl.pallas_call(
        functools.partial(_wkv_rows_kernel, steps=length),
        grid=(n // LANES,),
        in_specs=in_specs,
        out_specs=[pl.BlockSpec((length, HEAD, LANES), lambda c: (0, 0, c)), st],
        out_shape=[jax.ShapeDtypeStruct((length, HEAD, n), F32), jax.ShapeDtypeStruct(s_all.shape, F32)],
        input_output_aliases=aliases,
        compiler_params=_cparams(("parallel",), 40),
        name="wkv_rows",
    )(*args)


def _to_scan(a, length):
    return a.reshape(length, -1, HEAD).transpose(0, 2, 1)


def _to_scan_channel_major(a):
    return a.reshape(-1, HEAD, a.shape[1]).transpose(2, 1, 0)


def _from_scan(y, length):
    return y.transpose(0, 2, 1).reshape(length, -1)


def _sample_state_to_rows(s):
    n = s.shape[1] * s.shape[2]
    return s.reshape(DEPTH, n, HEAD * HEAD).transpose(0, 2, 1).reshape(DEPTH, HEAD, HEAD, n)


def _sample_state_from_rows(s):
    n = s.shape[-1]
    return s.reshape(DEPTH, HEAD * HEAD, n).transpose(0, 2, 1).reshape(DEPTH, n // HEADS, HEADS, HEAD, HEAD)


def _state_from_scan(s, batch):
    n = s.shape[2]
    s = lax.optimization_barrier(s.reshape(HEAD * HEAD, n).T)
    return s.reshape(n, HEAD, HEAD).transpose(0, 2, 1).reshape(batch, HEADS, HEAD, HEAD)


def _s5_discretize(lre_ref, lim_ref, ls_ref, bre_ref, bim_ref):
    lre, lim = lre_ref[...], lim_ref[...]
    step = jnp.exp(ls_ref[...])
    mag = jnp.exp(lre * step)
    ar = mag * jnp.cos(lim * step)
    ai = mag * jnp.sin(lim * step)
    den = lre * lre + lim * lim
    nr = ar - 1.0
    f_re = (nr * lre + ai * lim) / den
    f_im = (ai * lre - nr * lim) / den
    bre, bim = bre_ref[...], bim_ref[...]
    return ar, ai, f_re * bre - f_im * bim, f_re * bim + f_im * bre


def _dot_bf16(a, b):
    return jnp.dot(a.astype(BF16), b.astype(BF16), preferred_element_type=F32)


def _s5_readout(hr, hi, cre_ref, cim_ref, d_ref, u):
    return _dot_bf16(hr, cre_ref[...]) - _dot_bf16(hi, cim_ref[...]) + d_ref[...] * u


def _cmul(ar, ai, br, bi):
    return ar * br - ai * bi, ar * bi + ai * br


def _s5_prompt_kernel(u_ref, lre_ref, lim_ref, ls_ref, bre_ref, bim_ref, cre_ref, cim_ref, d_ref,
                      y_ref, hre_ref, him_ref, hr_scr, hi_scr):
    ar, ai, bbr, bbi = _s5_discretize(lre_ref, lim_ref, ls_ref, bre_ref, bim_ref)
    u = u_ref[...]
    hr_scr[...] = _dot_bf16(u, bbr)
    hi_scr[...] = _dot_bf16(u, bbi)

    rows = lax.broadcasted_iota(jnp.int32, (SUBLANES, S5_HW), 0)
    powers = [(ar, ai)]
    for _ in range(SUBLANES - 1):
        powers.append(_cmul(*powers[-1], ar, ai))
    levels = []
    for sh in (1, 2, 4):
        pr, pi = powers[sh - 1]
        levels.append((sh, jnp.where(rows >= sh, pr, 0.0), jnp.where(rows >= sh, pi, 0.0)))
    cpr = jnp.zeros((SUBLANES, S5_HW), F32)
    cpi = jnp.zeros((SUBLANES, S5_HW), F32)
    for n, (pr, pi) in enumerate(powers):
        cpr = jnp.where(rows == n, pr, cpr)
        cpi = jnp.where(rows == n, pi, cpi)

    def tile(i, carry):
        cr, ci = carry
        r0 = pl.multiple_of(i * SUBLANES, SUBLANES)
        xr = hr_scr[pl.ds(r0, SUBLANES), :]
        xi = hi_scr[pl.ds(r0, SUBLANES), :]
        for sh, mr, mi in levels:
            sr, si = pltpu.roll(xr, sh, 0), pltpu.roll(xi, sh, 0)
            xr, xi = xr + (mr * sr - mi * si), xi + (mr * si + mi * sr)
        xr = xr + (cpr * cr - cpi * ci)
        xi = xi + (cpr * ci + cpi * cr)
        hr_scr[pl.ds(r0, SUBLANES), :] = xr
        hi_scr[pl.ds(r0, SUBLANES), :] = xi
        last = SUBLANES - 1
        return (jnp.broadcast_to(xr[last:, :], (SUBLANES, S5_HW)),
                jnp.broadcast_to(xi[last:, :], (SUBLANES, S5_HW)))

    zero = jnp.zeros((SUBLANES, S5_HW), F32)
    cr, ci = lax.fori_loop(0, SEQ // SUBLANES, tile, (zero, zero), unroll=4)
    hre_ref[...] = cr[:1]
    him_ref[...] = ci[:1]
    y_ref[...] = _s5_readout(hr_scr[...], hi_scr[...], cre_ref, cim_ref, d_ref, u)


def _s5_sample_kernel(u_ref, h0r_ref, h0i_ref, lre_ref, lim_ref, ls_ref, bre_ref, bim_ref, cre_ref, cim_ref,
                      d_ref, y_ref, hre_ref, him_ref, hr_scr, hi_scr):
    ar, ai, bbr, bbi = _s5_discretize(lre_ref, lim_ref, ls_ref, bre_ref, bim_ref)
    u = u_ref[...]
    bu_r = _dot_bf16(u, bbr)
    bu_i = _dot_bf16(u, bbi)
    hr, hi = h0r_ref[...], h0i_ref[...]
    for t in range(DEC_SEQ):
        rows = slice(t * DEC_BATCH, (t + 1) * DEC_BATCH)
        hr, hi = ar * hr - ai * hi + bu_r[rows], ar * hi + ai * hr + bu_i[rows]
        hr_scr[rows, :] = hr
        hi_scr[rows, :] = hi
    hre_ref[...] = hr
    him_ref[...] = hi
    y_ref[...] = _s5_readout(hr_scr[...], hi_scr[...], cre_ref, cim_ref, d_ref, u)


def _s5_params(lam_re, lam_im, log_step, b_re, b_im, c_re, c_im, d):
    tiles = DEPTH * S5_TILES
    chan = lambda a: a.reshape(tiles, 1, S5_HW)

    def block_diag(x, rows_per_group, cols_per_group):
        tiled = jnp.tile(x, (1, 1, S5_GT))
        r = lax.broadcasted_iota(jnp.int32, tiled.shape[1:], 0) // rows_per_group
        c = lax.broadcasted_iota(jnp.int32, tiled.shape[1:], 1) // cols_per_group
        return jnp.where(r == c, tiled, 0.0)

    def b_blk(b):
        return block_diag(b.transpose(0, 1, 3, 2).reshape(tiles, S5_UW, S5_P), S5_C, S5_P)

    def c_blk(c):
        return block_diag(c.transpose(0, 1, 3, 2).reshape(tiles, S5_HW, S5_C), S5_P, S5_C)

    ls = jnp.broadcast_to(log_step[:, :, None], (DEPTH, S5_G, S5_P))
    return (chan(lam_re), chan(lam_im), chan(ls), b_blk(b_re), b_blk(b_im), c_blk(c_re), c_blk(c_im),
            d.reshape(tiles, 1, S5_UW))


def _s5_param_specs(tile_of):
    chan = pl.BlockSpec((None, 1, S5_HW), lambda *g: (tile_of(*g), 0, 0))
    bspec = pl.BlockSpec((None, S5_UW, S5_HW), lambda *g: (tile_of(*g), 0, 0))
    cspec = pl.BlockSpec((None, S5_HW, S5_UW), lambda *g: (tile_of(*g), 0, 0))
    dspec = pl.BlockSpec((None, 1, S5_UW), lambda *g: (tile_of(*g), 0, 0))
    return [chan, chan, chan, bspec, bspec, cspec, cspec, dspec]


def _s5_prompt(q, params, layer):
    col0 = COL_S5 // S5_UW
    state = jax.ShapeDtypeStruct((BATCH, 1, S5_G * S5_P), F32)
    st_spec = pl.BlockSpec((None, 1, S5_HW), lambda b, g: (b, 0, g))
    return pl.pallas_call(
        _s5_prompt_kernel,
        grid=(BATCH, S5_TILES),
        in_specs=[pl.BlockSpec((SEQ, S5_UW), lambda b, g: (b, col0 + g))]
                 + _s5_param_specs(lambda b, g: layer * S5_TILES + g),
        out_specs=[pl.BlockSpec((SEQ, S5_UW), lambda b, g: (b, g)), st_spec, st_spec],
        out_shape=[jax.ShapeDtypeStruct((M_PROMPT, S5W), F32), state, state],
        scratch_shapes=[pltpu.VMEM((SEQ, S5_HW), F32), pltpu.VMEM((SEQ, S5_HW), F32)],
        compiler_params=_cparams(("parallel", "parallel"), 48),
        name="s5_prompt",
    )(q, *params)


def _s5_sample(q, h0_re, h0_im, params, layer):
    col0 = COL_S5 // S5_UW
    state = jax.ShapeDtypeStruct((DEC_BATCH, S5_G * S5_P), F32)
    st_spec = pl.BlockSpec((DEC_BATCH, S5_HW), lambda g: (0, g))
    return pl.pallas_call(
        _s5_sample_kernel,
        grid=(S5_TILES,),
        in_specs=[pl.BlockSpec((M_SAMPLE, S5_UW), lambda g: (0, col0 + g)), st_spec, st_spec]
                 + _s5_param_specs(lambda g: layer * S5_TILES + g),
        out_specs=[pl.BlockSpec((M_SAMPLE, S5_UW), lambda g: (0, g)), st_spec, st_spec],
        out_shape=[jax.ShapeDtypeStruct((M_SAMPLE, S5W), F32), state, state],
        scratch_shapes=[pltpu.VMEM((M_SAMPLE, S5_HW), F32), pltpu.VMEM((M_SAMPLE, S5_HW), F32)],
        compiler_params=_cparams(("parallel",), 32),
        name="s5_sample",
    )(q, h0_re, h0_im, *params)


def _glu_norm_kernel(y_ref, w_ref, b_ref, g_ref, *rest):
    o_ref = rest[-1]
    y = y_ref[...]
    z = 0.5 * y * (1.0 + lax.erf(y * math.sqrt(0.5)))
    gate = jax.nn.sigmoid(jnp.dot(z.astype(BF16), w_ref[...], preferred_element_type=F32) + b_ref[...])
    out = z * gate
    ms = jnp.mean(out * out, axis=-1, keepdims=True)
    o_ref[...] = (out * lax.rsqrt(ms + EPS_RMS) * g_ref[...]).astype(o_ref.dtype)


def _glu_norm(y, w_glu_bf16, layer, b_glu, gain, row0, joined=None, tm=ROW_TM):
    m = y.shape[0]
    vec = pl.BlockSpec((1, S5W), lambda i: (0, 0))
    in_specs, args, aliases = _joined_out(
        [pl.BlockSpec((tm, S5W), lambda i: (i, 0)),
         pl.BlockSpec((None, S5W, S5W), lambda i: (layer, 0, 0)), vec, vec],
        [y, w_glu_bf16, b_glu.reshape(1, S5W), gain.reshape(1, S5W)], joined)
    return pl.pallas_call(
        _glu_norm_kernel,
        grid=(m // tm,),
        in_specs=in_specs,
        out_specs=pl.BlockSpec((tm, S5W), lambda i: (i + row0 // tm, 0)),
        out_shape=jax.ShapeDtypeStruct((M_ALL, S5W), BF16),
        input_output_aliases=aliases,
        compiler_params=_cparams(("parallel",), 56),
        name="s5_glu_norm",
    )(*args)


def kernel(x_prompt, x_sample, state_shift, state_wkv, state_ssm_re, state_ssm_im, ffn1_norm, ffn1_w_gate, ffn1_w_up, ffn1_w_down, mix_norm, w_in, shift_mu, rw_w0, rw_w_up, rw_a0, rw_a_up, rw_g_up, rw_k_k, rw_k_a, rw_r_k, rw_lnx_w, rw_lnx_b, s5_lam_re, s5_lam_im, s5_b_re, s5_b_im, s5_c_re, s5_c_im, s5_d, s5_log_step, s5_w_glu, s5_b_glu, s5_out_norm, w_out, ffn2_norm, ffn2_w_gate, ffn2_w_up, ffn2_w_down, final_norm):
    x = jnp.concatenate([x_prompt.reshape(M_PROMPT, D_MODEL),
                         x_sample.transpose(1, 0, 2).reshape(M_SAMPLE, D_MODEL)], axis=0)
    zero_state = jnp.zeros((HEAD, HEAD, BATCH * HEADS), F32)
    wkv_s0_rows, wkv_s_rows = _sample_state_to_rows(state_wkv), None
    shift_p, wkv_p, re_p, im_p, shift_s, wkv_s, re_s, im_s = ([] for _ in range(8))
    w_glu = s5_w_glu.astype(BF16)
    s5p = _s5_params(s5_lam_re, s5_lam_im, s5_log_step, s5_b_re, s5_b_im, s5_c_re, s5_c_im, s5_d)
    w_main, w_tail, mu_pad = _in_proj_tail(w_in, shift_mu)

    for l in range(DEPTH):
        x = _ffn(x, ffn1_norm[l], ffn1_w_gate, ffn1_w_up, ffn1_w_down, l)

        h_mix = _rmsnorm(x, mix_norm[l], BF16, ROW_TM)
        last8 = _rmsnorm(x, mix_norm[l], F32, SUBLANES, SEQ - SUBLANES, BATCH * SUBLANES, SEQ // SUBLANES)
        shift_p.append(last8[SUBLANES - 1::SUBLANES])
        shift_s.append(_rmsnorm(x, mix_norm[l], F32, DEC_BATCH, M_ALL - DEC_BATCH, DEC_BATCH))

        q_p = _win_prompt(h_mix, w_main, w_tail, mu_pad, l)
        q_s = _win_sample(jnp.concatenate([state_shift[l].astype(BF16), h_mix[M_PROMPT:]], axis=0),
                          w_main, w_tail, mu_pad, l)

        rw_args = (rw_w0[l], rw_a0[l], rw_k_k[l], rw_k_a[l], rw_r_k[l], rw_w_up[l], rw_a_up[l], rw_g_up[l])
        *scan_p, gate_p, bonus_p = _rwkv_pre(q_p, *rw_args, time_major=True, tm=LANES)
        scan_s, gate_s, bonus_s = _rwkv_pre(q_s, *rw_args, time_major=False)
        scan_p = [_to_scan_channel_major(a) for a in scan_p]
        scan_s = _to_scan(scan_s, 6 * DEC_SEQ).reshape(6, DEC_SEQ, HEAD, DEC_BATCH * HEADS)

        y5_p, hre_p, him_p = _s5_prompt(q_p, s5p, l)
        y5_s, hre_s, him_s = _s5_sample(q_s, state_ssm_re[l].reshape(DEC_BATCH, S5_G * S5_P),
                                        state_ssm_im[l].reshape(DEC_BATCH, S5_G * S5_P), s5p, l)
        re_p.append(hre_p.reshape(BATCH, S5_G, S5_P))
        im_p.append(him_p.reshape(BATCH, S5_G, S5_P))
        re_s.append(hre_s.reshape(DEC_BATCH, S5_G, S5_P))
        im_s.append(him_s.reshape(DEC_BATCH, S5_G, S5_P))
        y_s5 = _glu_norm(y5_p, w_glu, l, s5_b_glu[l], s5_out_norm[l], 0)
        y_s5 = _glu_norm(y5_s, w_glu, l, s5_b_glu[l], s5_out_norm[l], M_PROMPT, y_s5)

        y_p, s_p = _wkv_scan(scan_p, zero_state, y_s5, tc=WKV_CHUNK)
        y_s, wkv_s_rows = _wkv_rows(scan_s, wkv_s0_rows, l, y_s5, wkv_s_rows)
        wkv_p.append(_state_from_scan(s_p, BATCH))
        y_rw = _rwkv_post(_from_scan(y_p, SEQ), bonus_p, gate_p, rw_lnx_w[l], rw_lnx_b[l], 0, True)
        y_rw = _rwkv_post(_from_scan(y_s, DEC_SEQ).reshape(M_SAMPLE, RW), bonus_s, gate_s,
                          rw_lnx_w[l], rw_lnx_b[l], M_PROMPT, False, y_rw)

        x = _res_matmul([y_rw, y_s5], w_out, l, x, 1.0, tm=1088, tn=512, vmem_mib=56, cast_w=True)
        x = _ffn(x, ffn2_norm[l], ffn2_w_gate, ffn2_w_up, ffn2_w_down, l)

    y_prompt = _rmsnorm(x, final_norm, F32, ROW_TM, 0, M_PROMPT).reshape(BATCH, SEQ, D_MODEL)
    y_sample = _rmsnorm(x, final_norm, F32, ROW_TM, M_PROMPT, M_SAMPLE).reshape(
        DEC_SEQ, DEC_BATCH, D_MODEL).transpose(1, 0, 2)
    st = jnp.stack
    return (y_prompt, y_sample, st(shift_p), st(wkv_p), st(re_p), st(im_p),
            st(shift_s), _sample_state_from_rows(wkv_s_rows), st(re_s), st(im_s))
```

```python
import functools
import math

import jax
import jax.numpy as jnp
from jax import lax
from jax.experimental import pallas as pl
from jax.experimental.pallas import tpu as pltpu

F32 = jnp.float32
BF16 = jnp.bfloat16

D_MODEL = 4096
BATCH = 4
SEQ = 2048
DEPTH = 2
DEC_BATCH = 128
DEC_SEQ = 4
M_PROMPT = BATCH * SEQ
M_SAMPLE = DEC_BATCH * DEC_SEQ
M_ALL = M_PROMPT + M_SAMPLE

RW = D_MODEL // 2
HEAD = 64
HEADS = RW // HEAD
S5W = D_MODEL - RW
S5_C = 16
S5_G = S5W // S5_C
S5_P = 64
W_LORA = 96
A_LORA = 96
G_LORA = 256
LORA_PAD = 128
LORA_W = 2 * LORA_PAD + G_LORA
D_FF = 11008
EPS_RMS = 1e-6
EPS_GN = 64e-5

COL_LORA = 3 * RW
COL_S5 = COL_LORA + LORA_W
D_INP = COL_S5 + S5W

V7X_VMEM_BYTES = 64 * 1024 * 1024
LANES = 128
SUBLANES = 8

S5_GT = 8
S5_TILES = S5_G // S5_GT
S5_UW = S5_GT * S5_C
S5_HW = S5_GT * S5_P

ROW_TM = 512
PRE_TM = 256


def _cparams(semantics, vmem_mib):
    assert vmem_mib * 1024 * 1024 < V7X_VMEM_BYTES
    return pltpu.CompilerParams(dimension_semantics=semantics,
                                vmem_limit_bytes=vmem_mib * 1024 * 1024)


def _joined_out(in_specs, args, joined):
    if joined is None:
        return in_specs, args, {}
    return (in_specs + [pl.BlockSpec(memory_space=pl.ANY)], args + [joined], {len(args): 0})


def _rmsnorm_kernel(x_ref, g_ref, o_ref):
    x = x_ref[...]
    ms = jnp.mean(x * x, axis=-1, keepdims=True)
    o_ref[...] = (x * lax.rsqrt(ms + EPS_RMS) * g_ref[...]).astype(o_ref.dtype)


def _rmsnorm(x, g, out_dtype, tm, row0=0, rows=None, block_stride=1):
    d = x.shape[1]
    m = x.shape[0] if rows is None else rows
    return pl.pallas_call(
        _rmsnorm_kernel,
        grid=(m // tm,),
        in_specs=[pl.BlockSpec((tm, d), lambda i: (i * block_stride + row0 // tm, 0)),
                  pl.BlockSpec((1, d), lambda i: (0, 0))],
        out_specs=pl.BlockSpec((tm, d), lambda i: (i, 0)),
        out_shape=jax.ShapeDtypeStruct((m, d), out_dtype),
        compiler_params=_cparams(("parallel",), 40),
        name="rmsnorm",
    )(x, g.reshape(1, d))


WD_CAST_ROWS = 128


def _gate_up_kernel(h_ref, wg_ref, wu_ref, wd_ref, o_ref, wd_bf16_ref, *, cast_passes):
    h = h_ref[...]
    a = jnp.dot(h, wg_ref[...].astype(BF16), preferred_element_type=F32)
    b = jnp.dot(h, wu_ref[...].astype(BF16), preferred_element_type=F32)
    o_ref[...] = (a * jax.nn.sigmoid(a) * b).astype(o_ref.dtype)

    @pl.when(pl.program_id(0) < cast_passes)
    def _():
        wd_bf16_ref[...] = wd_ref[...].astype(BF16)


def _gate_up(h, wg, wu, wd, layer, tm=2176, tn=256):
    m, d = h.shape
    f = wg.shape[2]
    n_i, n_j = m // tm, f // tn
    cast_blocks = f // WD_CAST_ROWS
    cast_passes = cast_blocks // n_j
    assert cast_passes * n_j == cast_blocks and cast_passes <= n_i

    def cast_block(i, j):
        return jnp.where(i < cast_passes, i * n_j + j, cast_blocks - 1)

    w_spec = pl.BlockSpec((None, d, tn), lambda i, j: (layer, 0, j))
    return pl.pallas_call(
        functools.partial(_gate_up_kernel, cast_passes=cast_passes),
        grid=(n_i, n_j),
        in_specs=[pl.BlockSpec((tm, d), lambda i, j: (i, 0), pipeline_mode=pl.Buffered(1)), w_spec, w_spec,
                  pl.BlockSpec((None, WD_CAST_ROWS, d), lambda i, j: (layer, cast_block(i, j), 0))],
        out_specs=[pl.BlockSpec((tm, tn), lambda i, j: (i, j)),
                   pl.BlockSpec((WD_CAST_ROWS, d), lambda i, j: (cast_block(i, j), 0))],
        out_shape=[jax.ShapeDtypeStruct((m, f), BF16), jax.ShapeDtypeStruct((f, d), BF16)],
        compiler_params=_cparams(("arbitrary", "arbitrary"), 58),
        name="ffn_gate_up",
    )(h, wg, wu, wd)


def _res_matmul_kernel(*refs, n_pairs, scale, cast_w):
    a_refs = refs[:n_pairs]
    w_refs = refs[n_pairs:2 * n_pairs]
    x_ref, o_ref = refs[2 * n_pairs], refs[2 * n_pairs + 1]
    load_w = (lambda ref: ref[...].astype(BF16)) if cast_w else (lambda ref: ref[...])
    acc = jnp.dot(a_refs[0][...], load_w(w_refs[0]), preferred_element_type=F32)
    for a_ref, w_ref in zip(a_refs[1:], w_refs[1:]):
        acc = acc + jnp.dot(a_ref[...], load_w(w_ref), preferred_element_type=F32)
    if scale != 1.0:
        acc = scale * acc
    o_ref[...] = x_ref[...] + acc


def _res_matmul(a_list, w, layer, x, scale, tm, tn, vmem_mib, cast_w=False):
    m, n = x.shape
    n_pairs = len(a_list)
    in_specs = [pl.BlockSpec((tm, a.shape[1]), lambda i, j: (i, 0)) for a in a_list]
    for p, a in enumerate(a_list):
        if layer is None:
            in_specs.append(pl.BlockSpec((a.shape[1], tn), lambda i, j, p=p: (p, j)))
        else:
            in_specs.append(pl.BlockSpec((None, a.shape[1], tn), lambda i, j, p=p: (layer, p, j)))
    in_specs.append(pl.BlockSpec((tm, tn), lambda i, j: (i, j)))
    assert sum(a.shape[1] for a in a_list) == w.shape[-2]
    return pl.pallas_call(
        functools.partial(_res_matmul_kernel, n_pairs=n_pairs, scale=scale, cast_w=cast_w),
        grid=(m // tm, n // tn),
        in_specs=in_specs,
        out_specs=pl.BlockSpec((tm, tn), lambda i, j: (i, j)),
        out_shape=jax.ShapeDtypeStruct((m, n), F32),
        compiler_params=_cparams(("parallel", "arbitrary"), vmem_mib),
        name="res_matmul",
    )(*a_list, *([w] * n_pairs), x)


def _ffn(x, norm_g, wg, wu, wd, layer):
    h = _rmsnorm(x, norm_g, BF16, ROW_TM)
    act, wd_bf16 = _gate_up(h, wg, wu, wd, layer)
    return _res_matmul([act], wd_bf16, None, x, 0.5, tm=544, tn=512, vmem_mib=58)


WIN_TN = 512
WIN_MAIN_TILES = COL_LORA // WIN_TN
WIN_SHIFT_TILES = COL_S5 // WIN_TN


def _win_tile_cases(h_ref, wm_ref, wt_ref, j, emit):
    @pl.when(j < WIN_MAIN_TILES)
    def _():
        emit(jnp.dot(h_ref[...], wm_ref[...], preferred_element_type=F32), True)

    @pl.when((j >= WIN_MAIN_TILES) & (j < WIN_SHIFT_TILES))
    def _():
        emit(jnp.dot(h_ref[...], wt_ref[...], preferred_element_type=F32), True)

    @pl.when(j >= WIN_SHIFT_TILES)
    def _():
        emit(jnp.dot(h_ref[...], wt_ref[...], preferred_element_type=F32), False)


def _win_prompt_kernel(h_ref, wm_ref, wt_ref, mu_ref, o_ref):
    def emit(p, shifted):
        if shifted:
            rows = lax.broadcasted_iota(jnp.int32, p.shape, 0)
            prev = jnp.where(rows == 0, 0.0, pltpu.roll(p, 1, 0))
            o_ref[...] = p + (prev - p) * mu_ref[...]
        else:
            o_ref[...] = p

    _win_tile_cases(h_ref, wm_ref, wt_ref, pl.program_id(1), emit)


def _win_sample_kernel(h_ref, wm_ref, wt_ref, mu_ref, o_ref):
    def emit(p, shifted):
        cur = p[DEC_BATCH:]
        o_ref[...] = cur + (p[:M_SAMPLE] - cur) * mu_ref[...] if shifted else cur

    _win_tile_cases(h_ref, wm_ref, wt_ref, pl.program_id(0), emit)


def _win_weight_specs(layer, col_of):
    main = pl.BlockSpec((None, D_MODEL, WIN_TN),
                        lambda *g: (layer, 0, jnp.minimum(col_of(*g), WIN_MAIN_TILES - 1)))
    tail = pl.BlockSpec((None, D_MODEL, WIN_TN),
                        lambda *g: (layer, 0, jnp.maximum(col_of(*g) - WIN_MAIN_TILES, 0)))
    mu = pl.BlockSpec((None, 1, WIN_TN), lambda *g: (layer, 0, col_of(*g)))
    return [main, tail, mu]


def _win_prompt(h, w_main, w_tail, mu_pad, layer):
    tn = WIN_TN
    return pl.pallas_call(
        _win_prompt_kernel,
        grid=(BATCH, D_INP // tn),
        in_specs=[pl.BlockSpec((SEQ, D_MODEL), lambda b, j: (b, 0), pipeline_mode=pl.Buffered(1))]
                 + _win_weight_specs(layer, lambda b, j: j),
        out_specs=pl.BlockSpec((SEQ, tn), lambda b, j: (b, j)),
        out_shape=jax.ShapeDtypeStruct((M_PROMPT, D_INP), F32),
        compiler_params=_cparams(("parallel", "arbitrary"), 60),
        name="in_proj_prompt",
    )(h, w_main, w_tail, mu_pad)


def _win_sample(h_rows, w_main, w_tail, mu_pad, layer):
    tn = WIN_TN
    rows = h_rows.shape[0]
    return pl.pallas_call(
        _win_sample_kernel,
        grid=(D_INP // tn,),
        in_specs=[pl.BlockSpec((rows, D_MODEL), lambda j: (0, 0))] + _win_weight_specs(layer, lambda j: j),
        out_specs=pl.BlockSpec((M_SAMPLE, tn), lambda j: (0, j)),
        out_shape=jax.ShapeDtypeStruct((M_SAMPLE, D_INP), F32),
        compiler_params=_cparams(("arbitrary",), 48),
        name="in_proj_sample",
    )(h_rows, w_main, w_tail, mu_pad)


def _in_proj_tail(w_in, mu):
    o_w, o_a, o_g = COL_LORA, COL_LORA + W_LORA, COL_LORA + W_LORA + A_LORA
    o_s = o_g + G_LORA
    w_main = w_in.astype(BF16)
    zw = jnp.zeros((DEPTH, D_MODEL, LORA_PAD - W_LORA), BF16)
    w_tail = jnp.concatenate([w_main[:, :, o_w:o_a], zw, w_main[:, :, o_a:o_g], zw,
                              w_main[:, :, o_g:]], axis=2)
    zm = jnp.zeros((DEPTH, LORA_PAD - W_LORA), mu.dtype)
    mu_pad = jnp.concatenate([mu[:, :o_w], mu[:, o_w:o_a], zm, mu[:, o_a:o_g], zm, mu[:, o_g:o_s],
                              jnp.zeros((DEPTH, S5W), mu.dtype)], axis=1).reshape(DEPTH, 1, D_INP)
    return w_main, w_tail, mu_pad


def _head_sum(x):
    r = lax.broadcasted_iota(jnp.int32, (LANES, LANES), 0) // HEAD
    c = lax.broadcasted_iota(jnp.int32, (LANES, LANES), 1) // HEAD
    ones = (r == c).astype(BF16)
    hi = x.astype(BF16)
    r1 = x - hi.astype(F32)
    mid = r1.astype(BF16)
    lo = (r1 - mid.astype(F32)).astype(BF16)
    outs = []
    for s in range(x.shape[1] // LANES):
        sl = slice(s * LANES, (s + 1) * LANES)
        acc = jnp.dot(hi[:, sl], ones, preferred_element_type=F32)
        acc = acc + jnp.dot(mid[:, sl], ones, preferred_element_type=F32)
        acc = acc + jnp.dot(lo[:, sl], ones, preferred_element_type=F32)
        outs.append(acc)
    return jnp.concatenate(outs, axis=1)


def _rwkv_pre_kernel(r_ref, k_ref, v_ref, lora_ref, w0_ref, a0_ref, kk_ref, ka_ref, rk_ref,
                     wup_ref, aup_ref, gup_ref, *outs, channel_major):
    *scan_outs, g_out, bonus_out = outs
    r_out, w_out, k_out, v_out, kk_out, b_out = range(6)

    def emit(idx, val):
        if channel_major:
            scan_outs[idx][...] = val.T
        else:
            scan_outs[0][idx] = val

    r = r_ref[...]
    k = k_ref[...]
    v = v_ref[...]
    emit(r_out, r)
    emit(v_out, v)
    lora = lora_ref[...]
    wd = lora[:, :LORA_PAD]
    ad = lora[:, LORA_PAD:2 * LORA_PAD]
    gd = lora[:, 2 * LORA_PAD:]
    z = -(w0_ref[...] + jnp.dot(jnp.tanh(wd).astype(BF16), wup_ref[...], preferred_element_type=F32))
    softplus = jnp.maximum(z, 0.0) + jnp.log1p(jnp.exp(-jnp.abs(z)))
    emit(w_out, jnp.exp(-jnp.exp(-softplus - 0.5)))
    a = jax.nn.sigmoid(a0_ref[...] + jnp.dot(ad.astype(BF16), aup_ref[...], preferred_element_type=F32))
    g_out[...] = jnp.dot(jax.nn.sigmoid(gd).astype(BF16), gup_ref[...], preferred_element_type=F32)
    kk = k * kk_ref[...]
    k2 = k * (1.0 + (a - 1.0) * ka_ref[...])
    emit(k_out, k2)
    kkn = kk / jnp.maximum(jnp.sqrt(_head_sum(kk * kk)), 1e-12)
    emit(kk_out, kkn)
    emit(b_out, kkn * a)
    bonus_out[...] = _head_sum(r * k2 * rk_ref[...]) * v


def _time_major_spec(m, tm, time_major):
    if not time_major:
        return pl.BlockSpec((tm, RW), lambda i: (i, 0)), (m, RW)
    per_seq = SEQ // tm
    return pl.BlockSpec((tm, RW), lambda i: (i % per_seq, i // per_seq)), (SEQ, (m // SEQ) * RW)


def _rwkv_pre(q, w0, a0, k_k, k_a, r_k, w_up, a_up, g_up, time_major, tm=PRE_TM):
    m = q.shape[0]
    out_spec, out_dims = _time_major_spec(m, tm, time_major)
    if time_major:
        per_seq = SEQ // tm
        scan_specs = [pl.BlockSpec((RW, tm), lambda i: (i // per_seq, i % per_seq))] * 6
        scan_outs = [jax.ShapeDtypeStruct(((m // SEQ) * RW, SEQ), F32)] * 6
    else:
        scan_specs = [pl.BlockSpec((6, tm, RW), lambda i: (0, i, 0))]
        scan_outs = [jax.ShapeDtypeStruct((6, m, RW), F32)]
    row = lambda v: v.reshape(1, RW)
    pad_rows = lambda u: jnp.concatenate(
        [u, jnp.zeros((LORA_PAD - u.shape[0], RW), u.dtype)], axis=0).astype(BF16)
    vec = pl.BlockSpec((1, RW), lambda i: (0, 0))
    full = lambda rows: pl.BlockSpec((rows, RW), lambda i: (0, 0))
    out = jax.ShapeDtypeStruct(out_dims, F32)
    return pl.pallas_call(
        functools.partial(_rwkv_pre_kernel, channel_major=time_major),
        grid=(m // tm,),
        in_specs=[pl.BlockSpec((tm, RW), lambda i: (i, 0)),
                  pl.BlockSpec((tm, RW), lambda i: (i, 1)),
                  pl.BlockSpec((tm, RW), lambda i: (i, 2)),
                  pl.BlockSpec((tm, LORA_W), lambda i: (i, COL_LORA // LORA_W)),
                  vec, vec, vec, vec, vec, full(LORA_PAD), full(LORA_PAD), full(G_LORA)],
        out_specs=scan_specs + [out_spec] * 2,
        out_shape=scan_outs + [out] * 2,
        compiler_params=_cparams(("parallel",), 56),
        name="rwkv_pre",
    )(q, q, q, q, row(w0), row(a0), row(k_k), row(k_a), row(r_k.reshape(RW)),
      pad_rows(w_up), pad_rows(a_up), g_up.astype(BF16))


def _rwkv_post_kernel(y_ref, bonus_ref, g_ref, lw_ref, lb_ref, *rest):
    o_ref = rest[-1]
    y = y_ref[...]
    mean = _head_sum(y) * (1.0 / HEAD)
    c = y - mean
    var = _head_sum(c * c) * (1.0 / HEAD)
    yn = c * lax.rsqrt(var + EPS_GN) * lw_ref[...] + lb_ref[...]
    o_ref[...] = ((yn + bonus_ref[...]) * g_ref[...]).astype(o_ref.dtype)


def _rwkv_post(y, bonus, g, lnx_w, lnx_b, row0, time_major, joined=None, tm=ROW_TM):
    m = y.size // RW
    blk, _ = _time_major_spec(m, tm, time_major)
    vec = pl.BlockSpec((1, RW), lambda i: (0, 0))
    in_specs, args, aliases = _joined_out(
        [blk, blk, blk, vec, vec], [y, bonus, g, lnx_w.reshape(1, RW), lnx_b.reshape(1, RW)], joined)
    return pl.pallas_call(
        _rwkv_post_kernel,
        grid=(m // tm,),
        in_specs=in_specs,
        out_specs=pl.BlockSpec((tm, RW), lambda i: (i + row0 // tm, 0)),
        out_shape=jax.ShapeDtypeStruct((M_ALL, RW), BF16),
        input_output_aliases=aliases,
        compiler_params=_cparams(("parallel",), 56),
        name="rwkv_post",
    )(*args)


WKV_UNROLL = 8
WKV_CHUNK = 64


def _wkv_kernel(r_ref, w_ref, k_ref, v_ref, kk_ref, b_ref, s0_ref, after_ref, y_ref, s_ref,
                g_ref, kq_ref, wr_ref, bt_ref, kt_ref, *, steps):
    @pl.when(pl.program_id(1) == 0)
    def _():
        s_ref[...] = s0_ref[...]

    g_ref[...] = jnp.ones((HEAD, LANES), F32)

    def step(t, carry):
        r, k, b = r_ref[t], k_ref[t], b_ref[t]
        g_prev = g_ref[...]
        g = g_prev * w_ref[t]
        g_inv = 1.0 / g
        g_ref[...] = g
        kq_ref[...] = g_prev * kk_ref[t]
        wr_ref[...] = g * r
        bt_ref[...] = b * g_inv
        kt_ref[...] = k * g_inv
        beta = jnp.sum(b * r, axis=0, keepdims=True)
        kappa = jnp.sum(k * r, axis=0, keepdims=True)
        v = v_ref[t]

        def contract(j, acc):
            sa, u = acc
            sj = s_ref[j]
            return sa - sj * kq_ref[pl.ds(j, 1), :], u + sj * wr_ref[pl.ds(j, 1), :]

        zero = jnp.zeros((HEAD, LANES), F32)
        sa, u = lax.fori_loop(0, HEAD, contract, (zero, zero), unroll=WKV_UNROLL)

        def update(j, c):
            s_ref[j] = s_ref[j] + sa * bt_ref[pl.ds(j, 1), :] + v * kt_ref[pl.ds(j, 1), :]
            return c

        lax.fori_loop(0, HEAD, update, 0, unroll=WKV_UNROLL)
        y_ref[t] = u + sa * beta + v * kappa
        return carry

    lax.fori_loop(0, steps, step, 0)

    def denormalise(j, c):
        s_ref[j] = s_ref[j] * g_ref[pl.ds(j, 1), :]
        return c

    lax.fori_loop(0, HEAD, denormalise, 0, unroll=WKV_UNROLL)


def _wkv_scan(vectors, s0, after, tc):
    vec = pl.BlockSpec((tc, HEAD, LANES), lambda c, t: (t, 0, c))
    if isinstance(vectors, (list, tuple)):
        r, w, k, v, kk, b = vectors
        vec_specs = [vec] * 6
    else:
        r = w = k = v = kk = b = vectors
        vec_specs = [pl.BlockSpec((None, tc, HEAD, LANES), lambda c, t, i=i: (i, t, 0, c)) for i in range(6)]
    length, _, n = r.shape[-3:]
    st = pl.BlockSpec((HEAD, HEAD, LANES), lambda c, t: (0, 0, c))
    return pl.pallas_call(
        functools.partial(_wkv_kernel, steps=tc),
        grid=(n // LANES, length // tc),
        in_specs=vec_specs + [st, pl.BlockSpec(memory_space=pl.ANY)],
        out_specs=[vec, st],
        out_shape=[jax.ShapeDtypeStruct((length, HEAD, n), F32),
                   jax.ShapeDtypeStruct((HEAD, HEAD, n), F32)],
        scratch_shapes=[pltpu.VMEM((HEAD, LANES), F32)] * 5,
        compiler_params=_cparams(("parallel", "arbitrary"), 40),
        name="wkv_scan",
    )(r, w, k, v, kk, b, s0, after)


def _to_scan(a, length):
    return a.reshape(length, -1, HEAD).transpose(0, 2, 1)


def _to_scan_channel_major(a):
    return a.reshape(-1, HEAD, a.shape[1]).transpose(2, 1, 0)


def _from_scan(y, length):
    return y.transpose(0, 2, 1).reshape(length, -1)


def _state_to_scan(s):
    n = s.shape[0] * s.shape[1]
    s = lax.optimization_barrier(s.reshape(n, HEAD, HEAD).transpose(0, 2, 1))
    return s.reshape(n, HEAD * HEAD).T.reshape(HEAD, HEAD, n)


def _state_from_scan(s, batch):
    n = s.shape[2]
    s = lax.optimization_barrier(s.reshape(HEAD * HEAD, n).T)
    return s.reshape(n, HEAD, HEAD).transpose(0, 2, 1).reshape(batch, HEADS, HEAD, HEAD)


def _s5_discretize(lre_ref, lim_ref, ls_ref, bre_ref, bim_ref):
    lre, lim = lre_ref[...], lim_ref[...]
    step = jnp.exp(ls_ref[...])
    mag = jnp.exp(lre * step)
    ar = mag * jnp.cos(lim * step)
    ai = mag * jnp.sin(lim * step)
    den = lre * lre + lim * lim
    nr = ar - 1.0
    f_re = (nr * lre + ai * lim) / den
    f_im = (ai * lre - nr * lim) / den
    bre, bim = bre_ref[...], bim_ref[...]
    return ar, ai, f_re * bre - f_im * bim, f_re * bim + f_im * bre


def _dot_bf16(a, b):
    return jnp.dot(a.astype(BF16), b.astype(BF16), preferred_element_type=F32)


def _s5_readout(hr, hi, cre_ref, cim_ref, d_ref, u):
    return _dot_bf16(hr, cre_ref[...]) - _dot_bf16(hi, cim_ref[...]) + d_ref[...] * u


def _cmul(ar, ai, br, bi):
    return ar * br - ai * bi, ar * bi + ai * br


def _s5_prompt_kernel(u_ref, lre_ref, lim_ref, ls_ref, bre_ref, bim_ref, cre_ref, cim_ref, d_ref,
                      y_ref, hre_ref, him_ref, hr_scr, hi_scr):
    ar, ai, bbr, bbi = _s5_discretize(lre_ref, lim_ref, ls_ref, bre_ref, bim_ref)
    u = u_ref[...]
    hr_scr[...] = _dot_bf16(u, bbr)
    hi_scr[...] = _dot_bf16(u, bbi)

    rows = lax.broadcasted_iota(jnp.int32, (SUBLANES, S5_HW), 0)
    powers = [(ar, ai)]
    for _ in range(SUBLANES - 1):
        powers.append(_cmul(*powers[-1], ar, ai))
    levels = []
    for sh in (1, 2, 4):
        pr, pi = powers[sh - 1]
        levels.append((sh, jnp.where(rows >= sh, pr, 0.0), jnp.where(rows >= sh, pi, 0.0)))
    cpr = jnp.zeros((SUBLANES, S5_HW), F32)
    cpi = jnp.zeros((SUBLANES, S5_HW), F32)
    for n, (pr, pi) in enumerate(powers):
        cpr = jnp.where(rows == n, pr, cpr)
        cpi = jnp.where(rows == n, pi, cpi)

    def tile(i, carry):
        cr, ci = carry
        r0 = pl.multiple_of(i * SUBLANES, SUBLANES)
        xr = hr_scr[pl.ds(r0, SUBLANES), :]
        xi = hi_scr[pl.ds(r0, SUBLANES), :]
        for sh, mr, mi in levels:
            sr, si = pltpu.roll(xr, sh, 0), pltpu.roll(xi, sh, 0)
            xr, xi = xr + (mr * sr - mi * si), xi + (mr * si + mi * sr)
        xr = xr + (cpr * cr - cpi * ci)
        xi = xi + (cpr * ci + cpi * cr)
        hr_scr[pl.ds(r0, SUBLANES), :] = xr
        hi_scr[pl.ds(r0, SUBLANES), :] = xi
        last = SUBLANES - 1
        return (jnp.broadcast_to(xr[last:, :], (SUBLANES, S5_HW)),
                jnp.broadcast_to(xi[last:, :], (SUBLANES, S5_HW)))

    zero = jnp.zeros((SUBLANES, S5_HW), F32)
    cr, ci = lax.fori_loop(0, SEQ // SUBLANES, tile, (zero, zero), unroll=4)
    hre_ref[...] = cr[:1]
    him_ref[...] = ci[:1]
    y_ref[...] = _s5_readout(hr_scr[...], hi_scr[...], cre_ref, cim_ref, d_ref, u)


def _s5_sample_kernel(u_ref, h0r_ref, h0i_ref, lre_ref, lim_ref, ls_ref, bre_ref, bim_ref, cre_ref, cim_ref,
                      d_ref, y_ref, hre_ref, him_ref, hr_scr, hi_scr):
    ar, ai, bbr, bbi = _s5_discretize(lre_ref, lim_ref, ls_ref, bre_ref, bim_ref)
    u = u_ref[...]
    bu_r = _dot_bf16(u, bbr)
    bu_i = _dot_bf16(u, bbi)
    hr, hi = h0r_ref[...], h0i_ref[...]
    for t in range(DEC_SEQ):
        rows = slice(t * DEC_BATCH, (t + 1) * DEC_BATCH)
        hr, hi = ar * hr - ai * hi + bu_r[rows], ar * hi + ai * hr + bu_i[rows]
        hr_scr[rows, :] = hr
        hi_scr[rows, :] = hi
    hre_ref[...] = hr
    him_ref[...] = hi
    y_ref[...] = _s5_readout(hr_scr[...], hi_scr[...], cre_ref, cim_ref, d_ref, u)


def _s5_params(lam_re, lam_im, log_step, b_re, b_im, c_re, c_im, d):
    tiles = DEPTH * S5_TILES
    chan = lambda a: a.reshape(tiles, 1, S5_HW)

    def block_diag(x, rows_per_group, cols_per_group):
        tiled = jnp.tile(x, (1, 1, S5_GT))
        r = lax.broadcasted_iota(jnp.int32, tiled.shape[1:], 0) // rows_per_group
        c = lax.broadcasted_iota(jnp.int32, tiled.shape[1:], 1) // cols_per_group
        return jnp.where(r == c, tiled, 0.0)

    def b_blk(b):
        return block_diag(b.transpose(0, 1, 3, 2).reshape(tiles, S5_UW, S5_P), S5_C, S5_P)

    def c_blk(c):
        return block_diag(c.transpose(0, 1, 3, 2).reshape(tiles, S5_HW, S5_C), S5_P, S5_C)

    ls = jnp.broadcast_to(log_step[:, :, None], (DEPTH, S5_G, S5_P))
    return (chan(lam_re), chan(lam_im), chan(ls), b_blk(b_re), b_blk(b_im), c_blk(c_re), c_blk(c_im),
            d.reshape(tiles, 1, S5_UW))


def _s5_param_specs(tile_of):
    chan = pl.BlockSpec((None, 1, S5_HW), lambda *g: (tile_of(*g), 0, 0))
    bspec = pl.BlockSpec((None, S5_UW, S5_HW), lambda *g: (tile_of(*g), 0, 0))
    cspec = pl.BlockSpec((None, S5_HW, S5_UW), lambda *g: (tile_of(*g), 0, 0))
    dspec = pl.BlockSpec((None, 1, S5_UW), lambda *g: (tile_of(*g), 0, 0))
    return [chan, chan, chan, bspec, bspec, cspec, cspec, dspec]


def _s5_prompt(q, params, layer):
    col0 = COL_S5 // S5_UW
    state = jax.ShapeDtypeStruct((BATCH, 1, S5_G * S5_P), F32)
    st_spec = pl.BlockSpec((None, 1, S5_HW), lambda b, g: (b, 0, g))
    return pl.pallas_call(
        _s5_prompt_kernel,
        grid=(BATCH, S5_TILES),
        in_specs=[pl.BlockSpec((SEQ, S5_UW), lambda b, g: (b, col0 + g))]
                 + _s5_param_specs(lambda b, g: layer * S5_TILES + g),
        out_specs=[pl.BlockSpec((SEQ, S5_UW), lambda b, g: (b, g)), st_spec, st_spec],
        out_shape=[jax.ShapeDtypeStruct((M_PROMPT, S5W), F32), state, state],
        scratch_shapes=[pltpu.VMEM((SEQ, S5_HW), F32), pltpu.VMEM((SEQ, S5_HW), F32)],
        compiler_params=_cparams(("parallel", "parallel"), 48),
        name="s5_prompt",
    )(q, *params)


def _s5_sample(q, h0_re, h0_im, params, layer):
    col0 = COL_S5 // S5_UW
    state = jax.ShapeDtypeStruct((DEC_BATCH, S5_G * S5_P), F32)
    st_spec = pl.BlockSpec((DEC_BATCH, S5_HW), lambda g: (0, g))
    return pl.pallas_call(
        _s5_sample_kernel,
        grid=(S5_TILES,),
        in_specs=[pl.BlockSpec((M_SAMPLE, S5_UW), lambda g: (0, col0 + g)), st_spec, st_spec]
                 + _s5_param_specs(lambda g: layer * S5_TILES + g),
        out_specs=[pl.BlockSpec((M_SAMPLE, S5_UW), lambda g: (0, g)), st_spec, st_spec],
        out_shape=[jax.ShapeDtypeStruct((M_SAMPLE, S5W), F32), state, state],
        scratch_shapes=[pltpu.VMEM((M_SAMPLE, S5_HW), F32), pltpu.VMEM((M_SAMPLE, S5_HW), F32)],
        compiler_params=_cparams(("parallel",), 32),
        name="s5_sample",
    )(q, h0_re, h0_im, *params)


def _glu_norm_kernel(y_ref, w_ref, b_ref, g_ref, *rest):
    o_ref = rest[-1]
    y = y_ref[...]
    z = 0.5 * y * (1.0 + lax.erf(y * math.sqrt(0.5)))
    gate = jax.nn.sigmoid(jnp.dot(z.astype(BF16), w_ref[...], preferred_element_type=F32) + b_ref[...])
    out = z * gate
    ms = jnp.mean(out * out, axis=-1, keepdims=True)
    o_ref[...] = (out * lax.rsqrt(ms + EPS_RMS) * g_ref[...]).astype(o_ref.dtype)


def _glu_norm(y, w_glu_bf16, layer, b_glu, gain, row0, joined=None, tm=ROW_TM):
    m = y.shape[0]
    vec = pl.BlockSpec((1, S5W), lambda i: (0, 0))
    in_specs, args, aliases = _joined_out(
        [pl.BlockSpec((tm, S5W), lambda i: (i, 0)),
         pl.BlockSpec((None, S5W, S5W), lambda i: (layer, 0, 0)), vec, vec],
        [y, w_glu_bf16, b_glu.reshape(1, S5W), gain.reshape(1, S5W)], joined)
    return pl.pallas_call(
        _glu_norm_kernel,
        grid=(m // tm,),
        in_specs=in_specs,
        out_specs=pl.BlockSpec((tm, S5W), lambda i: (i + row0 // tm, 0)),
        out_shape=jax.ShapeDtypeStruct((M_ALL, S5W), BF16),
        input_output_aliases=aliases,
        compiler_params=_cparams(("parallel",), 56),
        name="s5_glu_norm",
    )(*args)


def kernel(x_prompt, x_sample, state_shift, state_wkv, state_ssm_re, state_ssm_im, ffn1_norm, ffn1_w_gate, ffn1_w_up, ffn1_w_down, mix_norm, w_in, shift_mu, rw_w0, rw_w_up, rw_a0, rw_a_up, rw_g_up, rw_k_k, rw_k_a, rw_r_k, rw_lnx_w, rw_lnx_b, s5_lam_re, s5_lam_im, s5_b_re, s5_b_im, s5_c_re, s5_c_im, s5_d, s5_log_step, s5_w_glu, s5_b_glu, s5_out_norm, w_out, ffn2_norm, ffn2_w_gate, ffn2_w_up, ffn2_w_down, final_norm):
    x = jnp.concatenate([x_prompt.reshape(M_PROMPT, D_MODEL),
                         x_sample.transpose(1, 0, 2).reshape(M_SAMPLE, D_MODEL)], axis=0)
    zero_state = jnp.zeros((HEAD, HEAD, BATCH * HEADS), F32)
    shift_p, wkv_p, re_p, im_p, shift_s, wkv_s, re_s, im_s = ([] for _ in range(8))
    w_glu = s5_w_glu.astype(BF16)
    s5p = _s5_params(s5_lam_re, s5_lam_im, s5_log_step, s5_b_re, s5_b_im, s5_c_re, s5_c_im, s5_d)
    w_main, w_tail, mu_pad = _in_proj_tail(w_in, shift_mu)

    for l in range(DEPTH):
        x = _ffn(x, ffn1_norm[l], ffn1_w_gate, ffn1_w_up, ffn1_w_down, l)

        h_mix = _rmsnorm(x, mix_norm[l], BF16, ROW_TM)
        last8 = _rmsnorm(x, mix_norm[l], F32, SUBLANES, SEQ - SUBLANES, BATCH * SUBLANES, SEQ // SUBLANES)
        shift_p.append(last8[SUBLANES - 1::SUBLANES])
        shift_s.append(_rmsnorm(x, mix_norm[l], F32, DEC_BATCH, M_ALL - DEC_BATCH, DEC_BATCH))

        q_p = _win_prompt(h_mix, w_main, w_tail, mu_pad, l)
        q_s = _win_sample(jnp.concatenate([state_shift[l].astype(BF16), h_mix[M_PROMPT:]], axis=0),
                          w_main, w_tail, mu_pad, l)

        rw_args = (rw_w0[l], rw_a0[l], rw_k_k[l], rw_k_a[l], rw_r_k[l], rw_w_up[l], rw_a_up[l], rw_g_up[l])
        *scan_p, gate_p, bonus_p = _rwkv_pre(q_p, *rw_args, time_major=True, tm=LANES)
        scan_s, gate_s, bonus_s = _rwkv_pre(q_s, *rw_args, time_major=False)
        scan_p = [_to_scan_channel_major(a) for a in scan_p]
        scan_s = _to_scan(scan_s, 6 * DEC_SEQ).reshape(6, DEC_SEQ, HEAD, DEC_BATCH * HEADS)

        y5_p, hre_p, him_p = _s5_prompt(q_p, s5p, l)
        y5_s, hre_s, him_s = _s5_sample(q_s, state_ssm_re[l].reshape(DEC_BATCH, S5_G * S5_P),
                                        state_ssm_im[l].reshape(DEC_BATCH, S5_G * S5_P), s5p, l)
        re_p.append(hre_p.reshape(BATCH, S5_G, S5_P))
        im_p.append(him_p.reshape(BATCH, S5_G, S5_P))
        re_s.append(hre_s.reshape(DEC_BATCH, S5_G, S5_P))
        im_s.append(him_s.reshape(DEC_BATCH, S5_G, S5_P))
        y_s5 = _glu_norm(y5_p, w_glu, l, s5_b_glu[l], s5_out_norm[l], 0)
        y_s5 = _glu_norm(y5_s, w_glu, l, s5_b_glu[l], s5_out_norm[l], M_PROMPT, y_s5)

        y_p, s_p = _wkv_scan(scan_p, zero_state, y_s5, tc=WKV_CHUNK)
        y_s, s_s = _wkv_scan(scan_s, _state_to_scan(state_wkv[l]), y_s5, tc=DEC_SEQ)
        wkv_p.append(_state_from_scan(s_p, BATCH))
        wkv_s.append(_state_from_scan(s_s, DEC_BATCH))
        y_rw = _rwkv_post(_from_scan(y_p, SEQ), bonus_p, gate_p, rw_lnx_w[l], rw_lnx_b[l], 0, True)
        y_rw = _rwkv_post(_from_scan(y_s, DEC_SEQ).reshape(M_SAMPLE, RW), bonus_s, gate_s,
                          rw_lnx_w[l], rw_lnx_b[l], M_PROMPT, False, y_rw)

        x = _res_matmul([y_rw, y_s5], w_out, l, x, 1.0, tm=1088, tn=512, vmem_mib=56, cast_w=True)
        x = _ffn(x, ffn2_norm[l], ffn2_w_gate, ffn2_w_up, ffn2_w_down, l)

    y_prompt = _rmsnorm(x, final_norm, F32, ROW_TM, 0, M_PROMPT).reshape(BATCH, SEQ, D_MODEL)
    y_sample = _rmsnorm(x, final_norm, F32, ROW_TM, M_PROMPT, M_SAMPLE).reshape(
        DEC_SEQ, DEC_BATCH, D_MODEL).transpose(1, 0, 2)
    st = jnp.stack
    return (y_prompt, y_sample, st(shift_p), st(wkv_p), st(re_p), st(im_p),
            st(shift_s), st(wkv_s), st(re_s), st(im_s))
```

```python
import functools
import math

import jax
import jax.numpy as jnp
from jax import lax
from jax.experimental import pallas as pl
from jax.experimental.pallas import tpu as pltpu

F32 = jnp.float32
BF16 = jnp.bfloat16

D_MODEL = 4096
BATCH = 4
SEQ = 2048
DEPTH = 2
DEC_BATCH = 128
DEC_SEQ = 4
M_PROMPT = BATCH * SEQ
M_SAMPLE = DEC_BATCH * DEC_SEQ
M_ALL = M_PROMPT + M_SAMPLE

RW = D_MODEL // 2
HEAD = 64
HEADS = RW // HEAD
S5W = D_MODEL - RW
S5_C = 16
S5_G = S5W // S5_C
S5_P = 64
W_LORA = 96
A_LORA = 96
G_LORA = 256
LORA_PAD = 128
LORA_W = 2 * LORA_PAD + G_LORA
D_FF = 11008
EPS_RMS = 1e-6
EPS_GN = 64e-5

COL_LORA = 3 * RW
COL_S5 = COL_LORA + LORA_W
D_INP = COL_S5 + S5W

V7X_VMEM_BYTES = 64 * 1024 * 1024
LANES = 128
SUBLANES = 8

S5_GT = 8
S5_TILES = S5_G // S5_GT
S5_UW = S5_GT * S5_C
S5_HW = S5_GT * S5_P

ROW_TM = 512
PRE_TM = 256


def _cparams(semantics, vmem_mib):
    assert vmem_mib * 1024 * 1024 < V7X_VMEM_BYTES
    return pltpu.CompilerParams(dimension_semantics=semantics,
                                vmem_limit_bytes=vmem_mib * 1024 * 1024)


def _joined_out(in_specs, args, joined):
    if joined is None:
        return in_specs, args, {}
    return (in_specs + [pl.BlockSpec(memory_space=pl.ANY)], args + [joined], {len(args): 0})


def _rmsnorm_kernel(x_ref, g_ref, o_ref):
    x = x_ref[...]
    ms = jnp.mean(x * x, axis=-1, keepdims=True)
    o_ref[...] = (x * lax.rsqrt(ms + EPS_RMS) * g_ref[...]).astype(o_ref.dtype)


def _rmsnorm(x, g, out_dtype, tm, row0=0, rows=None, block_stride=1):
    d = x.shape[1]
    m = x.shape[0] if rows is None else rows
    return pl.pallas_call(
        _rmsnorm_kernel,
        grid=(m // tm,),
        in_specs=[pl.BlockSpec((tm, d), lambda i: (i * block_stride + row0 // tm, 0)),
                  pl.BlockSpec((1, d), lambda i: (0, 0))],
        out_specs=pl.BlockSpec((tm, d), lambda i: (i, 0)),
        out_shape=jax.ShapeDtypeStruct((m, d), out_dtype),
        compiler_params=_cparams(("parallel",), 40),
        name="rmsnorm",
    )(x, g.reshape(1, d))


WD_CAST_ROWS = 128


def _gate_up_kernel(h_ref, wg_ref, wu_ref, wd_ref, o_ref, wd_bf16_ref, *, cast_passes):
    h = h_ref[...]
    a = jnp.dot(h, wg_ref[...].astype(BF16), preferred_element_type=F32)
    b = jnp.dot(h, wu_ref[...].astype(BF16), preferred_element_type=F32)
    o_ref[...] = (a * jax.nn.sigmoid(a) * b).astype(o_ref.dtype)

    @pl.when(pl.program_id(0) < cast_passes)
    def _():
        wd_bf16_ref[...] = wd_ref[...].astype(BF16)


def _gate_up(h, wg, wu, wd, layer, tm=2176, tn=256):
    m, d = h.shape
    f = wg.shape[2]
    n_i, n_j = m // tm, f // tn
    cast_blocks = f // WD_CAST_ROWS
    cast_passes = cast_blocks // n_j
    assert cast_passes * n_j == cast_blocks and cast_passes <= n_i

    def cast_block(i, j):
        return jnp.where(i < cast_passes, i * n_j + j, cast_blocks - 1)

    w_spec = pl.BlockSpec((None, d, tn), lambda i, j: (layer, 0, j))
    return pl.pallas_call(
        functools.partial(_gate_up_kernel, cast_passes=cast_passes),
        grid=(n_i, n_j),
        in_specs=[pl.BlockSpec((tm, d), lambda i, j: (i, 0), pipeline_mode=pl.Buffered(1)), w_spec, w_spec,
                  pl.BlockSpec((None, WD_CAST_ROWS, d), lambda i, j: (layer, cast_block(i, j), 0))],
        out_specs=[pl.BlockSpec((tm, tn), lambda i, j: (i, j)),
                   pl.BlockSpec((WD_CAST_ROWS, d), lambda i, j: (cast_block(i, j), 0))],
        out_shape=[jax.ShapeDtypeStruct((m, f), BF16), jax.ShapeDtypeStruct((f, d), BF16)],
        compiler_params=_cparams(("arbitrary", "arbitrary"), 58),
        name="ffn_gate_up",
    )(h, wg, wu, wd)


def _res_matmul_kernel(*refs, n_pairs, scale, cast_w):
    a_refs = refs[:n_pairs]
    w_refs = refs[n_pairs:2 * n_pairs]
    x_ref, o_ref = refs[2 * n_pairs], refs[2 * n_pairs + 1]
    load_w = (lambda ref: ref[...].astype(BF16)) if cast_w else (lambda ref: ref[...])
    acc = jnp.dot(a_refs[0][...], load_w(w_refs[0]), preferred_element_type=F32)
    for a_ref, w_ref in zip(a_refs[1:], w_refs[1:]):
        acc = acc + jnp.dot(a_ref[...], load_w(w_ref), preferred_element_type=F32)
    if scale != 1.0:
        acc = scale * acc
    o_ref[...] = x_ref[...] + acc


def _res_matmul(a_list, w, layer, x, scale, tm, tn, vmem_mib, cast_w=False):
    m, n = x.shape
    n_pairs = len(a_list)
    in_specs = [pl.BlockSpec((tm, a.shape[1]), lambda i, j: (i, 0)) for a in a_list]
    for p, a in enumerate(a_list):
        if layer is None:
            in_specs.append(pl.BlockSpec((a.shape[1], tn), lambda i, j, p=p: (p, j)))
        else:
            in_specs.append(pl.BlockSpec((None, a.shape[1], tn), lambda i, j, p=p: (layer, p, j)))
    in_specs.append(pl.BlockSpec((tm, tn), lambda i, j: (i, j)))
    assert sum(a.shape[1] for a in a_list) == w.shape[-2]
    return pl.pallas_call(
        functools.partial(_res_matmul_kernel, n_pairs=n_pairs, scale=scale, cast_w=cast_w),
        grid=(m // tm, n // tn),
        in_specs=in_specs,
        out_specs=pl.BlockSpec((tm, tn), lambda i, j: (i, j)),
        out_shape=jax.ShapeDtypeStruct((m, n), F32),
        compiler_params=_cparams(("parallel", "arbitrary"), vmem_mib),
        name="res_matmul",
    )(*a_list, *([w] * n_pairs), x)


def _ffn(x, norm_g, wg, wu, wd, layer):
    h = _rmsnorm(x, norm_g, BF16, ROW_TM)
    act, wd_bf16 = _gate_up(h, wg, wu, wd, layer)
    return _res_matmul([act], wd_bf16, None, x, 0.5, tm=544, tn=512, vmem_mib=58)


WIN_TN = 512
WIN_MAIN_TILES = COL_LORA // WIN_TN
WIN_SHIFT_TILES = COL_S5 // WIN_TN


def _win_tile_cases(h_ref, wm_ref, wt_ref, j, emit):
    @pl.when(j < WIN_MAIN_TILES)
    def _():
        emit(jnp.dot(h_ref[...], wm_ref[...], preferred_element_type=F32), True)

    @pl.when((j >= WIN_MAIN_TILES) & (j < WIN_SHIFT_TILES))
    def _():
        emit(jnp.dot(h_ref[...], wt_ref[...], preferred_element_type=F32), True)

    @pl.when(j >= WIN_SHIFT_TILES)
    def _():
        emit(jnp.dot(h_ref[...], wt_ref[...], preferred_element_type=F32), False)


def _win_prompt_kernel(h_ref, wm_ref, wt_ref, mu_ref, o_ref):
    def emit(p, shifted):
        if shifted:
            rows = lax.broadcasted_iota(jnp.int32, p.shape, 0)
            prev = jnp.where(rows == 0, 0.0, pltpu.roll(p, 1, 0))
            o_ref[...] = p + (prev - p) * mu_ref[...]
        else:
            o_ref[...] = p

    _win_tile_cases(h_ref, wm_ref, wt_ref, pl.program_id(1), emit)


def _win_sample_kernel(h_ref, wm_ref, wt_ref, mu_ref, o_ref):
    def emit(p, shifted):
        cur = p[DEC_BATCH:]
        o_ref[...] = cur + (p[:M_SAMPLE] - cur) * mu_ref[...] if shifted else cur

    _win_tile_cases(h_ref, wm_ref, wt_ref, pl.program_id(0), emit)


def _win_weight_specs(layer, col_of):
    main = pl.BlockSpec((None, D_MODEL, WIN_TN),
                        lambda *g: (layer, 0, jnp.minimum(col_of(*g), WIN_MAIN_TILES - 1)))
    tail = pl.BlockSpec((None, D_MODEL, WIN_TN),
                        lambda *g: (layer, 0, jnp.maximum(col_of(*g) - WIN_MAIN_TILES, 0)))
    mu = pl.BlockSpec((None, 1, WIN_TN), lambda *g: (layer, 0, col_of(*g)))
    return [main, tail, mu]


def _win_prompt(h, w_main, w_tail, mu_pad, layer):
    tn = WIN_TN
    return pl.pallas_call(
        _win_prompt_kernel,
        grid=(BATCH, D_INP // tn),
        in_specs=[pl.BlockSpec((SEQ, D_MODEL), lambda b, j: (b, 0), pipeline_mode=pl.Buffered(1))]
                 + _win_weight_specs(layer, lambda b, j: j),
        out_specs=pl.BlockSpec((SEQ, tn), lambda b, j: (b, j)),
        out_shape=jax.ShapeDtypeStruct((M_PROMPT, D_INP), F32),
        compiler_params=_cparams(("parallel", "arbitrary"), 60),
        name="in_proj_prompt",
    )(h, w_main, w_tail, mu_pad)


def _win_sample(h_rows, w_main, w_tail, mu_pad, layer):
    tn = WIN_TN
    rows = h_rows.shape[0]
    return pl.pallas_call(
        _win_sample_kernel,
        grid=(D_INP // tn,),
        in_specs=[pl.BlockSpec((rows, D_MODEL), lambda j: (0, 0))] + _win_weight_specs(layer, lambda j: j),
        out_specs=pl.BlockSpec((M_SAMPLE, tn), lambda j: (0, j)),
        out_shape=jax.ShapeDtypeStruct((M_SAMPLE, D_INP), F32),
        compiler_params=_cparams(("arbitrary",), 48),
        name="in_proj_sample",
    )(h_rows, w_main, w_tail, mu_pad)


def _in_proj_tail(w_in, mu):
    o_w, o_a, o_g = COL_LORA, COL_LORA + W_LORA, COL_LORA + W_LORA + A_LORA
    o_s = o_g + G_LORA
    w_main = w_in.astype(BF16)
    zw = jnp.zeros((DEPTH, D_MODEL, LORA_PAD - W_LORA), BF16)
    w_tail = jnp.concatenate([w_main[:, :, o_w:o_a], zw, w_main[:, :, o_a:o_g], zw,
                              w_main[:, :, o_g:]], axis=2)
    zm = jnp.zeros((DEPTH, LORA_PAD - W_LORA), mu.dtype)
    mu_pad = jnp.concatenate([mu[:, :o_w], mu[:, o_w:o_a], zm, mu[:, o_a:o_g], zm, mu[:, o_g:o_s],
                              jnp.zeros((DEPTH, S5W), mu.dtype)], axis=1).reshape(DEPTH, 1, D_INP)
    return w_main, w_tail, mu_pad


def _head_sum(x):
    r = lax.broadcasted_iota(jnp.int32, (LANES, LANES), 0) // HEAD
    c = lax.broadcasted_iota(jnp.int32, (LANES, LANES), 1) // HEAD
    ones = (r == c).astype(BF16)
    hi = x.astype(BF16)
    r1 = x - hi.astype(F32)
    mid = r1.astype(BF16)
    lo = (r1 - mid.astype(F32)).astype(BF16)
    outs = []
    for s in range(x.shape[1] // LANES):
        sl = slice(s * LANES, (s + 1) * LANES)
        acc = jnp.dot(hi[:, sl], ones, preferred_element_type=F32)
        acc = acc + jnp.dot(mid[:, sl], ones, preferred_element_type=F32)
        acc = acc + jnp.dot(lo[:, sl], ones, preferred_element_type=F32)
        outs.append(acc)
    return jnp.concatenate(outs, axis=1)


def _rwkv_pre_kernel(r_ref, k_ref, v_ref, lora_ref, w0_ref, a0_ref, kk_ref, ka_ref, rk_ref,
                     wup_ref, aup_ref, gup_ref, *outs, channel_major):
    *scan_outs, g_out, bonus_out = outs
    r_out, w_out, k_out, v_out, kk_out, b_out = range(6)

    def emit(idx, val):
        if channel_major:
            scan_outs[idx][...] = val.T
        else:
            scan_outs[0][idx] = val

    r = r_ref[...]
    k = k_ref[...]
    v = v_ref[...]
    emit(r_out, r)
    emit(v_out, v)
    lora = lora_ref[...]
    wd = lora[:, :LORA_PAD]
    ad = lora[:, LORA_PAD:2 * LORA_PAD]
    gd = lora[:, 2 * LORA_PAD:]
    z = -(w0_ref[...] + jnp.dot(jnp.tanh(wd).astype(BF16), wup_ref[...], preferred_element_type=F32))
    softplus = jnp.maximum(z, 0.0) + jnp.log1p(jnp.exp(-jnp.abs(z)))
    emit(w_out, jnp.exp(-jnp.exp(-softplus - 0.5)))
    a = jax.nn.sigmoid(a0_ref[...] + jnp.dot(ad.astype(BF16), aup_ref[...], preferred_element_type=F32))
    g_out[...] = jnp.dot(jax.nn.sigmoid(gd).astype(BF16), gup_ref[...], preferred_element_type=F32)
    kk = k * kk_ref[...]
    k2 = k * (1.0 + (a - 1.0) * ka_ref[...])
    emit(k_out, k2)
    kkn = kk / jnp.maximum(jnp.sqrt(_head_sum(kk * kk)), 1e-12)
    emit(kk_out, kkn)
    emit(b_out, kkn * a)
    bonus_out[...] = _head_sum(r * k2 * rk_ref[...]) * v


def _time_major_spec(m, tm, time_major):
    if not time_major:
        return pl.BlockSpec((tm, RW), lambda i: (i, 0)), (m, RW)
    per_seq = SEQ // tm
    return pl.BlockSpec((tm, RW), lambda i: (i % per_seq, i // per_seq)), (SEQ, (m // SEQ) * RW)


def _rwkv_pre(q, w0, a0, k_k, k_a, r_k, w_up, a_up, g_up, time_major, tm=PRE_TM):
    m = q.shape[0]
    out_spec, out_dims = _time_major_spec(m, tm, time_major)
    if time_major:
        per_seq = SEQ // tm
        scan_specs = [pl.BlockSpec((RW, tm), lambda i: (i // per_seq, i % per_seq))] * 6
        scan_outs = [jax.ShapeDtypeStruct(((m // SEQ) * RW, SEQ), F32)] * 6
    else:
        scan_specs = [pl.BlockSpec((6, tm, RW), lambda i: (0, i, 0))]
        scan_outs = [jax.ShapeDtypeStruct((6, m, RW), F32)]
    row = lambda v: v.reshape(1, RW)
    pad_rows = lambda u: jnp.concatenate(
        [u, jnp.zeros((LORA_PAD - u.shape[0], RW), u.dtype)], axis=0).astype(BF16)
    vec = pl.BlockSpec((1, RW), lambda i: (0, 0))
    full = lambda rows: pl.BlockSpec((rows, RW), lambda i: (0, 0))
    out = jax.ShapeDtypeStruct(out_dims, F32)
    return pl.pallas_call(
        functools.partial(_rwkv_pre_kernel, channel_major=time_major),
        grid=(m // tm,),
        in_specs=[pl.BlockSpec((tm, RW), lambda i: (i, 0)),
                  pl.BlockSpec((tm, RW), lambda i: (i, 1)),
                  pl.BlockSpec((tm, RW), lambda i: (i, 2)),
                  pl.BlockSpec((tm, LORA_W), lambda i: (i, COL_LORA // LORA_W)),
                  vec, vec, vec, vec, vec, full(LORA_PAD), full(LORA_PAD), full(G_LORA)],
        out_specs=scan_specs + [out_spec] * 2,
        out_shape=scan_outs + [out] * 2,
        compiler_params=_cparams(("parallel",), 56),
        name="rwkv_pre",
    )(q, q, q, q, row(w0), row(a0), row(k_k), row(k_a), row(r_k.reshape(RW)),
      pad_rows(w_up), pad_rows(a_up), g_up.astype(BF16))


def _rwkv_post_kernel(y_ref, bonus_ref, g_ref, lw_ref, lb_ref, *rest, channel_major):
    o_ref = rest[-1]
    y = y_ref[...].T if channel_major else y_ref[...]
    mean = _head_sum(y) * (1.0 / HEAD)
    c = y - mean
    var = _head_sum(c * c) * (1.0 / HEAD)
    yn = c * lax.rsqrt(var + EPS_GN) * lw_ref[...] + lb_ref[...]
    o_ref[...] = ((yn + bonus_ref[...]) * g_ref[...]).astype(o_ref.dtype)


def _rwkv_post(y, bonus, g, lnx_w, lnx_b, row0, time_major, joined=None, tm=ROW_TM):
    m = y.size // RW
    blk, _ = _time_major_spec(m, tm, time_major)
    y_blk = blk
    if time_major:
        per_seq = SEQ // tm
        y_blk = pl.BlockSpec((RW, tm), lambda i: (i // per_seq, i % per_seq))
    vec = pl.BlockSpec((1, RW), lambda i: (0, 0))
    in_specs, args, aliases = _joined_out(
        [y_blk, blk, blk, vec, vec], [y, bonus, g, lnx_w.reshape(1, RW), lnx_b.reshape(1, RW)], joined)
    return pl.pallas_call(
        functools.partial(_rwkv_post_kernel, channel_major=time_major),
        grid=(m // tm,),
        in_specs=in_specs,
        out_specs=pl.BlockSpec((tm, RW), lambda i: (i + row0 // tm, 0)),
        out_shape=jax.ShapeDtypeStruct((M_ALL, RW), BF16),
        input_output_aliases=aliases,
        compiler_params=_cparams(("parallel",), 56),
        name="rwkv_post",
    )(*args)


WKV_UNROLL = 8
WKV_CHUNK = 64


def _wkv_kernel(r_ref, w_ref, k_ref, v_ref, kk_ref, b_ref, s0_ref, after_ref, y_ref, s_ref,
                g_ref, kq_ref, wr_ref, bt_ref, kt_ref, *, steps):
    @pl.when(pl.program_id(1) == 0)
    def _():
        s_ref[...] = s0_ref[...]

    g_ref[...] = jnp.ones((HEAD, LANES), F32)

    def step(t, carry):
        r, k, b = r_ref[t], k_ref[t], b_ref[t]
        g_prev = g_ref[...]
        g = g_prev * w_ref[t]
        g_inv = 1.0 / g
        g_ref[...] = g
        kq_ref[...] = g_prev * kk_ref[t]
        wr_ref[...] = g * r
        bt_ref[...] = b * g_inv
        kt_ref[...] = k * g_inv
        beta = jnp.sum(b * r, axis=0, keepdims=True)
        kappa = jnp.sum(k * r, axis=0, keepdims=True)
        v = v_ref[t]

        def contract(j, acc):
            sa, u = acc
            sj = s_ref[j]
            return sa - sj * kq_ref[pl.ds(j, 1), :], u + sj * wr_ref[pl.ds(j, 1), :]

        zero = jnp.zeros((HEAD, LANES), F32)
        sa, u = lax.fori_loop(0, HEAD, contract, (zero, zero), unroll=WKV_UNROLL)

        def update(j, c):
            s_ref[j] = s_ref[j] + sa * bt_ref[pl.ds(j, 1), :] + v * kt_ref[pl.ds(j, 1), :]
            return c

        lax.fori_loop(0, HEAD, update, 0, unroll=WKV_UNROLL)
        y_ref[t] = u + sa * beta + v * kappa
        return carry

    lax.fori_loop(0, steps, step, 0)

    def denormalise(j, c):
        s_ref[j] = s_ref[j] * g_ref[pl.ds(j, 1), :]
        return c

    lax.fori_loop(0, HEAD, denormalise, 0, unroll=WKV_UNROLL)


def _wkv_scan(vectors, s0, after, tc):
    vec = pl.BlockSpec((tc, HEAD, LANES), lambda c, t: (t, 0, c))
    if isinstance(vectors, (list, tuple)):
        r, w, k, v, kk, b = vectors
        vec_specs = [vec] * 6
    else:
        r = w = k = v = kk = b = vectors
        vec_specs = [pl.BlockSpec((None, tc, HEAD, LANES), lambda c, t, i=i: (i, t, 0, c)) for i in range(6)]
    length, _, n = r.shape[-3:]
    st = pl.BlockSpec((HEAD, HEAD, LANES), lambda c, t: (0, 0, c))
    return pl.pallas_call(
        functools.partial(_wkv_kernel, steps=tc),
        grid=(n // LANES, length // tc),
        in_specs=vec_specs + [st, pl.BlockSpec(memory_space=pl.ANY)],
        out_specs=[vec, st],
        out_shape=[jax.ShapeDtypeStruct((length, HEAD, n), F32),
                   jax.ShapeDtypeStruct((HEAD, HEAD, n), F32)],
        scratch_shapes=[pltpu.VMEM((HEAD, LANES), F32)] * 5,
        compiler_params=_cparams(("parallel", "arbitrary"), 40),
        name="wkv_scan",
    )(r, w, k, v, kk, b, s0, after)


def _to_scan(a, length):
    return a.reshape(length, -1, HEAD).transpose(0, 2, 1)


def _to_scan_channel_major(a):
    return a.reshape(-1, HEAD, a.shape[1]).transpose(2, 1, 0)


def _from_scan_channel_major(y):
    return y.transpose(2, 1, 0).reshape(-1, y.shape[0])


def _from_scan(y, length):
    return y.transpose(0, 2, 1).reshape(length, -1)


def _state_to_scan(s):
    n = s.shape[0] * s.shape[1]
    s = lax.optimization_barrier(s.reshape(n, HEAD, HEAD).transpose(0, 2, 1))
    return s.reshape(n, HEAD * HEAD).T.reshape(HEAD, HEAD, n)


def _state_from_scan(s, batch):
    n = s.shape[2]
    s = lax.optimization_barrier(s.reshape(HEAD * HEAD, n).T)
    return s.reshape(n, HEAD, HEAD).transpose(0, 2, 1).reshape(batch, HEADS, HEAD, HEAD)


def _s5_discretize(lre_ref, lim_ref, ls_ref, bre_ref, bim_ref):
    lre, lim = lre_ref[...], lim_ref[...]
    step = jnp.exp(ls_ref[...])
    mag = jnp.exp(lre * step)
    ar = mag * jnp.cos(lim * step)
    ai = mag * jnp.sin(lim * step)
    den = lre * lre + lim * lim
    nr = ar - 1.0
    f_re = (nr * lre + ai * lim) / den
    f_im = (ai * lre - nr * lim) / den
    bre, bim = bre_ref[...], bim_ref[...]
    return ar, ai, f_re * bre - f_im * bim, f_re * bim + f_im * bre


def _dot_bf16(a, b):
    return jnp.dot(a.astype(BF16), b.astype(BF16), preferred_element_type=F32)


def _s5_readout(hr, hi, cre_ref, cim_ref, d_ref, u):
    return _dot_bf16(hr, cre_ref[...]) - _dot_bf16(hi, cim_ref[...]) + d_ref[...] * u


def _cmul(ar, ai, br, bi):
    return ar * br - ai * bi, ar * bi + ai * br


def _s5_prompt_kernel(u_ref, lre_ref, lim_ref, ls_ref, bre_ref, bim_ref, cre_ref, cim_ref, d_ref,
                      y_ref, hre_ref, him_ref, hr_scr, hi_scr):
    ar, ai, bbr, bbi = _s5_discretize(lre_ref, lim_ref, ls_ref, bre_ref, bim_ref)
    u = u_ref[...]
    hr_scr[...] = _dot_bf16(u, bbr)
    hi_scr[...] = _dot_bf16(u, bbi)

    rows = lax.broadcasted_iota(jnp.int32, (SUBLANES, S5_HW), 0)
    powers = [(ar, ai)]
    for _ in range(SUBLANES - 1):
        powers.append(_cmul(*powers[-1], ar, ai))
    levels = []
    for sh in (1, 2, 4):
        pr, pi = powers[sh - 1]
        levels.append((sh, jnp.where(rows >= sh, pr, 0.0), jnp.where(rows >= sh, pi, 0.0)))
    cpr = jnp.zeros((SUBLANES, S5_HW), F32)
    cpi = jnp.zeros((SUBLANES, S5_HW), F32)
    for n, (pr, pi) in enumerate(powers):
        cpr = jnp.where(rows == n, pr, cpr)
        cpi = jnp.where(rows == n, pi, cpi)

    def tile(i, carry):
        cr, ci = carry
        r0 = pl.multiple_of(i * SUBLANES, SUBLANES)
        xr = hr_scr[pl.ds(r0, SUBLANES), :]
        xi = hi_scr[pl.ds(r0, SUBLANES), :]
        for sh, mr, mi in levels:
            sr, si = pltpu.roll(xr, sh, 0), pltpu.roll(xi, sh, 0)
            xr, xi = xr + (mr * sr - mi * si), xi + (mr * si + mi * sr)
        xr = xr + (cpr * cr - cpi * ci)
        xi = xi + (cpr * ci + cpi * cr)
        hr_scr[pl.ds(r0, SUBLANES), :] = xr
        hi_scr[pl.ds(r0, SUBLANES), :] = xi
        last = SUBLANES - 1
        return (jnp.broadcast_to(xr[last:, :], (SUBLANES, S5_HW)),
                jnp.broadcast_to(xi[last:, :], (SUBLANES, S5_HW)))

    zero = jnp.zeros((SUBLANES, S5_HW), F32)
    cr, ci = lax.fori_loop(0, SEQ // SUBLANES, tile, (zero, zero), unroll=4)
    hre_ref[...] = cr[:1]
    him_ref[...] = ci[:1]
    y_ref[...] = _s5_readout(hr_scr[...], hi_scr[...], cre_ref, cim_ref, d_ref, u)


def _s5_sample_kernel(u_ref, h0r_ref, h0i_ref, lre_ref, lim_ref, ls_ref, bre_ref, bim_ref, cre_ref, cim_ref,
                      d_ref, y_ref, hre_ref, him_ref, hr_scr, hi_scr):
    ar, ai, bbr, bbi = _s5_discretize(lre_ref, lim_ref, ls_ref, bre_ref, bim_ref)
    u = u_ref[...]
    bu_r = _dot_bf16(u, bbr)
    bu_i = _dot_bf16(u, bbi)
    hr, hi = h0r_ref[...], h0i_ref[...]
    for t in range(DEC_SEQ):
        rows = slice(t * DEC_BATCH, (t + 1) * DEC_BATCH)
        hr, hi = ar * hr - ai * hi + bu_r[rows], ar * hi + ai * hr + bu_i[rows]
        hr_scr[rows, :] = hr
        hi_scr[rows, :] = hi
    hre_ref[...] = hr
    him_ref[...] = hi
    y_ref[...] = _s5_readout(hr_scr[...], hi_scr[...], cre_ref, cim_ref, d_ref, u)


def _s5_params(lam_re, lam_im, log_step, b_re, b_im, c_re, c_im, d):
    tiles = DEPTH * S5_TILES
    chan = lambda a: a.reshape(tiles, 1, S5_HW)

    def block_diag(x, rows_per_group, cols_per_group):
        tiled = jnp.tile(x, (1, 1, S5_GT))
        r = lax.broadcasted_iota(jnp.int32, tiled.shape[1:], 0) // rows_per_group
        c = lax.broadcasted_iota(jnp.int32, tiled.shape[1:], 1) // cols_per_group
        return jnp.where(r == c, tiled, 0.0)

    def b_blk(b):
        return block_diag(b.transpose(0, 1, 3, 2).reshape(tiles, S5_UW, S5_P), S5_C, S5_P)

    def c_blk(c):
        return block_diag(c.transpose(0, 1, 3, 2).reshape(tiles, S5_HW, S5_C), S5_P, S5_C)

    ls = jnp.broadcast_to(log_step[:, :, None], (DEPTH, S5_G, S5_P))
    return (chan(lam_re), chan(lam_im), chan(ls), b_blk(b_re), b_blk(b_im), c_blk(c_re), c_blk(c_im),
            d.reshape(tiles, 1, S5_UW))


def _s5_param_specs(tile_of):
    chan = pl.BlockSpec((None, 1, S5_HW), lambda *g: (tile_of(*g), 0, 0))
    bspec = pl.BlockSpec((None, S5_UW, S5_HW), lambda *g: (tile_of(*g), 0, 0))
    cspec = pl.BlockSpec((None, S5_HW, S5_UW), lambda *g: (tile_of(*g), 0, 0))
    dspec = pl.BlockSpec((None, 1, S5_UW), lambda *g: (tile_of(*g), 0, 0))
    return [chan, chan, chan, bspec, bspec, cspec, cspec, dspec]


def _s5_prompt(q, params, layer):
    col0 = COL_S5 // S5_UW
    state = jax.ShapeDtypeStruct((BATCH, 1, S5_G * S5_P), F32)
    st_spec = pl.BlockSpec((None, 1, S5_HW), lambda b, g: (b, 0, g))
    return pl.pallas_call(
        _s5_prompt_kernel,
        grid=(BATCH, S5_TILES),
        in_specs=[pl.BlockSpec((SEQ, S5_UW), lambda b, g: (b, col0 + g))]
                 + _s5_param_specs(lambda b, g: layer * S5_TILES + g),
        out_specs=[pl.BlockSpec((SEQ, S5_UW), lambda b, g: (b, g)), st_spec, st_spec],
        out_shape=[jax.ShapeDtypeStruct((M_PROMPT, S5W), F32), state, state],
        scratch_shapes=[pltpu.VMEM((SEQ, S5_HW), F32), pltpu.VMEM((SEQ, S5_HW), F32)],
        compiler_params=_cparams(("parallel", "parallel"), 48),
        name="s5_prompt",
    )(q, *params)


def _s5_sample(q, h0_re, h0_im, params, layer):
    col0 = COL_S5 // S5_UW
    state = jax.ShapeDtypeStruct((DEC_BATCH, S5_G * S5_P), F32)
    st_spec = pl.BlockSpec((DEC_BATCH, S5_HW), lambda g: (0, g))
    return pl.pallas_call(
        _s5_sample_kernel,
        grid=(S5_TILES,),
        in_specs=[pl.BlockSpec((M_SAMPLE, S5_UW), lambda g: (0, col0 + g)), st_spec, st_spec]
                 + _s5_param_specs(lambda g: layer * S5_TILES + g),
        out_specs=[pl.BlockSpec((M_SAMPLE, S5_UW), lambda g: (0, g)), st_spec, st_spec],
        out_shape=[jax.ShapeDtypeStruct((M_SAMPLE, S5W), F32), state, state],
        scratch_shapes=[pltpu.VMEM((M_SAMPLE, S5_HW), F32), pltpu.VMEM((M_SAMPLE, S5_HW), F32)],
        compiler_params=_cparams(("parallel",), 32),
        name="s5_sample",
    )(q, h0_re, h0_im, *params)


def _glu_norm_kernel(y_ref, w_ref, b_ref, g_ref, *rest):
    o_ref = rest[-1]
    y = y_ref[...]
    z = 0.5 * y * (1.0 + lax.erf(y * math.sqrt(0.5)))
    gate = jax.nn.sigmoid(jnp.dot(z.astype(BF16), w_ref[...], preferred_element_type=F32) + b_ref[...])
    out = z * gate
    ms = jnp.mean(out * out, axis=-1, keepdims=True)
    o_ref[...] = (out * lax.rsqrt(ms + EPS_RMS) * g_ref[...]).astype(o_ref.dtype)


def _glu_norm(y, w_glu_bf16, layer, b_glu, gain, row0, joined=None, tm=ROW_TM):
    m = y.shape[0]
    vec = pl.BlockSpec((1, S5W), lambda i: (0, 0))
    in_specs, args, aliases = _joined_out(
        [pl.BlockSpec((tm, S5W), lambda i: (i, 0)),
         pl.BlockSpec((None, S5W, S5W), lambda i: (layer, 0, 0)), vec, vec],
        [y, w_glu_bf16, b_glu.reshape(1, S5W), gain.reshape(1, S5W)], joined)
    return pl.pallas_call(
        _glu_norm_kernel,
        grid=(m // tm,),
        in_specs=in_specs,
        out_specs=pl.BlockSpec((tm, S5W), lambda i: (i + row0 // tm, 0)),
        out_shape=jax.ShapeDtypeStruct((M_ALL, S5W), BF16),
        input_output_aliases=aliases,
        compiler_params=_cparams(("parallel",), 56),
        name="s5_glu_norm",
    )(*args)


def kernel(x_prompt, x_sample, state_shift, state_wkv, state_ssm_re, state_ssm_im, ffn1_norm, ffn1_w_gate, ffn1_w_up, ffn1_w_down, mix_norm, w_in, shift_mu, rw_w0, rw_w_up, rw_a0, rw_a_up, rw_g_up, rw_k_k, rw_k_a, rw_r_k, rw_lnx_w, rw_lnx_b, s5_lam_re, s5_lam_im, s5_b_re, s5_b_im, s5_c_re, s5_c_im, s5_d, s5_log_step, s5_w_glu, s5_b_glu, s5_out_norm, w_out, ffn2_norm, ffn2_w_gate, ffn2_w_up, ffn2_w_down, final_norm):
    x = jnp.concatenate([x_prompt.reshape(M_PROMPT, D_MODEL),
                         x_sample.transpose(1, 0, 2).reshape(M_SAMPLE, D_MODEL)], axis=0)
    zero_state = jnp.zeros((HEAD, HEAD, BATCH * HEADS), F32)
    shift_p, wkv_p, re_p, im_p, shift_s, wkv_s, re_s, im_s = ([] for _ in range(8))
    w_glu = s5_w_glu.astype(BF16)
    s5p = _s5_params(s5_lam_re, s5_lam_im, s5_log_step, s5_b_re, s5_b_im, s5_c_re, s5_c_im, s5_d)
    w_main, w_tail, mu_pad = _in_proj_tail(w_in, shift_mu)

    for l in range(DEPTH):
        x = _ffn(x, ffn1_norm[l], ffn1_w_gate, ffn1_w_up, ffn1_w_down, l)

        h_mix = _rmsnorm(x, mix_norm[l], BF16, ROW_TM)
        last8 = _rmsnorm(x, mix_norm[l], F32, SUBLANES, SEQ - SUBLANES, BATCH * SUBLANES, SEQ // SUBLANES)
        shift_p.append(last8[SUBLANES - 1::SUBLANES])
        shift_s.append(_rmsnorm(x, mix_norm[l], F32, DEC_BATCH, M_ALL - DEC_BATCH, DEC_BATCH))

        q_p = _win_prompt(h_mix, w_main, w_tail, mu_pad, l)
        q_s = _win_sample(jnp.concatenate([state_shift[l].astype(BF16), h_mix[M_PROMPT:]], axis=0),
                          w_main, w_tail, mu_pad, l)

        rw_args = (rw_w0[l], rw_a0[l], rw_k_k[l], rw_k_a[l], rw_r_k[l], rw_w_up[l], rw_a_up[l], rw_g_up[l])
        *scan_p, gate_p, bonus_p = _rwkv_pre(q_p, *rw_args, time_major=True, tm=LANES)
        scan_s, gate_s, bonus_s = _rwkv_pre(q_s, *rw_args, time_major=False)
        scan_p = [_to_scan_channel_major(a) for a in scan_p]
        scan_s = _to_scan(scan_s, 6 * DEC_SEQ).reshape(6, DEC_SEQ, HEAD, DEC_BATCH * HEADS)

        y5_p, hre_p, him_p = _s5_prompt(q_p, s5p, l)
        y5_s, hre_s, him_s = _s5_sample(q_s, state_ssm_re[l].reshape(DEC_BATCH, S5_G * S5_P),
                                        state_ssm_im[l].reshape(DEC_BATCH, S5_G * S5_P), s5p, l)
        re_p.append(hre_p.reshape(BATCH, S5_G, S5_P))
        im_p.append(him_p.reshape(BATCH, S5_G, S5_P))
        re_s.append(hre_s.reshape(DEC_BATCH, S5_G, S5_P))
        im_s.append(him_s.reshape(DEC_BATCH, S5_G, S5_P))
        y_s5 = _glu_norm(y5_p, w_glu, l, s5_b_glu[l], s5_out_norm[l], 0)
        y_s5 = _glu_norm(y5_s, w_glu, l, s5_b_glu[l], s5_out_norm[l], M_PROMPT, y_s5)

        y_p, s_p = _wkv_scan(scan_p, zero_state, y_s5, tc=WKV_CHUNK)
        y_s, s_s = _wkv_scan(scan_s, _state_to_scan(state_wkv[l]), y_s5, tc=DEC_SEQ)
        wkv_p.append(_state_from_scan(s_p, BATCH))
        wkv_s.append(_state_from_scan(s_s, DEC_BATCH))
        y_rw = _rwkv_post(_from_scan_channel_major(y_p), bonus_p, gate_p, rw_lnx_w[l], rw_lnx_b[l], 0, True)
        y_rw = _rwkv_post(_from_scan(y_s, DEC_SEQ).reshape(M_SAMPLE, RW), bonus_s, gate_s,
                          rw_lnx_w[l], rw_lnx_b[l], M_PROMPT, False, y_rw)

        x = _res_matmul([y_rw, y_s5], w_out, l, x, 1.0, tm=1088, tn=512, vmem_mib=56, cast_w=True)
        x = _ffn(x, ffn2_norm[l], ffn2_w_gate, ffn2_w_up, ffn2_w_down, l)

    y_prompt = _rmsnorm(x, final_norm, F32, ROW_TM, 0, M_PROMPT).reshape(BATCH, SEQ, D_MODEL)
    y_sample = _rmsnorm(x, final_norm, F32, ROW_TM, M_PROMPT, M_SAMPLE).reshape(
        DEC_SEQ, DEC_BATCH, D_MODEL).transpose(1, 0, 2)
    st = jnp.stack
    return (y_prompt, y_sample, st(shift_p), st(wkv_p), st(re_p), st(im_p),
            st(shift_s), st(wkv_s), st(re_s), st(im_s))
```
